```python
import jax, jax.numpy as jnp
from jax import lax
import numpy as np

D_MODEL = 2048
BATCH = 4
SEQ = 4096
DEPTH = 4

CHUNK = 64
Q_BLOCK = 128
EPS = 1e-6
NEG_INF = -1e30
ROPE_BASE = 10000.0

RET_HEADS = 8
RET_DK = 128
RET_DV = 256
RET_QK_W = RET_HEADS * RET_DK
RET_V_W = RET_HEADS * RET_DV

MLA_HEADS = 16
MLA_Q_RANK = 512
MLA_KV_RANK = 512
MLA_NOPE = 128
MLA_ROPE = 64
MLA_DV = 128
MLA_V_W = MLA_HEADS * MLA_DV

N_BRANCH = 2
IN_SPLITS = (RET_QK_W, RET_QK_W, RET_V_W, RET_V_W, MLA_Q_RANK, MLA_KV_RANK, MLA_ROPE, MLA_V_W, N_BRANCH * D_MODEL)
D_IN = 2 * RET_QK_W + 2 * RET_V_W + MLA_Q_RANK + MLA_KV_RANK + MLA_ROPE + MLA_V_W + N_BRANCH * D_MODEL

kernel_name = "hybrid_retention_mla_adaln_trunk"


def rms_norm(x, g):
    xf = x.astype(jnp.float32)
    y = xf * lax.rsqrt(jnp.mean(xf * xf, axis=-1, keepdims=True) + EPS)
    return (y * g.astype(jnp.float32)).astype(x.dtype)


def rope_tables(positions, dim):
    inv = 1.0 / (ROPE_BASE ** (jnp.arange(0, dim, 2, dtype=jnp.float32) / dim))
    ang = positions.astype(jnp.float32)[..., None] * inv
    return jnp.cos(ang), jnp.sin(ang)


def apply_rope(x, cos, sin):
    half = x.shape[-1] // 2
    x1, x2 = x[..., :half], x[..., half:]
    cos = cos.astype(x.dtype)
    sin = sin.astype(x.dtype)
    return jnp.concatenate([x1 * cos - x2 * sin, x1 * sin + x2 * cos], axis=-1)


def retention(q, k, v):
    B, S, H, dk = q.shape
    dv = v.shape[-1]
    nc = S // CHUNK
    f32 = jnp.float32
    log_gamma = jnp.log(1.0 - 2.0 ** (-5.0 - jnp.arange(H, dtype=f32)))
    qf = q.astype(f32).reshape(B, nc, CHUNK, H, dk)
    kf = (k.astype(f32) * (dk ** -0.5)).reshape(B, nc, CHUNK, H, dk)
    vf = v.astype(f32).reshape(B, nc, CHUNK, H, dv)
    idx = jnp.arange(CHUNK, dtype=f32)
    dmat = jnp.exp(jnp.abs(idx[:, None] - idx[None, :])[None] * log_gamma[:, None, None])
    scores = jnp.einsum('bnihd,bnjhd->bnhij', qf, kf) * dmat[None, None]
    o_intra = jnp.einsum('bnhij,bnjhe->bnihe', scores, vf)
    xi = jnp.exp((idx + 1.0)[:, None] * log_gamma[None, :])
    zeta = jnp.exp((CHUNK - 1.0 - idx)[:, None] * log_gamma[None, :])
    decay_chunk = jnp.exp(CHUNK * log_gamma)
    q_x = (qf * xi[None, None, :, :, None]).transpose(1, 0, 2, 3, 4)
    k_z = (kf * zeta[None, None, :, :, None]).transpose(1, 0, 2, 3, 4)
    v_t = vf.transpose(1, 0, 2, 3, 4)

    def step(state, inp):
        qc, kc, vc = inp
        o = jnp.einsum('bihd,bhde->bihe', qc, state)
        state = state * decay_chunk[None, :, None, None] + jnp.einsum('bjhd,bjhe->bhde', kc, vc)
        return state, o

    init = jnp.zeros((B, H, dk, dv), f32)
    _, o_cross = lax.scan(step, init, (q_x, k_z, v_t))
    o = o_intra + o_cross.transpose(1, 0, 2, 3, 4)
    return o.reshape(B, S, H, dv)


def head_group_norm(o):
    mu = jnp.mean(o, axis=-1, keepdims=True)
    var = jnp.mean(jnp.square(o - mu), axis=-1, keepdims=True)
    return (o - mu) * lax.rsqrt(var + EPS)


def mla_attention(q_nope, q_rope, k_nope, k_rope, v):
    B, S, H, _ = q_nope.shape
    nb = S // Q_BLOCK
    scale = (MLA_NOPE + MLA_ROPE) ** -0.5
    key_chunk = jnp.arange(S) // CHUNK

    def block(i):
        qs = i * Q_BLOCK
        qn = lax.dynamic_slice_in_dim(q_nope, qs, Q_BLOCK, axis=1)
        qr = lax.dynamic_slice_in_dim(q_rope, qs, Q_BLOCK, axis=1)
        s = (jnp.einsum('bqhd,bkhd->bhqk', qn, k_nope)
             + jnp.einsum('bqhd,bkd->bhqk', qr, k_rope)).astype(jnp.float32) * scale
        q_chunk = (qs + jnp.arange(Q_BLOCK)) // CHUNK
        mask = key_chunk[None, :] <= q_chunk[:, None]
        s = jnp.where(mask[None, None], s, NEG_INF)
        p = jax.nn.softmax(s, axis=-1)
        return jnp.einsum('bhqk,bkhd->bqhd', p.astype(v.dtype), v)

    out = lax.map(block, jnp.arange(nb))
    return out.transpose(1, 0, 2, 3, 4).reshape(B, S, H * v.shape[-1])


def hybrid_layer(x, c_act, cos_r, sin_r, cos_m, sin_m,
                 w_mod, b_mod, g_norm, w_in, g_cq, g_ckv, w_uq, w_ukv,
                 w_ret_proj, w_mla_proj, w_out):
    B, S, D = x.shape
    mod = c_act @ w_mod + b_mod
    shift, scale, gate = jnp.split(mod, 3, axis=-1)
    h = rms_norm(x, g_norm) * (1.0 + scale[:, None, :]) + shift[:, None, :]

    proj = h @ w_in
    points = np.cumsum(IN_SPLITS)[:-1].tolist()
    rq, rk, rv, rg, cq, ckv, kr, mg, bg = jnp.split(proj, points, axis=-1)

    rq = apply_rope(rq.reshape(B, S, RET_HEADS, RET_DK), cos_r[:, :, None], sin_r[:, :, None])
    rk = apply_rope(rk.reshape(B, S, RET_HEADS, RET_DK), cos_r[:, :, None], sin_r[:, :, None])
    rv = rv.reshape(B, S, RET_HEADS, RET_DV)
    o_ret = head_group_norm(retention(rq, rk, rv)).reshape(B, S, RET_V_W).astype(x.dtype)
    y_ret = (o_ret * jax.nn.silu(rg)) @ w_ret_proj

    q = (rms_norm(cq, g_cq) @ w_uq).reshape(B, S, MLA_HEADS, MLA_NOPE + MLA_ROPE)
    q_nope, q_rope = q[..., :MLA_NOPE], q[..., MLA_NOPE:]
    q_rope = apply_rope(q_rope, cos_m[:, :, None], sin_m[:, :, None])
    kv = (rms_norm(ckv, g_ckv) @ w_ukv).reshape(B, S, MLA_HEADS, MLA_NOPE + MLA_DV)
    k_nope, v = kv[..., :MLA_NOPE], kv[..., MLA_NOPE:]
    k_rope = apply_rope(kr, cos_m, sin_m)
    o_mla = mla_attention(q_nope, q_rope, k_nope, k_rope, v)
    y_mla = (o_mla * jax.nn.silu(mg)) @ w_mla_proj

    g_a, g_b = jnp.split(jax.nn.sigmoid(bg), 2, axis=-1)
    merged = g_a * y_ret + g_b * y_mla
    out = merged @ w_out
    return x + gate[:, None, :] * out


def setup_inputs(seed: int = 0) -> dict:
    key = jax.random.key(seed)
    ks = jax.random.split(key, 16)
    f32 = jnp.float32

    def nrm(k, shape, fan_in, mult=1.0):
        return jax.random.normal(k, shape, f32) * (mult * fan_in ** -0.5)

    x = jax.random.normal(ks[0], (BATCH, SEQ, D_MODEL), f32)
    c = jax.random.normal(ks[1], (BATCH, D_MODEL), f32)
    positions = (jnp.arange(SEQ, dtype=jnp.int32)[None, :]
                 + jax.random.randint(ks[2], (BATCH, 1), 0, 1024, dtype=jnp.int32))
    w_mod = nrm(ks[3], (DEPTH, D_MODEL, 3 * D_MODEL), D_MODEL, 0.5)
    b_mod = 0.01 * jax.random.normal(ks[4], (DEPTH, 3 * D_MODEL), f32)
    g_norm = 1.0 + 0.02 * jax.random.normal(ks[5], (DEPTH, D_MODEL), f32)
    w_in = nrm(ks[6], (DEPTH, D_MODEL, D_IN), D_MODEL)
    g_cq = 1.0 + 0.02 * jax.random.normal(ks[7], (DEPTH, MLA_Q_RANK), f32)
    g_ckv = 1.0 + 0.02 * jax.random.normal(ks[8], (DEPTH, MLA_KV_RANK), f32)
    w_uq = nrm(ks[9], (DEPTH, MLA_Q_RANK, MLA_HEADS * (MLA_NOPE + MLA_ROPE)), MLA_Q_RANK)
    w_ukv = nrm(ks[10], (DEPTH, MLA_KV_RANK, MLA_HEADS * (MLA_NOPE + MLA_DV)), MLA_KV_RANK)
    w_ret_proj = nrm(ks[11], (DEPTH, RET_V_W, D_MODEL), RET_V_W)
    w_mla_proj = nrm(ks[12], (DEPTH, MLA_V_W, D_MODEL), MLA_V_W)
    w_out = nrm(ks[13], (DEPTH, D_MODEL, D_MODEL), D_MODEL)
    g_final = 1.0 + 0.02 * jax.random.normal(ks[14], (D_MODEL,), f32)
    return {"x": x, "c": c, "positions": positions, "w_mod": w_mod, "b_mod": b_mod,
            "g_norm": g_norm, "w_in": w_in, "g_cq": g_cq, "g_ckv": g_ckv, "w_uq": w_uq,
            "w_ukv": w_ukv, "w_ret_proj": w_ret_proj, "w_mla_proj": w_mla_proj,
            "w_out": w_out, "g_final": g_final}


def reference(x, c, positions, w_mod, b_mod, g_norm, w_in, g_cq, g_ckv, w_uq, w_ukv,
              w_ret_proj, w_mla_proj, w_out, g_final):
    c_act = jax.nn.silu(c)
    cos_r, sin_r = rope_tables(positions, RET_DK)
    cos_m, sin_m = rope_tables(positions, MLA_ROPE)
    for l in range(DEPTH):
        x = hybrid_layer(x, c_act, cos_r, sin_r, cos_m, sin_m,
                         w_mod[l], b_mod[l], g_norm[l], w_in[l], g_cq[l], g_ckv[l],
                         w_uq[l], w_ukv[l], w_ret_proj[l], w_mla_proj[l], w_out[l])
    return rms_norm(x, g_final)
```

```python
import functools

import jax
import jax.numpy as jnp
from jax import lax
from jax.experimental import pallas as pl
from jax.experimental.pallas import tpu as pltpu

F32 = jnp.float32
BF16 = jnp.bfloat16

CHUNK = 64
EPS = 1e-6
NEG_INF = -1e30
ROPE_BASE = 10000.0
RET_HEADS = 8
RET_DK = 128
RET_DV = 256
RET_QK_W = RET_HEADS * RET_DK
RET_V_W = RET_HEADS * RET_DV
MLA_HEADS = 16
MLA_Q_RANK = 512
MLA_KV_RANK = 512
MLA_NOPE = 128
MLA_ROPE = 64
MLA_DV = 128
MLA_V_W = MLA_HEADS * MLA_DV
MLA_QK_PAD = 256

LANES = 128
VMEM_LIMIT = 56 * 1024 * 1024

C_RQ = 0
C_RK = C_RQ + RET_QK_W
C_RV = C_RK + RET_QK_W
C_RG = C_RV + RET_V_W
C_CQ = C_RG + RET_V_W
C_CKV = C_CQ + MLA_Q_RANK
C_MG = C_CKV + MLA_KV_RANK
C_BG = C_MG + MLA_V_W
IN_TN = 512


def _params(sem):
    return pltpu.CompilerParams(dimension_semantics=sem, vmem_limit_bytes=VMEM_LIMIT)


def _sigmoid(x):
    return 1.0 / (1.0 + jnp.exp(-x))


def _rope_tables_kernel(pos_ref, invr_ref, invm_ref, cr_ref, sr_ref, cm_ref, sm1_ref, sm2_ref):
    pos = pos_ref[...].astype(F32)
    lane = lax.broadcasted_iota(jnp.int32, cr_ref.shape, 1)
    ang_r = pos * invr_ref[...]
    cr_ref[...] = jnp.cos(ang_r)
    sin_r = jnp.sin(ang_r)
    sr_ref[...] = jnp.where(lane < RET_DK // 2, -sin_r, sin_r)
    ang_m = pos * invm_ref[...]
    cos_m = jnp.cos(ang_m)
    sin_m = jnp.sin(ang_m)
    half = MLA_ROPE // 2
    cm_ref[...] = jnp.where(lane < MLA_ROPE, cos_m, 0.0)
    sm1_ref[...] = jnp.where(lane < half, -sin_m, 0.0)
    sm2_ref[...] = jnp.where((lane >= half) & (lane < MLA_ROPE), sin_m, 0.0)


def _rope_tables(positions):
    t = positions.size
    tm = min(t, 1024)
    inv_r = 1.0 / (ROPE_BASE ** (jnp.arange(0, RET_DK, 2, dtype=F32) / RET_DK))
    inv_m = 1.0 / (ROPE_BASE ** (jnp.arange(0, MLA_ROPE, 2, dtype=F32) / MLA_ROPE))
    invr = jnp.concatenate([inv_r, inv_r])[None, :]
    invm = jnp.concatenate([inv_m, inv_m, jnp.zeros((LANES - MLA_ROPE,), F32)])[None, :]
    row = pl.BlockSpec((tm, LANES), lambda i: (i, 0))
    const = pl.BlockSpec((1, LANES), lambda i: (0, 0))
    return pl.pallas_call(
        _rope_tables_kernel,
        grid=(t // tm,),
        in_specs=[pl.BlockSpec((tm, 1), lambda i: (i, 0)), const, const],
        out_specs=[row] * 5,
        out_shape=[jax.ShapeDtypeStruct((t, LANES), F32)] * 5,
        compiler_params=_params(("parallel",)),
        name="rope_tables",
    )(positions.reshape(t, 1), invr, invm)


def _mod_kernel(c_ref, w_ref, b_ref, o_ref):
    c = c_ref[...]
    ca = (c * _sigmoid(c)).astype(BF16)
    o_ref[...] = jnp.dot(ca, w_ref[...].astype(BF16), preferred_element_type=F32) + b_ref[...]


def _modulation(c, w_mod, b_mod):
    depth, d, n = w_mod.shape
    bsz = c.shape[0]
    rows = 8
    cp = jnp.pad(c, ((0, rows - bsz), (0, 0)))
    tn = 512
    return pl.pallas_call(
        _mod_kernel,
        grid=(depth, n // tn),
        in_specs=[
            pl.BlockSpec((rows, d), lambda l, j: (0, 0)),
            pl.BlockSpec((None, d, tn), lambda l, j: (l, 0, j)),
            pl.BlockSpec((None, 1, tn), lambda l, j: (l, 0, j)),
        ],
        out_specs=pl.BlockSpec((None, rows, tn), lambda l, j: (l, 0, j)),
        out_shape=jax.ShapeDtypeStruct((depth, rows, n), F32),
        compiler_params=_params(("parallel", "parallel")),
        name="modulation",
    )(cp, w_mod, b_mod.reshape(depth, 1, n))


def _rope_ret(a, cos, sin):
    outs = []
    for hh in range(a.shape[1] // RET_DK):
        t = a[:, hh * RET_DK:(hh + 1) * RET_DK]
        outs.append(t * cos + pltpu.roll(t, RET_DK // 2, 1) * sin)
    return jnp.concatenate(outs, axis=1) if len(outs) > 1 else outs[0]


def _rope_mla(t, cos, sin_lo, sin_hi):
    half = MLA_ROPE // 2
    return t * cos + pltpu.roll(t, LANES - half, 1) * sin_lo + pltpu.roll(t, half, 1) * sin_hi


def _in_proj_kernel(x_ref, shift_ref, scale_ref, g_ref, w_ref, wkr_ref, cr_ref, sr_ref,
                    cm_ref, sm1_ref, sm2_ref, gcq_ref, gckv_ref, p_ref, kr_ref, h_scr):
    j = pl.program_id(1)

    @pl.when(j == 0)
    def _():
        x = x_ref[...]
        inv = lax.rsqrt(jnp.mean(x * x, axis=-1, keepdims=True) + EPS)
        h = (x * inv) * g_ref[...]
        h = h * (1.0 + scale_ref[...]) + shift_ref[...]
        hb = h.astype(BF16)
        h_scr[...] = hb
        kr = jnp.dot(hb, wkr_ref[...], preferred_element_type=F32)
        kr_ref[...] = _rope_mla(kr, cm_ref[...], sm1_ref[...], sm2_ref[...]).astype(kr_ref.dtype)

    acc = jnp.dot(h_scr[...], w_ref[...], preferred_element_type=F32)

    def tile_range(lo, width):
        return (j >= lo // IN_TN) & (j < (lo + width) // IN_TN)

    @pl.when(tile_range(C_RQ, RET_QK_W))
    def _():
        p_ref[...] = _rope_ret(acc, cr_ref[...], sr_ref[...]).astype(p_ref.dtype)

    @pl.when(tile_range(C_RK, RET_QK_W))
    def _():
        k = _rope_ret(acc, cr_ref[...], sr_ref[...]) * (RET_DK ** -0.5)
        p_ref[...] = k.astype(p_ref.dtype)

    @pl.when(tile_range(C_RV, RET_V_W))
    def _():
        p_ref[...] = acc.astype(p_ref.dtype)

    @pl.when(tile_range(C_RG, RET_V_W) | tile_range(C_MG, MLA_V_W))
    def _():
        p_ref[...] = (acc * _sigmoid(acc)).astype(p_ref.dtype)

    def latent_norm(g_latent_ref):
        inv = lax.rsqrt(jnp.mean(acc * acc, axis=-1, keepdims=True) + EPS)
        p_ref[...] = ((acc * inv) * g_latent_ref[...]).astype(p_ref.dtype)

    @pl.when(tile_range(C_CQ, MLA_Q_RANK))
    def _():
        latent_norm(gcq_ref)

    @pl.when(tile_range(C_CKV, MLA_KV_RANK))
    def _():
        latent_norm(gckv_ref)

    @pl.when(j >= C_BG // IN_TN)
    def _():
        p_ref[...] = _sigmoid(acc).astype(p_ref.dtype)


def _in_proj(x2, mod4, g_norm3, w_main, w_kr, tabs, g_cq3, g_ckv3, layer, seq):
    t, d = x2.shape
    n = w_main.shape[2]
    tm = min(seq, 1024)
    per_b = seq // tm
    cr, sr, cm, sm1, sm2 = tabs
    tab = pl.BlockSpec((tm, LANES), lambda i, j: (i, 0))
    return pl.pallas_call(
        _in_proj_kernel,
        grid=(t // tm, n // IN_TN),
        in_specs=[
            pl.BlockSpec((tm, d), lambda i, j: (i, 0)),
            pl.BlockSpec((None, None, 1, d), lambda i, j: (layer, i // per_b, 0, 0)),
            pl.BlockSpec((None, None, 1, d), lambda i, j: (layer, i // per_b, 0, 1)),
            pl.BlockSpec((None, 1, d), lambda i, j: (layer, 0, 0)),
            pl.BlockSpec((None, d, IN_TN), lambda i, j: (layer, 0, j)),
            pl.BlockSpec((None, d, LANES), lambda i, j: (layer, 0, 0)),
            tab, tab, tab, tab, tab,
            pl.BlockSpec((None, 1, MLA_Q_RANK), lambda i, j: (layer, 0, 0)),
            pl.BlockSpec((None, 1, MLA_KV_RANK), lambda i, j: (layer, 0, 0)),
        ],
        out_specs=[
            pl.BlockSpec((tm, IN_TN), lambda i, j: (i, j)),
            pl.BlockSpec((tm, LANES), lambda i, j: (i, 0)),
        ],
        out_shape=[
            jax.ShapeDtypeStruct((t, n), BF16),
            jax.ShapeDtypeStruct((t, LANES), BF16),
        ],
        scratch_shapes=[pltpu.VMEM((tm, d), BF16)],
        compiler_params=_params(("parallel", "arbitrary")),
        name="in_proj",
    )(x2, mod4, mod4, g_norm3, w_main, w_kr, cr, sr, cm, sm1, sm2, g_cq3, g_ckv3)


UP_HEADS = 4


def _q_up_kernel(c_ref, w_ref, cm_ref, sm1_ref, sm2_ref, q_ref):
    scale = (MLA_NOPE + MLA_ROPE) ** -0.5
    acc = jnp.dot(c_ref[...], w_ref[...], preferred_element_type=F32) * scale
    cm, s1, s2 = cm_ref[...], sm1_ref[...], sm2_ref[...]
    for hh in range(UP_HEADS):
        lo = hh * MLA_QK_PAD
        q_ref[:, lo:lo + MLA_NOPE] = acc[:, lo:lo + MLA_NOPE].astype(q_ref.dtype)
        r = acc[:, lo + MLA_NOPE:lo + MLA_QK_PAD]
        q_ref[:, lo + MLA_NOPE:lo + MLA_QK_PAD] = _rope_mla(r, cm, s1, s2).astype(q_ref.dtype)


def _q_up(p, w_uq_p, tabs, layer):
    t = p.shape[0]
    tm = min(t, 512)
    tn = UP_HEADS * MLA_QK_PAD
    _, _, cm, sm1, sm2 = tabs
    tab = pl.BlockSpec((tm, LANES), lambda i, j: (i, 0))
    return pl.pallas_call(
        _q_up_kernel,
        grid=(t // tm, MLA_HEADS // UP_HEADS),
        in_specs=[
            pl.BlockSpec((tm, MLA_Q_RANK), lambda i, j: (i, C_CQ // MLA_Q_RANK)),
            pl.BlockSpec((None, MLA_Q_RANK, tn), lambda i, j: (layer, 0, j)),
            tab, tab, tab,
        ],
        out_specs=pl.BlockSpec((tm, tn), lambda i, j: (i, j)),
        out_shape=jax.ShapeDtypeStruct((t, MLA_HEADS * MLA_QK_PAD), BF16),
        compiler_params=_params(("parallel", "arbitrary")),
        name="q_up",
    )(p, w_uq_p, cm, sm1, sm2)


def _kv_up_kernel(c_ref, wk_ref, wv_ref, kr_ref, k_ref, v_ref):
    c = c_ref[...]
    kn = jnp.dot(c, wk_ref[...], preferred_element_type=F32)
    v_ref[...] = jnp.dot(c, wv_ref[...], preferred_element_type=F32).astype(v_ref.dtype)
    kr = kr_ref[...]
    for hh in range(UP_HEADS):
        lo = hh * MLA_QK_PAD
        k_ref[:, lo:lo + MLA_NOPE] = kn[:, hh * MLA_NOPE:(hh + 1) * MLA_NOPE].astype(k_ref.dtype)
        k_ref[:, lo + MLA_NOPE:lo + MLA_QK_PAD] = kr


def _kv_up(p, kr, w_k, w_v, layer):
    t = p.shape[0]
    tm = min(t, 512)
    return pl.pallas_call(
        _kv_up_kernel,
        grid=(t // tm, MLA_HEADS // UP_HEADS),
        in_specs=[
            pl.BlockSpec((tm, MLA_KV_RANK), lambda i, j: (i, C_CKV // MLA_KV_RANK)),
            pl.BlockSpec((None, MLA_KV_RANK, UP_HEADS * MLA_NOPE), lambda i, j: (layer, 0, j)),
            pl.BlockSpec((None, MLA_KV_RANK, UP_HEADS * MLA_DV), lambda i, j: (layer, 0, j)),
            pl.BlockSpec((tm, LANES), lambda i, j: (i, 0)),
        ],
        out_specs=[
            pl.BlockSpec((tm, UP_HEADS * MLA_QK_PAD), lambda i, j: (i, j)),
            pl.BlockSpec((tm, UP_HEADS * MLA_DV), lambda i, j: (i, j)),
        ],
        out_shape=[
            jax.ShapeDtypeStruct((t, MLA_HEADS * MLA_QK_PAD), BF16),
            jax.ShapeDtypeStruct((t, MLA_V_W), BF16),
        ],
        compiler_params=_params(("parallel", "arbitrary")),
        name="kv_up",
    )(p, w_k, w_v, kr)


RET_BLOCK = 256


def _retention_kernel(q_ref, k_ref, v_ref, gate_ref, lg_ref, o_ref, state, dmat, xi, zeta):
    blk = q_ref.shape[0]
    lg = lg_ref[...]

    @pl.when(pl.program_id(2) == 0)
    def _():
        state[...] = jnp.zeros_like(state)
        r = lax.broadcasted_iota(jnp.int32, (blk, blk), 0)
        c = lax.broadcasted_iota(jnp.int32, (blk, blk), 1)
        dist = jnp.abs(r - c).astype(F32)
        decay = jnp.exp(dist * lg)
        dmat[...] = jnp.where((c // CHUNK) <= (r // CHUNK), decay, 0.0)
        rx = lax.broadcasted_iota(jnp.int32, xi.shape, 0).astype(F32)
        xi[...] = jnp.exp((rx + 1.0) * lg[:, :xi.shape[1]])
        rz = lax.broadcasted_iota(jnp.int32, zeta.shape, 0).astype(F32)
        zeta[...] = jnp.exp((blk - 1.0 - rz) * lg[:, :zeta.shape[1]])

    q = q_ref[...]
    k = k_ref[...]
    v = v_ref[...]
    s = lax.dot_general(q, k, (((1,), (1,)), ((), ())), preferred_element_type=F32)
    o = jnp.dot((s * dmat[...]).astype(BF16), v, preferred_element_type=F32)
    st = state[...]
    o = o + xi[...] * jnp.dot(q, st.astype(BF16), preferred_element_type=F32)
    kz_t = (k.astype(F32) * zeta[...]).T.astype(BF16)
    block_decay = jnp.exp(blk * lg[:, :st.shape[1]])
    state[...] = st * block_decay + jnp.dot(kz_t, v, preferred_element_type=F32)

    mu = jnp.mean(o, axis=-1, keepdims=True)
    dlt = o - mu
    var = jnp.mean(dlt * dlt, axis=-1, keepdims=True)
    y = dlt * lax.rsqrt(var + EPS)
    o_ref[...] = (y * gate_ref[...].astype(F32)).astype(o_ref.dtype)


def _retention(p, bsz, seq):
    t = p.shape[0]
    blk = min(seq, RET_BLOCK)
    nblk = seq // blk
    log_gamma = jnp.log(1.0 - 2.0 ** (-5.0 - jnp.arange(RET_HEADS, dtype=F32)))
    lg = jnp.broadcast_to(log_gamma[:, None, None], (RET_HEADS, 1, blk))
    qk_blocks = RET_QK_W // RET_DK
    return pl.pallas_call(
        _retention_kernel,
        grid=(bsz, RET_HEADS, nblk),
        in_specs=[
            pl.BlockSpec((blk, RET_DK), lambda b, h, j: (b * nblk + j, C_RQ // RET_DK + h)),
            pl.BlockSpec((blk, RET_DK), lambda b, h, j: (b * nblk + j, C_RK // RET_DK + h)),
            pl.BlockSpec((blk, RET_DV), lambda b, h, j: (b * nblk + j, C_RV // RET_DV + h)),
            pl.BlockSpec((blk, RET_DV), lambda b, h, j: (b * nblk + j, C_RG // RET_DV + h)),
            pl.BlockSpec((None, 1, blk), lambda b, h, j: (h, 0, 0)),
        ],
        out_specs=pl.BlockSpec((blk, RET_DV), lambda b, h, j: (b * nblk + j, h)),
        out_shape=jax.ShapeDtypeStruct((t, RET_V_W), BF16),
        scratch_shapes=[
            pltpu.VMEM((RET_DK, RET_DV), F32),
            pltpu.VMEM((blk, blk), F32),
            pltpu.VMEM((blk, RET_DV), F32),
            pltpu.VMEM((blk, RET_DK), F32),
        ],
        compiler_params=_params(("parallel", "parallel", "arbitrary")),
        name="retention",
    )(p, p, p, p, lg)


ATT_BLOCK = 512


def _attention_kernel(q_ref, k_ref, v_ref, gate_ref, o_ref, m_scr, l_scr, acc_scr):
    tq = q_ref.shape[0]
    qi = pl.program_id(2)
    q = q_ref[...]
    m_scr[...] = jnp.full_like(m_scr, -jnp.inf)
    l_scr[...] = jnp.zeros_like(l_scr)
    acc_scr[...] = jnp.zeros_like(acc_scr)

    def step(kb, masked):
        start = pl.multiple_of(kb * tq, tq)
        k = k_ref[pl.ds(start, tq), :]
        v = v_ref[pl.ds(start, tq), :]
        s = lax.dot_general(q, k, (((1,), (1,)), ((), ())), preferred_element_type=F32)
        if masked:
            r = lax.broadcasted_iota(jnp.int32, s.shape, 0)
            c = lax.broadcasted_iota(jnp.int32, s.shape, 1)
            s = jnp.where((c // CHUNK) <= (r // CHUNK), s, NEG_INF)
        m_old = m_scr[...]
        m_new = jnp.maximum(m_old, jnp.max(s, axis=-1, keepdims=True))
        alpha = jnp.exp(m_old - m_new)
        p = jnp.exp(s - m_new)
        l_scr[...] = alpha * l_scr[...] + jnp.sum(p, axis=-1, keepdims=True)
        acc_scr[...] = alpha * acc_scr[...] + jnp.dot(p.astype(BF16), v, preferred_element_type=F32)
        m_scr[...] = m_new

    def body(kb, carry):
        step(kb, False)
        return carry

    lax.fori_loop(0, qi, body, 0)
    step(qi, True)
    o = acc_scr[...] / l_scr[...]
    o_ref[...] = (o * gate_ref[...].astype(F32)).astype(o_ref.dtype)


def _attention(qc, kc, v, p, bsz, seq):
    t = qc.shape[0]
    tq = min(seq, ATT_BLOCK)
    nq = seq // tq
    return pl.pallas_call(
        _attention_kernel,
        grid=(bsz, MLA_HEADS, nq),
        in_specs=[
            pl.BlockSpec((tq, MLA_QK_PAD), lambda b, h, i: (b * nq + i, h)),
            pl.BlockSpec((seq, MLA_QK_PAD), lambda b, h, i: (b, h)),
            pl.BlockSpec((seq, MLA_DV), lambda b, h, i: (b, h)),
            pl.BlockSpec((tq, MLA_DV), lambda b, h, i: (b * nq + i, C_MG // MLA_DV + h)),
        ],
        out_specs=pl.BlockSpec((tq, MLA_DV), lambda b, h, i: (b * nq + i, h)),
        out_shape=jax.ShapeDtypeStruct((t, MLA_V_W), BF16),
        scratch_shapes=[
            pltpu.VMEM((tq, 1), F32),
            pltpu.VMEM((tq, 1), F32),
            pltpu.VMEM((tq, MLA_DV), F32),
        ],
        compiler_params=_params(("parallel", "parallel", "arbitrary")),
        name="attention",
    )(qc, kc, v, p)


def _merge_kernel(a_ref, b_ref, wr_ref, wm_ref, ga_ref, gb_ref, o_ref):
    y_ret = jnp.dot(a_ref[...], wr_ref[...], preferred_element_type=F32)
    y_mla = jnp.dot(b_ref[...], wm_ref[...], preferred_element_type=F32)
    merged = ga_ref[...].astype(F32) * y_ret + gb_ref[...].astype(F32) * y_mla
    o_ref[...] = merged.astype(o_ref.dtype)


def _merge(a, bm, p, w_ret, w_mla, layer):
    t, d_in = a.shape
    d = w_ret.shape[2]
    tm = min(t, 1024)
    tn = 512
    return pl.pallas_call(
        _merge_kernel,
        grid=(t // tm, d // tn),
        in_specs=[
            pl.BlockSpec((tm, d_in), lambda i, j: (i, 0)),
            pl.BlockSpec((tm, d_in), lambda i, j: (i, 0)),
            pl.BlockSpec((None, d_in, tn), lambda i, j: (layer, 0, j)),
            pl.BlockSpec((None, d_in, tn), lambda i, j: (layer, 0, j)),
            pl.BlockSpec((tm, tn), lambda i, j: (i, C_BG // tn + j)),
            pl.BlockSpec((tm, tn), lambda i, j: (i, (C_BG + d) // tn + j)),
        ],
        out_specs=pl.BlockSpec((tm, tn), lambda i, j: (i, j)),
        out_shape=jax.ShapeDtypeStruct((t, d), BF16),
        compiler_params=_params(("parallel", "arbitrary")),
        name="merge_proj",
    )(a, bm, w_ret, w_mla, p, p)


def _out_kernel(m_ref, w_ref, x_ref, gate_ref, o_ref):
    out = jnp.dot(m_ref[...], w_ref[...], preferred_element_type=F32)
    o_ref[...] = x_ref[...] + gate_ref[...] * out


def _out_proj(merged, w_out, x2, mod4, layer, seq):
    t, d = x2.shape
    tm = min(seq, 1024)
    per_b = seq // tm
    tn = 512
    gate_blk = 2 * d // tn
    return pl.pallas_call(
        _out_kernel,
        grid=(t // tm, d // tn),
        in_specs=[
            pl.BlockSpec((tm, d), lambda i, j: (i, 0)),
            pl.BlockSpec((None, d, tn), lambda i, j: (layer, 0, j)),
            pl.BlockSpec((tm, tn), lambda i, j: (i, j)),
            pl.BlockSpec((None, None, 1, tn), lambda i, j: (layer, i // per_b, 0, gate_blk + j)),
        ],
        out_specs=pl.BlockSpec((tm, tn), lambda i, j: (i, j)),
        out_shape=jax.ShapeDtypeStruct((t, d), F32),
        compiler_params=_params(("parallel", "arbitrary")),
        name="out_proj",
    )(merged, w_out, x2, mod4)


def _final_norm_kernel(x_ref, g_ref, o_ref):
    x = x_ref[...]
    inv = lax.rsqrt(jnp.mean(x * x, axis=-1, keepdims=True) + EPS)
    o_ref[...] = (x * inv) * g_ref[...]


def _final_norm(x2, g_final):
    t, d = x2.shape
    tm = min(t, 512)
    return pl.pallas_call(
        _final_norm_kernel,
        grid=(t // tm,),
        in_specs=[pl.BlockSpec((tm, d), lambda i: (i, 0)), pl.BlockSpec((1, d), lambda i: (0, 0))],
        out_specs=pl.BlockSpec((tm, d), lambda i: (i, 0)),
        out_shape=jax.ShapeDtypeStruct((t, d), F32),
        compiler_params=_params(("parallel",)),
        name="final_norm",
    )(x2, g_final.reshape(1, d))


def kernel(x, c, positions, w_mod, b_mod, g_norm, w_in, g_cq, g_ckv, w_uq, w_ukv,
           w_ret_proj, w_mla_proj, w_out, g_final):
    bsz, seq, d = x.shape
    depth = w_in.shape[0]
    t = bsz * seq
    assert seq % RET_BLOCK == 0 and seq % min(seq, ATT_BLOCK) == 0 and d % IN_TN == 0

    kr_lo = C_CKV + MLA_KV_RANK
    w_main = jnp.concatenate([w_in[:, :, :kr_lo], w_in[:, :, kr_lo + MLA_ROPE:]], axis=2).astype(BF16)
    w_kr = jnp.pad(w_in[:, :, kr_lo:kr_lo + MLA_ROPE], ((0, 0), (0, 0), (0, LANES - MLA_ROPE))).astype(BF16)
    w_uq_p = jnp.pad(
        w_uq.reshape(depth, MLA_Q_RANK, MLA_HEADS, MLA_NOPE + MLA_ROPE),
        ((0, 0), (0, 0), (0, 0), (0, MLA_QK_PAD - MLA_NOPE - MLA_ROPE)),
    ).reshape(depth, MLA_Q_RANK, MLA_HEADS * MLA_QK_PAD).astype(BF16)
    w_ukv4 = w_ukv.reshape(depth, MLA_KV_RANK, MLA_HEADS, MLA_NOPE + MLA_DV)
    w_k = w_ukv4[..., :MLA_NOPE].reshape(depth, MLA_KV_RANK, MLA_HEADS * MLA_NOPE).astype(BF16)
    w_v = w_ukv4[..., MLA_NOPE:].reshape(depth, MLA_KV_RANK, MLA_V_W).astype(BF16)
    w_ret = w_ret_proj.astype(BF16)
    w_mla = w_mla_proj.astype(BF16)
    w_o = w_out.astype(BF16)

    tabs = _rope_tables(positions)
    mod = _modulation(c, w_mod, b_mod)
    mod4 = mod.reshape(depth, mod.shape[1], 1, 3 * d)
    g_norm3 = g_norm.reshape(depth, 1, d)
    g_cq3 = g_cq.reshape(depth, 1, MLA_Q_RANK)
    g_ckv3 = g_ckv.reshape(depth, 1, MLA_KV_RANK)

    x2 = x.reshape(t, d)
    for layer in range(depth):
        p, kr = _in_proj(x2, mod4, g_norm3, w_main, w_kr, tabs, g_cq3, g_ckv3, layer, seq)
        qc = _q_up(p, w_uq_p, tabs, layer)
        kc, v = _kv_up(p, kr, w_k, w_v, layer)
        a = _retention(p, bsz, seq)
        bm = _attention(qc, kc, v, p, bsz, seq)
        merged = _merge(a, bm, p, w_ret, w_mla, layer)
        x2 = _out_proj(merged, w_o, x2, mod4, layer, seq)
    return _final_norm(x2, g_final).reshape(bsz, seq, d)
```

```python
import jax
import jax.numpy as jnp
from jax import lax
from jax.experimental import pallas as pl
from jax.experimental.pallas import tpu as pltpu

F32 = jnp.float32
BF16 = jnp.bfloat16

CHUNK = 64
EPS = 1e-6
NEG_INF = -1e30
ROPE_BASE = 10000.0
RET_HEADS = 8
RET_DK = 128
RET_DV = 256
RET_QK_W = RET_HEADS * RET_DK
RET_V_W = RET_HEADS * RET_DV
MLA_HEADS = 16
MLA_Q_RANK = 512
MLA_KV_RANK = 512
MLA_NOPE = 128
MLA_ROPE = 64
MLA_DV = 128
MLA_V_W = MLA_HEADS * MLA_DV
MLA_QK_PAD = 256
MLA_V_PAD = 2 * MLA_DV
LOG2E = 1.4426950408889634

LANES = 128
VMEM_LIMIT = 56 * 1024 * 1024

C_RQ = 0
C_RK = C_RQ + RET_QK_W
C_RV = C_RK + RET_QK_W
C_RG = C_RV + RET_V_W
C_CQ = C_RG + RET_V_W
C_CKV = C_CQ + MLA_Q_RANK
C_MG = C_CKV + MLA_KV_RANK
C_BG = C_MG + MLA_V_W
IN_TN = 1024
IN_CHUNK = 256


def _params(sem):
    return pltpu.CompilerParams(dimension_semantics=sem, vmem_limit_bytes=VMEM_LIMIT)


def _sigmoid(x):
    return 1.0 / (1.0 + jnp.exp(-x))


def _rope_tables_kernel(pos_ref, invr_ref, invm_ref, cr_ref, sr_ref, cm_ref, sm1_ref, sm2_ref):
    pos = pos_ref[...].astype(F32)
    lane = lax.broadcasted_iota(jnp.int32, cr_ref.shape, 1)
    ang_r = pos * invr_ref[...]
    cr_ref[...] = jnp.cos(ang_r)
    sin_r = jnp.sin(ang_r)
    sr_ref[...] = jnp.where(lane < RET_DK // 2, -sin_r, sin_r)
    ang_m = pos * invm_ref[...]
    cos_m = jnp.cos(ang_m)
    sin_m = jnp.sin(ang_m)
    half = MLA_ROPE // 2
    cm_ref[...] = jnp.where(lane < MLA_ROPE, cos_m, 0.0)
    sm1_ref[...] = jnp.where(lane < half, -sin_m, 0.0)
    sm2_ref[...] = jnp.where((lane >= half) & (lane < MLA_ROPE), sin_m, 0.0)


def _rope_tables(positions):
    t = positions.size
    tm = min(t, 1024)
    inv_r = 1.0 / (ROPE_BASE ** (jnp.arange(0, RET_DK, 2, dtype=F32) / RET_DK))
    inv_m = 1.0 / (ROPE_BASE ** (jnp.arange(0, MLA_ROPE, 2, dtype=F32) / MLA_ROPE))
    invr = jnp.concatenate([inv_r, inv_r])[None, :]
    invm = jnp.concatenate([inv_m, inv_m, jnp.zeros((LANES - MLA_ROPE,), F32)])[None, :]
    row = pl.BlockSpec((tm, LANES), lambda i: (i, 0))
    const = pl.BlockSpec((1, LANES), lambda i: (0, 0))
    return pl.pallas_call(
        _rope_tables_kernel,
        grid=(t // tm,),
        in_specs=[pl.BlockSpec((tm, 1), lambda i: (i, 0)), const, const],
        out_specs=[row] * 5,
        out_shape=[jax.ShapeDtypeStruct((t, LANES), F32)] * 5,
        compiler_params=_params(("parallel",)),
        name="rope_tables",
    )(positions.reshape(t, 1), invr, invm)


def _mod_kernel(c_ref, w_ref, b_ref, o_ref):
    c = c_ref[...]
    ca = (c * _sigmoid(c)).astype(BF16)
    o_ref[...] = jnp.dot(ca, w_ref[...].astype(BF16), preferred_element_type=F32) + b_ref[...]


def _modulation(c, w_mod, b_mod):
    depth, d, n = w_mod.shape
    bsz = c.shape[0]
    rows = 8
    cp = jnp.pad(c, ((0, rows - bsz), (0, 0)))
    tn = 512
    return pl.pallas_call(
        _mod_kernel,
        grid=(depth, n // tn),
        in_specs=[
            pl.BlockSpec((rows, d), lambda l, j: (0, 0)),
            pl.BlockSpec((None, d, tn), lambda l, j: (l, 0, j)),
            pl.BlockSpec((None, 1, tn), lambda l, j: (l, 0, j)),
        ],
        out_specs=pl.BlockSpec((None, rows, tn), lambda l, j: (l, 0, j)),
        out_shape=jax.ShapeDtypeStruct((depth, rows, n), F32),
        compiler_params=_params(("parallel", "parallel")),
        name="modulation",
    )(cp, w_mod, b_mod.reshape(depth, 1, n))


def _rope_ret(a, cos, sin):
    outs = []
    for hh in range(a.shape[1] // RET_DK):
        t = a[:, hh * RET_DK:(hh + 1) * RET_DK]
        outs.append(t * cos + pltpu.roll(t, RET_DK // 2, 1) * sin)
    return jnp.concatenate(outs, axis=1) if len(outs) > 1 else outs[0]


def _rope_mla(t, cos, sin_lo, sin_hi):
    half = MLA_ROPE // 2
    return t * cos + pltpu.roll(t, LANES - half, 1) * sin_lo + pltpu.roll(t, half, 1) * sin_hi


def _in_proj_kernel(x_ref, shift_ref, scale_ref, g_ref, w_ref, wkr_ref, cr_ref, sr_ref,
                    cm_ref, sm1_ref, sm2_ref, gcq_ref, gckv_ref, p_ref, kr_ref, h_scr):
    j = pl.program_id(1)

    @pl.when(j == 0)
    def _():
        x = x_ref[...]
        inv = lax.rsqrt(jnp.mean(x * x, axis=-1, keepdims=True) + EPS)
        h = (x * inv) * g_ref[...]
        h = h * (1.0 + scale_ref[...]) + shift_ref[...]
        hb = h.astype(BF16)
        h_scr[...] = hb
        kr = jnp.dot(hb, wkr_ref[...], preferred_element_type=F32)
        kr_ref[...] = _rope_mla(kr, cm_ref[...], sm1_ref[...], sm2_ref[...]).astype(kr_ref.dtype)

    def tile_range(lo, width):
        return (j >= lo // IN_TN) & (j < (lo + width) // IN_TN)

    def project(epilogues, chunk=IN_CHUNK):
        for idx, c0 in enumerate(range(0, IN_TN, chunk)):
            acc = jnp.dot(h_scr[...], w_ref[:, c0:c0 + chunk], preferred_element_type=F32)
            p_ref[:, c0:c0 + chunk] = epilogues[idx % len(epilogues)](acc).astype(p_ref.dtype)

    def rope(acc):
        return _rope_ret(acc, cr_ref[...], sr_ref[...])

    def latent_norm(g_latent_ref):
        def apply(acc):
            inv = lax.rsqrt(jnp.mean(acc * acc, axis=-1, keepdims=True) + EPS)
            return (acc * inv) * g_latent_ref[...]
        return apply

    @pl.when(tile_range(C_RQ, RET_QK_W))
    def _():
        project([rope])

    @pl.when(tile_range(C_RK, RET_QK_W))
    def _():
        project([lambda acc: rope(acc) * (RET_DK ** -0.5)])

    @pl.when(tile_range(C_RV, RET_V_W))
    def _():
        project([lambda acc: acc])

    @pl.when(tile_range(C_RG, RET_V_W) | tile_range(C_MG, MLA_V_W))
    def _():
        project([lambda acc: acc * _sigmoid(acc)])

    @pl.when(tile_range(C_CQ, MLA_Q_RANK + MLA_KV_RANK))
    def _():
        project([latent_norm(gcq_ref), latent_norm(gckv_ref)], chunk=MLA_Q_RANK)

    @pl.when(j >= C_BG // IN_TN)
    def _():
        project([_sigmoid])


def _in_proj(x2, mod4, g_norm3, w_main, w_kr, tabs, g_cq3, g_ckv3, layer, seq):
    t, d = x2.shape
    n = w_main.shape[2]
    tm = min(seq, 1024)
    per_b = seq // tm
    cr, sr, cm, sm1, sm2 = tabs
    tab = pl.BlockSpec((tm, LANES), lambda i, j: (i, 0))
    return pl.pallas_call(
        _in_proj_kernel,
        grid=(t // tm, n // IN_TN),
        in_specs=[
            pl.BlockSpec((tm, d), lambda i, j: (i, 0)),
            pl.BlockSpec((None, None, 1, d), lambda i, j: (layer, i // per_b, 0, 0)),
            pl.BlockSpec((None, None, 1, d), lambda i, j: (layer, i // per_b, 0, 1)),
            pl.BlockSpec((None, 1, d), lambda i, j: (layer, 0, 0)),
            pl.BlockSpec((None, d, IN_TN), lambda i, j: (layer, 0, j)),
            pl.BlockSpec((None, d, LANES), lambda i, j: (layer, 0, 0)),
            tab, tab, tab, tab, tab,
            pl.BlockSpec((None, 1, MLA_Q_RANK), lambda i, j: (layer, 0, 0)),
            pl.BlockSpec((None, 1, MLA_KV_RANK), lambda i, j: (layer, 0, 0)),
        ],
        out_specs=[
            pl.BlockSpec((tm, IN_TN), lambda i, j: (i, j)),
            pl.BlockSpec((tm, LANES), lambda i, j: (i, 0)),
        ],
        out_shape=[
            jax.ShapeDtypeStruct((t, n), BF16),
            jax.ShapeDtypeStruct((t, LANES), BF16),
        ],
        scratch_shapes=[pltpu.VMEM((tm, d), BF16)],
        compiler_params=_params(("parallel", "arbitrary")),
        name="in_proj",
    )(x2, mod4, mod4, g_norm3, w_main, w_kr, cr, sr, cm, sm1, sm2, g_cq3, g_ckv3)


UP_HEADS = 4


def _q_up_kernel(c_ref, w_ref, cm_ref, sm1_ref, sm2_ref, q_ref):
    scale = (MLA_NOPE + MLA_ROPE) ** -0.5 * LOG2E
    acc = jnp.dot(c_ref[...], w_ref[...], preferred_element_type=F32) * scale
    cm, s1, s2 = cm_ref[...], sm1_ref[...], sm2_ref[...]
    for hh in range(UP_HEADS):
        lo = hh * MLA_QK_PAD
        q_ref[:, lo:lo + MLA_NOPE] = acc[:, lo:lo + MLA_NOPE].astype(q_ref.dtype)
        r = acc[:, lo + MLA_NOPE:lo + MLA_QK_PAD]
        q_ref[:, lo + MLA_NOPE:lo + MLA_QK_PAD] = _rope_mla(r, cm, s1, s2).astype(q_ref.dtype)


def _q_up(p, w_uq_p, tabs, layer):
    t = p.shape[0]
    tm = min(t, 512)
    tn = UP_HEADS * MLA_QK_PAD
    _, _, cm, sm1, sm2 = tabs
    tab = pl.BlockSpec((tm, LANES), lambda i, j: (i, 0))
    return pl.pallas_call(
        _q_up_kernel,
        grid=(t // tm, MLA_HEADS // UP_HEADS),
        in_specs=[
            pl.BlockSpec((tm, MLA_Q_RANK), lambda i, j: (i, C_CQ // MLA_Q_RANK)),
            pl.BlockSpec((None, MLA_Q_RANK, tn), lambda i, j: (layer, 0, j)),
            tab, tab, tab,
        ],
        out_specs=pl.BlockSpec((tm, tn), lambda i, j: (i, j)),
        out_shape=jax.ShapeDtypeStruct((t, MLA_HEADS * MLA_QK_PAD), BF16),
        compiler_params=_params(("parallel", "arbitrary")),
        name="q_up",
    )(p, w_uq_p, cm, sm1, sm2)


def _kv_up_kernel(c_ref, wk_ref, wv_ref, kr_ref, k_ref, v_ref):
    c = c_ref[...]
    kn = jnp.dot(c, wk_ref[...], preferred_element_type=F32)
    vv = jnp.dot(c, wv_ref[...], preferred_element_type=F32)
    kr = kr_ref[...]
    ones = jnp.ones((c.shape[0], MLA_V_PAD - MLA_DV), v_ref.dtype)
    for hh in range(UP_HEADS):
        lo = hh * MLA_QK_PAD
        k_ref[:, lo:lo + MLA_NOPE] = kn[:, hh * MLA_NOPE:(hh + 1) * MLA_NOPE].astype(k_ref.dtype)
        k_ref[:, lo + MLA_NOPE:lo + MLA_QK_PAD] = kr
        vlo = hh * MLA_V_PAD
        v_ref[:, vlo:vlo + MLA_DV] = vv[:, hh * MLA_DV:(hh + 1) * MLA_DV].astype(v_ref.dtype)
        v_ref[:, vlo + MLA_DV:vlo + MLA_V_PAD] = ones


def _kv_up(p, kr, w_k, w_v, layer):
    t = p.shape[0]
    tm = min(t, 512)
    return pl.pallas_call(
        _kv_up_kernel,
        grid=(t // tm, MLA_HEADS // UP_HEADS),
        in_specs=[
            pl.BlockSpec((tm, MLA_KV_RANK), lambda i, j: (i, C_CKV // MLA_KV_RANK)),
            pl.BlockSpec((None, MLA_KV_RANK, UP_HEADS * MLA_NOPE), lambda i, j: (layer, 0, j)),
            pl.BlockSpec((None, MLA_KV_RANK, UP_HEADS * MLA_DV), lambda i, j: (layer, 0, j)),
            pl.BlockSpec((tm, LANES), lambda i, j: (i, 0)),
        ],
        out_specs=[
            pl.BlockSpec((tm, UP_HEADS * MLA_QK_PAD), lambda i, j: (i, j)),
            pl.BlockSpec((tm, UP_HEADS * MLA_V_PAD), lambda i, j: (i, j)),
        ],
        out_shape=[
            jax.ShapeDtypeStruct((t, MLA_HEADS * MLA_QK_PAD), BF16),
            jax.ShapeDtypeStruct((t, MLA_HEADS * MLA_V_PAD), BF16),
        ],
        compiler_params=_params(("parallel", "arbitrary")),
        name="kv_up",
    )(p, w_k, w_v, kr)


RET_BLOCK = 256


def _retention_kernel(q_ref, k_ref, v_ref, gate_ref, lg_ref, o_ref, state, dmat, xi, zeta):
    blk = q_ref.shape[0]
    lg = lg_ref[...]

    @pl.when(pl.program_id(2) == 0)
    def _():
        state[...] = jnp.zeros_like(state)
        r = lax.broadcasted_iota(jnp.int32, (blk, blk), 0)
        c = lax.broadcasted_iota(jnp.int32, (blk, blk), 1)
        dist = jnp.abs(r - c).astype(F32)
        decay = jnp.exp(dist * lg)
        dmat[...] = jnp.where((c // CHUNK) <= (r // CHUNK), decay, 0.0)
        rx = lax.broadcasted_iota(jnp.int32, xi.shape, 0).astype(F32)
        xi[...] = jnp.exp((rx + 1.0) * lg[:, :xi.shape[1]])
        rz = lax.broadcasted_iota(jnp.int32, zeta.shape, 0).astype(F32)
        zeta[...] = jnp.exp((blk - 1.0 - rz) * lg[:, :zeta.shape[1]])

    q = q_ref[...]
    k = k_ref[...]
    v = v_ref[...]
    s = lax.dot_general(q, k, (((1,), (1,)), ((), ())), preferred_element_type=F32)
    o = jnp.dot((s * dmat[...]).astype(BF16), v, preferred_element_type=F32)
    st = state[...]
    o = o + xi[...] * jnp.dot(q, st.astype(BF16), preferred_element_type=F32)
    kz_t = (k.astype(F32) * zeta[...]).T.astype(BF16)
    block_decay = jnp.exp(blk * lg[:, :st.shape[1]])
    state[...] = st * block_decay + jnp.dot(kz_t, v, preferred_element_type=F32)

    mu = jnp.mean(o, axis=-1, keepdims=True)
    dlt = o - mu
    var = jnp.mean(dlt * dlt, axis=-1, keepdims=True)
    y = dlt * lax.rsqrt(var + EPS)
    o_ref[...] = (y * gate_ref[...].astype(F32)).astype(o_ref.dtype)


def _retention(p, bsz, seq):
    t = p.shape[0]
    blk = min(seq, RET_BLOCK)
    nblk = seq // blk
    log_gamma = jnp.log(1.0 - 2.0 ** (-5.0 - jnp.arange(RET_HEADS, dtype=F32)))
    lg = jnp.broadcast_to(log_gamma[:, None, None], (RET_HEADS, 1, blk))
    return pl.pallas_call(
        _retention_kernel,
        grid=(bsz, RET_HEADS, nblk),
        in_specs=[
            pl.BlockSpec((blk, RET_DK), lambda b, h, j: (b * nblk + j, C_RQ // RET_DK + h)),
            pl.BlockSpec((blk, RET_DK), lambda b, h, j: (b * nblk + j, C_RK // RET_DK + h)),
            pl.BlockSpec((blk, RET_DV), lambda b, h, j: (b * nblk + j, C_RV // RET_DV + h)),
            pl.BlockSpec((blk, RET_DV), lambda b, h, j: (b * nblk + j, C_RG // RET_DV + h)),
            pl.BlockSpec((None, 1, blk), lambda b, h, j: (h, 0, 0)),
        ],
        out_specs=pl.BlockSpec((blk, RET_DV), lambda b, h, j: (b * nblk + j, h)),
        out_shape=jax.ShapeDtypeStruct((t, RET_V_W), BF16),
        scratch_shapes=[
            pltpu.VMEM((RET_DK, RET_DV), F32),
            pltpu.VMEM((blk, blk), F32),
            pltpu.VMEM((blk, RET_DV), F32),
            pltpu.VMEM((blk, RET_DK), F32),
        ],
        compiler_params=_params(("parallel", "parallel", "arbitrary")),
        name="retention",
    )(p, p, p, p, lg)


ATT_BLOCK = 512
ATT_HEADS = 2
ATT_SM_ROWS = 64


def _attention_kernel(q_ref, k_ref, v_ref, gate_ref, bias_ref, o_ref, s_scr, m_scr, acc_scr):
    seq = q_ref.shape[0]
    tq = s_scr.shape[1]
    nq = seq // tq
    n_pairs = nq * (nq + 1) // 2

    m_scr[...] = jnp.full_like(m_scr, -jnp.inf)
    acc_scr[...] = jnp.zeros_like(acc_scr)

    def advance(qi, kb):
        wrap = kb >= qi
        return jnp.where(wrap, qi + 1, qi), jnp.where(wrap, 0, kb + 1)

    def rows(blk):
        return pl.ds(pl.multiple_of(blk * tq, tq), tq)

    def slot(next_pair, pair, masked=False):
        for hh in range(ATT_HEADS):
            qk_cols = slice(hh * MLA_QK_PAD, (hh + 1) * MLA_QK_PAD)
            v_cols = slice(hh * MLA_V_PAD, (hh + 1) * MLA_V_PAD)
            m_cols = slice(hh * LANES, (hh + 1) * LANES)
            if pair is not None:
                restart = jnp.where(pair[1] == 0, -jnp.inf, 0.0)
                s = s_scr[hh]
                if masked:
                    s = s + bias_ref[...]
                m_old = m_scr[:, m_cols] + restart
                m_new = jnp.maximum(m_old, jnp.max(s, axis=-1, keepdims=True))
                alpha = jnp.exp2(m_old - m_new)
                p = jnp.exp2(s - jnp.concatenate([m_new] * (tq // LANES), axis=1))
                m_scr[:, m_cols] = m_new
                v = v_ref[rows(pair[1]), v_cols]
                pv = jnp.dot(p.astype(BF16), v, preferred_element_type=F32)
                rescale = jnp.concatenate([alpha] * (MLA_V_PAD // LANES), axis=1)
                acc_scr[:, v_cols] = acc_scr[:, v_cols] * rescale + pv
            if next_pair is not None:
                q = q_ref[rows(next_pair[0]), qk_cols]
                k = k_ref[rows(next_pair[1]), qk_cols]
                s_scr[hh] = lax.dot_general(q, k, (((1,), (1,)), ((), ())), preferred_element_type=F32)

        if pair is not None:
            qi, kb = pair

            @pl.when(kb == qi)
            def _():
                for hh in range(ATT_HEADS):
                    tot = acc_scr[:, hh * MLA_V_PAD:(hh + 1) * MLA_V_PAD]
                    o = tot[:, :MLA_DV] / tot[:, MLA_DV:]
                    cols = slice(hh * MLA_DV, (hh + 1) * MLA_DV)
                    g = gate_ref[rows(qi), cols].astype(F32)
                    o_ref[rows(qi), cols] = (o * g).astype(o_ref.dtype)

    zero = jnp.int32(0)
    pair0 = (zero, zero)
    slot(pair0, None)

    def body(_, carry):
        pair, next_pair = carry
        on_diag = pair[0] == pair[1]

        @pl.when(on_diag)
        def _():
            slot(next_pair, pair, masked=True)

        @pl.when(jnp.logical_not(on_diag))
        def _():
            slot(next_pair, pair, masked=False)

        return next_pair, advance(*next_pair)

    last_pair, _ = lax.fori_loop(0, n_pairs - 1, body, (pair0, advance(*pair0)))
    slot(None, last_pair, masked=True)


def _attention(qc, kc, v, p, bsz, seq):
    t = qc.shape[0]
    tq = min(seq, ATT_BLOCK)
    nh = ATT_HEADS
    r = lax.broadcasted_iota(jnp.int32, (tq, tq), 0) // CHUNK
    c = lax.broadcasted_iota(jnp.int32, (tq, tq), 1) // CHUNK
    bias = jnp.where(c <= r, 0.0, NEG_INF * LOG2E).astype(F32)
    return pl.pallas_call(
        _attention_kernel,
        grid=(bsz, MLA_HEADS // nh),
        in_specs=[
            pl.BlockSpec((seq, nh * MLA_QK_PAD), lambda b, h: (b, h)),
            pl.BlockSpec((seq, nh * MLA_QK_PAD), lambda b, h: (b, h)),
            pl.BlockSpec((seq, nh * MLA_V_PAD), lambda b, h: (b, h)),
            pl.BlockSpec((seq, nh * MLA_DV), lambda b, h: (b, C_MG // (nh * MLA_DV) + h)),
            pl.BlockSpec((tq, tq), lambda b, h: (0, 0)),
        ],
        out_specs=pl.BlockSpec((seq, nh * MLA_DV), lambda b, h: (b, h)),
        out_shape=jax.ShapeDtypeStruct((t, MLA_V_W), BF16),
        scratch_shapes=[
            pltpu.VMEM((nh, tq, tq), F32),
            pltpu.VMEM((tq, nh * LANES), F32),
            pltpu.VMEM((tq, nh * MLA_V_PAD), F32),
        ],
        compiler_params=_params(("parallel", "parallel")),
        name="attention",
    )(qc, kc, v, p, bias)


def _merge_kernel(a_ref, b_ref, wr_ref, wm_ref, ga_ref, gb_ref, o_ref):
    y_ret = jnp.dot(a_ref[...], wr_ref[...], preferred_element_type=F32)
    y_mla = jnp.dot(b_ref[...], wm_ref[...], preferred_element_type=F32)
    merged = ga_ref[...].astype(F32) * y_ret + gb_ref[...].astype(F32) * y_mla
    o_ref[...] = merged.astype(o_ref.dtype)


def _merge(a, bm, p, w_ret, w_mla, layer):
    t, d_in = a.shape
    d = w_ret.shape[2]
    tm = min(t, 1024)
    tn = 512
    return pl.pallas_call(
        _merge_kernel,
        grid=(t // tm, d // tn),
        in_specs=[
            pl.BlockSpec((tm, d_in), lambda i, j: (i, 0)),
            pl.BlockSpec((tm, d_in), lambda i, j: (i, 0)),
            pl.BlockSpec((None, d_in, tn), lambda i, j: (layer, 0, j)),
            pl.BlockSpec((None, d_in, tn), lambda i, j: (layer, 0, j)),
            pl.BlockSpec((tm, tn), lambda i, j: (i, C_BG // tn + j)),
            pl.BlockSpec((tm, tn), lambda i, j: (i, (C_BG + d) // tn + j)),
        ],
        out_specs=pl.BlockSpec((tm, tn), lambda i, j: (i, j)),
        out_shape=jax.ShapeDtypeStruct((t, d), BF16),
        compiler_params=_params(("parallel", "arbitrary")),
        name="merge_proj",
    )(a, bm, w_ret, w_mla, p, p)


def _out_kernel(m_ref, w_ref, x_ref, gate_ref, o_ref):
    out = jnp.dot(m_ref[...], w_ref[...], preferred_element_type=F32)
    o_ref[...] = x_ref[...] + gate_ref[...] * out


def _out_proj(merged, w_out, x2, mod4, layer, seq):
    t, d = x2.shape
    tm = min(seq, 1024)
    per_b = seq // tm
    tn = 512
    gate_blk = 2 * d // tn
    return pl.pallas_call(
        _out_kernel,
        grid=(t // tm, d // tn),
        in_specs=[
            pl.BlockSpec((tm, d), lambda i, j: (i, 0)),
            pl.BlockSpec((None, d, tn), lambda i, j: (layer, 0, j)),
            pl.BlockSpec((tm, tn), lambda i, j: (i, j)),
            pl.BlockSpec((None, None, 1, tn), lambda i, j: (layer, i // per_b, 0, gate_blk + j)),
        ],
        out_specs=pl.BlockSpec((tm, tn), lambda i, j: (i, j)),
        out_shape=jax.ShapeDtypeStruct((t, d), F32),
        compiler_params=_params(("parallel", "arbitrary")),
        name="out_proj",
    )(merged, w_out, x2, mod4)


def _final_norm_kernel(x_ref, g_ref, o_ref):
    x = x_ref[...]
    inv = lax.rsqrt(jnp.mean(x * x, axis=-1, keepdims=True) + EPS)
    o_ref[...] = (x * inv) * g_ref[...]


def _final_norm(x2, g_final):
    t, d = x2.shape
    tm = min(t, 512)
    return pl.pallas_call(
        _final_norm_kernel,
        grid=(t // tm,),
        in_specs=[pl.BlockSpec((tm, d), lambda i: (i, 0)), pl.BlockSpec((1, d), lambda i: (0, 0))],
        out_specs=pl.BlockSpec((tm, d), lambda i: (i, 0)),
        out_shape=jax.ShapeDtypeStruct((t, d), F32),
        compiler_params=_params(("parallel",)),
        name="final_norm",
    )(x2, g_final.reshape(1, d))


def kernel(x, c, positions, w_mod, b_mod, g_norm, w_in, g_cq, g_ckv, w_uq, w_ukv,
           w_ret_proj, w_mla_proj, w_out, g_final):
    bsz, seq, d = x.shape
    depth = w_in.shape[0]
    t = bsz * seq
    assert seq % RET_BLOCK == 0 and seq % min(seq, ATT_BLOCK) == 0 and d % IN_TN == 0

    kr_lo = C_CKV + MLA_KV_RANK
    w_main = jnp.concatenate([w_in[:, :, :kr_lo], w_in[:, :, kr_lo + MLA_ROPE:]], axis=2).astype(BF16)
    w_kr = jnp.pad(w_in[:, :, kr_lo:kr_lo + MLA_ROPE], ((0, 0), (0, 0), (0, LANES - MLA_ROPE))).astype(BF16)
    w_uq_p = jnp.pad(
        w_uq.reshape(depth, MLA_Q_RANK, MLA_HEADS, MLA_NOPE + MLA_ROPE),
        ((0, 0), (0, 0), (0, 0), (0, MLA_QK_PAD - MLA_NOPE - MLA_ROPE)),
    ).reshape(depth, MLA_Q_RANK, MLA_HEADS * MLA_QK_PAD).astype(BF16)
    w_ukv4 = w_ukv.reshape(depth, MLA_KV_RANK, MLA_HEADS, MLA_NOPE + MLA_DV)
    w_k = w_ukv4[..., :MLA_NOPE].reshape(depth, MLA_KV_RANK, MLA_HEADS * MLA_NOPE).astype(BF16)
    w_v = w_ukv4[..., MLA_NOPE:].reshape(depth, MLA_KV_RANK, MLA_V_W).astype(BF16)
    w_ret = w_ret_proj.astype(BF16)
    w_mla = w_mla_proj.astype(BF16)
    w_o = w_out.astype(BF16)

    tabs = _rope_tables(positions)
    mod = _modulation(c, w_mod, b_mod)
    mod4 = mod.reshape(depth, mod.shape[1], 1, 3 * d)
    g_norm3 = g_norm.reshape(depth, 1, d)
    g_cq3 = g_cq.reshape(depth, 1, MLA_Q_RANK)
    g_ckv3 = g_ckv.reshape(depth, 1, MLA_KV_RANK)

    x2 = x.reshape(t, d)
    for layer in range(depth):
        p, kr = _in_proj(x2, mod4, g_norm3, w_main, w_kr, tabs, g_cq3, g_ckv3, layer, seq)
        qc = _q_up(p, w_uq_p, tabs, layer)
        kc, v = _kv_up(p, kr, w_k, w_v, layer)
        a = _retention(p, bsz, seq)
        bm = _attention(qc, kc, v, p, bsz, seq)
        merged = _merge(a, bm, p, w_ret, w_mla, layer)
        x2 = _out_proj(merged, w_o, x2, mod4, layer, seq)
    return _final_norm(x2, g_final).reshape(bsz, seq, d)
```

```python
import jax
import jax.numpy as jnp
from jax import lax
from jax.experimental import pallas as pl
from jax.experimental.pallas import tpu as pltpu

F32 = jnp.float32
BF16 = jnp.bfloat16

CHUNK = 64
EPS = 1e-6
NEG_INF = -1e30
ROPE_BASE = 10000.0
RET_HEADS = 8
RET_DK = 128
RET_DV = 256
RET_QK_W = RET_HEADS * RET_DK
RET_V_W = RET_HEADS * RET_DV
MLA_HEADS = 16
MLA_Q_RANK = 512
MLA_KV_RANK = 512
MLA_NOPE = 128
MLA_ROPE = 64
MLA_DV = 128
MLA_V_W = MLA_HEADS * MLA_DV
MLA_QK_PAD = 256
MLA_V_PAD = 2 * MLA_DV
LOG2E = 1.4426950408889634

LANES = 128
VMEM_LIMIT = 56 * 1024 * 1024

C_RQ = 0
C_RK = C_RQ + RET_QK_W
C_RV = C_RK + RET_QK_W
C_RG = C_RV + RET_V_W
C_CQ = C_RG + RET_V_W
C_CKV = C_CQ + MLA_Q_RANK
C_MG = C_CKV + MLA_KV_RANK
C_BG = C_MG + MLA_V_W
IN_TN = 1024
IN_CHUNK = 256


def _params(sem):
    return pltpu.CompilerParams(dimension_semantics=sem, vmem_limit_bytes=VMEM_LIMIT)


def _sigmoid(x):
    return 1.0 / (1.0 + jnp.exp(-x))


def _rope_tables_kernel(pos_ref, invr_ref, invm_ref, cr_ref, sr_ref, cm_ref, sm1_ref, sm2_ref):
    pos = pos_ref[...].astype(F32)
    lane = lax.broadcasted_iota(jnp.int32, cr_ref.shape, 1)
    ang_r = pos * invr_ref[...]
    cr_ref[...] = jnp.cos(ang_r)
    sin_r = jnp.sin(ang_r)
    sr_ref[...] = jnp.where(lane < RET_DK // 2, -sin_r, sin_r)
    ang_m = pos * invm_ref[...]
    cos_m = jnp.cos(ang_m)
    sin_m = jnp.sin(ang_m)
    half = MLA_ROPE // 2
    cm_ref[...] = jnp.where(lane < MLA_ROPE, cos_m, 0.0)
    sm1_ref[...] = jnp.where(lane < half, -sin_m, 0.0)
    sm2_ref[...] = jnp.where((lane >= half) & (lane < MLA_ROPE), sin_m, 0.0)


def _rope_tables(positions):
    t = positions.size
    tm = min(t, 1024)
    inv_r = 1.0 / (ROPE_BASE ** (jnp.arange(0, RET_DK, 2, dtype=F32) / RET_DK))
    inv_m = 1.0 / (ROPE_BASE ** (jnp.arange(0, MLA_ROPE, 2, dtype=F32) / MLA_ROPE))
    invr = jnp.concatenate([inv_r, inv_r])[None, :]
    invm = jnp.concatenate([inv_m, inv_m, jnp.zeros((LANES - MLA_ROPE,), F32)])[None, :]
    row = pl.BlockSpec((tm, LANES), lambda i: (i, 0))
    const = pl.BlockSpec((1, LANES), lambda i: (0, 0))
    return pl.pallas_call(
        _rope_tables_kernel,
        grid=(t // tm,),
        in_specs=[pl.BlockSpec((tm, 1), lambda i: (i, 0)), const, const],
        out_specs=[row] * 5,
        out_shape=[jax.ShapeDtypeStruct((t, LANES), F32)] * 5,
        compiler_params=_params(("parallel",)),
        name="rope_tables",
    )(positions.reshape(t, 1), invr, invm)


def _mod_kernel(c_ref, w_ref, b_ref, o_ref):
    c = c_ref[...]
    ca = (c * _sigmoid(c)).astype(BF16)
    o_ref[...] = jnp.dot(ca, w_ref[...].astype(BF16), preferred_element_type=F32) + b_ref[...]


def _modulation(c, w_mod, b_mod):
    depth, d, n = w_mod.shape
    bsz = c.shape[0]
    rows = 8
    cp = jnp.pad(c, ((0, rows - bsz), (0, 0)))
    tn = 512
    return pl.pallas_call(
        _mod_kernel,
        grid=(depth, n // tn),
        in_specs=[
            pl.BlockSpec((rows, d), lambda l, j: (0, 0)),
            pl.BlockSpec((None, d, tn), lambda l, j: (l, 0, j)),
            pl.BlockSpec((None, 1, tn), lambda l, j: (l, 0, j)),
        ],
        out_specs=pl.BlockSpec((None, rows, tn), lambda l, j: (l, 0, j)),
        out_shape=jax.ShapeDtypeStruct((depth, rows, n), F32),
        compiler_params=_params(("parallel", "parallel")),
        name="modulation",
    )(cp, w_mod, b_mod.reshape(depth, 1, n))


def _rope_ret(a, cos, sin):
    outs = []
    for hh in range(a.shape[1] // RET_DK):
        t = a[:, hh * RET_DK:(hh + 1) * RET_DK]
        outs.append(t * cos + pltpu.roll(t, RET_DK // 2, 1) * sin)
    return jnp.concatenate(outs, axis=1) if len(outs) > 1 else outs[0]


def _rope_mla(t, cos, sin_lo, sin_hi):
    half = MLA_ROPE // 2
    return t * cos + pltpu.roll(t, LANES - half, 1) * sin_lo + pltpu.roll(t, half, 1) * sin_hi


def _in_proj_kernel(x_ref, shift_ref, scale_ref, g_ref, wlo_ref, whi_ref, wkr_ref, cr_ref, sr_ref,
                    cm_ref, sm1_ref, sm2_ref, gcq_ref, gckv_ref, p_ref, kr_ref, h_scr):
    j = pl.program_id(1)

    @pl.when(j == 0)
    def _():
        x = x_ref[...]
        inv = lax.rsqrt(jnp.mean(x * x, axis=-1, keepdims=True) + EPS)
        h = (x * inv) * g_ref[...]
        h = h * (1.0 + scale_ref[...]) + shift_ref[...]
        hb = h.astype(BF16)
        h_scr[...] = hb
        kr = jnp.dot(hb, wkr_ref[...], preferred_element_type=F32)
        kr_ref[...] = _rope_mla(kr, cm_ref[...], sm1_ref[...], sm2_ref[...]).astype(kr_ref.dtype)

    def tile_range(lo, width):
        return (j >= lo // IN_TN) & (j < (lo + width) // IN_TN)

    def project(w_ref, epilogues, chunk=IN_CHUNK):
        for idx, c0 in enumerate(range(0, IN_TN, chunk)):
            acc = jnp.dot(h_scr[...], w_ref[:, c0:c0 + chunk], preferred_element_type=F32)
            p_ref[:, c0:c0 + chunk] = epilogues[idx % len(epilogues)](acc).astype(p_ref.dtype)

    def rope(acc):
        return _rope_ret(acc, cr_ref[...], sr_ref[...])

    def latent_norm(g_latent_ref):
        def apply(acc):
            inv = lax.rsqrt(jnp.mean(acc * acc, axis=-1, keepdims=True) + EPS)
            return (acc * inv) * g_latent_ref[...]
        return apply

    @pl.when(tile_range(C_RQ, RET_QK_W))
    def _():
        project(wlo_ref, [rope])

    @pl.when(tile_range(C_RK, RET_QK_W))
    def _():
        project(wlo_ref, [lambda acc: rope(acc) * (RET_DK ** -0.5)])

    @pl.when(tile_range(C_RV, RET_V_W))
    def _():
        project(wlo_ref, [lambda acc: acc])

    def silu(acc):
        return acc * _sigmoid(acc)

    @pl.when(tile_range(C_RG, RET_V_W))
    def _():
        project(wlo_ref, [silu])

    @pl.when(tile_range(C_MG, MLA_V_W))
    def _():
        project(whi_ref, [silu])

    @pl.when(tile_range(C_CQ, MLA_Q_RANK + MLA_KV_RANK))
    def _():
        project(wlo_ref, [latent_norm(gcq_ref), latent_norm(gckv_ref)], chunk=MLA_Q_RANK)

    @pl.when(j >= C_BG // IN_TN)
    def _():
        project(whi_ref, [_sigmoid])


def _in_proj(x2, mod4, g_norm3, w_lo, w_hi, w_kr, tabs, g_cq3, g_ckv3, layer, seq):
    t, d = x2.shape
    lo_tiles = w_lo.shape[2] // IN_TN
    n = w_lo.shape[2] + w_hi.shape[2]
    tm = min(seq, 1024)
    per_b = seq // tm
    cr, sr, cm, sm1, sm2 = tabs
    tab = pl.BlockSpec((tm, LANES), lambda i, j: (i, 0))
    return pl.pallas_call(
        _in_proj_kernel,
        grid=(t // tm, n // IN_TN),
        in_specs=[
            pl.BlockSpec((tm, d), lambda i, j: (i, 0)),
            pl.BlockSpec((None, None, 1, d), lambda i, j: (layer, i // per_b, 0, 0)),
            pl.BlockSpec((None, None, 1, d), lambda i, j: (layer, i // per_b, 0, 1)),
            pl.BlockSpec((None, 1, d), lambda i, j: (layer, 0, 0)),
            pl.BlockSpec((None, d, IN_TN), lambda i, j: (layer, 0, jnp.minimum(j, lo_tiles - 1))),
            pl.BlockSpec((None, d, IN_TN), lambda i, j: (layer, 0, jnp.maximum(j - lo_tiles, 0))),
            pl.BlockSpec((None, d, LANES), lambda i, j: (layer, 0, 0)),
            tab, tab, tab, tab, tab,
            pl.BlockSpec((None, 1, MLA_Q_RANK), lambda i, j: (layer, 0, 0)),
            pl.BlockSpec((None, 1, MLA_KV_RANK), lambda i, j: (layer, 0, 0)),
        ],
        out_specs=[
            pl.BlockSpec((tm, IN_TN), lambda i, j: (i, j)),
            pl.BlockSpec((tm, LANES), lambda i, j: (i, 0)),
        ],
        out_shape=[
            jax.ShapeDtypeStruct((t, n), BF16),
            jax.ShapeDtypeStruct((t, LANES), BF16),
        ],
        scratch_shapes=[pltpu.VMEM((tm, d), BF16)],
        compiler_params=_params(("parallel", "arbitrary")),
        name="in_proj",
    )(x2, mod4, mod4, g_norm3, w_lo, w_hi, w_kr, cr, sr, cm, sm1, sm2, g_cq3, g_ckv3)


UP_HEADS = 4


def _latent_up_kernel(cq_ref, ckv_ref, wq_ref, wk_ref, wv_ref, kr_ref, cm_ref, sm1_ref, sm2_ref,
                      q_ref, k_ref, v_ref):
    scale = (MLA_NOPE + MLA_ROPE) ** -0.5 * LOG2E
    cq = cq_ref[...]
    ckv = ckv_ref[...]
    cm, s1, s2 = cm_ref[...], sm1_ref[...], sm2_ref[...]
    kr = kr_ref[...]
    ones = jnp.ones((cq.shape[0], MLA_V_PAD - MLA_DV), v_ref.dtype)
    for hh in range(UP_HEADS):
        lo = hh * MLA_QK_PAD
        acc = jnp.dot(cq, wq_ref[:, lo:lo + MLA_QK_PAD], preferred_element_type=F32) * scale
        q_ref[:, lo:lo + MLA_NOPE] = acc[:, :MLA_NOPE].astype(q_ref.dtype)
        q_ref[:, lo + MLA_NOPE:lo + MLA_QK_PAD] = _rope_mla(acc[:, MLA_NOPE:], cm, s1, s2).astype(q_ref.dtype)
    kn = jnp.dot(ckv, wk_ref[...], preferred_element_type=F32)
    vv = jnp.dot(ckv, wv_ref[...], preferred_element_type=F32)
    for hh in range(UP_HEADS):
        lo = hh * MLA_QK_PAD
        k_ref[:, lo:lo + MLA_NOPE] = kn[:, hh * MLA_NOPE:(hh + 1) * MLA_NOPE].astype(k_ref.dtype)
        k_ref[:, lo + MLA_NOPE:lo + MLA_QK_PAD] = kr
        vlo = hh * MLA_V_PAD
        v_ref[:, vlo:vlo + MLA_DV] = vv[:, hh * MLA_DV:(hh + 1) * MLA_DV].astype(v_ref.dtype)
        v_ref[:, vlo + MLA_DV:vlo + MLA_V_PAD] = ones


def _latent_up(p, kr, w_uq_p, w_k, w_v, tabs, layer):
    t = p.shape[0]
    tm = min(t, 1024)
    _, _, cm, sm1, sm2 = tabs
    tab = pl.BlockSpec((tm, LANES), lambda i, j: (i, 0))

    def head_cols(width):
        return pl.BlockSpec((tm, UP_HEADS * width), lambda i, j: (i, j))

    return pl.pallas_call(
        _latent_up_kernel,
        grid=(t // tm, MLA_HEADS // UP_HEADS),
        in_specs=[
            pl.BlockSpec((tm, MLA_Q_RANK), lambda i, j: (i, C_CQ // MLA_Q_RANK)),
            pl.BlockSpec((tm, MLA_KV_RANK), lambda i, j: (i, C_CKV // MLA_KV_RANK)),
            pl.BlockSpec((None, MLA_Q_RANK, UP_HEADS * MLA_QK_PAD), lambda i, j: (layer, 0, j)),
            pl.BlockSpec((None, MLA_KV_RANK, UP_HEADS * MLA_NOPE), lambda i, j: (layer, 0, j)),
            pl.BlockSpec((None, MLA_KV_RANK, UP_HEADS * MLA_DV), lambda i, j: (layer, 0, j)),
            tab, tab, tab, tab,
        ],
        out_specs=[head_cols(MLA_QK_PAD), head_cols(MLA_QK_PAD), head_cols(MLA_V_PAD)],
        out_shape=[
            jax.ShapeDtypeStruct((t, MLA_HEADS * MLA_QK_PAD), BF16),
            jax.ShapeDtypeStruct((t, MLA_HEADS * MLA_QK_PAD), BF16),
            jax.ShapeDtypeStruct((t, MLA_HEADS * MLA_V_PAD), BF16),
        ],
        compiler_params=_params(("parallel", "arbitrary")),
        name="latent_up",
    )(p, p, w_uq_p, w_k, w_v, kr, cm, sm1, sm2)


RET_BLOCK = 256


RET_GROUP = 2


def _retention_kernel(q_ref, k_ref, v_ref, gate_ref, lg_ref, o_ref, state, dmat, xi, zeta):
    blk = dmat.shape[1]
    nblk = q_ref.shape[0] // blk

    state[...] = jnp.zeros_like(state)
    for hh in range(RET_GROUP):
        lg = lg_ref[hh]
        r = lax.broadcasted_iota(jnp.int32, (blk, blk), 0)
        c = lax.broadcasted_iota(jnp.int32, (blk, blk), 1)
        decay = jnp.exp(jnp.abs(r - c).astype(F32) * lg)
        dmat[hh] = jnp.where((c // CHUNK) <= (r // CHUNK), decay, 0.0)
        rx = lax.broadcasted_iota(jnp.int32, xi.shape[1:], 0).astype(F32)
        xi[hh] = jnp.exp((rx + 1.0) * lg[:, :RET_DV])
        rz = lax.broadcasted_iota(jnp.int32, zeta.shape[1:], 0).astype(F32)
        zeta[hh] = jnp.exp((blk - 1.0 - rz) * lg[:, :RET_DK])

    def body(j, carry):
        rows = pl.ds(pl.multiple_of(j * blk, blk), blk)
        for hh in range(RET_GROUP):
            qk_cols = slice(hh * RET_DK, (hh + 1) * RET_DK)
            v_cols = slice(hh * RET_DV, (hh + 1) * RET_DV)
            q = q_ref[rows, qk_cols]
            k = k_ref[rows, qk_cols]
            v = v_ref[rows, v_cols]
            s = lax.dot_general(q, k, (((1,), (1,)), ((), ())), preferred_element_type=F32)
            o = jnp.dot((s * dmat[hh]).astype(BF16), v, preferred_element_type=F32)
            st = state[hh]
            o = o + xi[hh] * jnp.dot(q, st.astype(BF16), preferred_element_type=F32)
            kz_t = (k.astype(F32) * zeta[hh]).T.astype(BF16)
            block_decay = jnp.exp(blk * lg_ref[hh][:, :RET_DV])
            state[hh] = st * block_decay + jnp.dot(kz_t, v, preferred_element_type=F32)

            mu = jnp.mean(o, axis=-1, keepdims=True)
            dlt = o - mu
            var = jnp.mean(dlt * dlt, axis=-1, keepdims=True)
            y = dlt * lax.rsqrt(var + EPS)
            o_ref[rows, v_cols] = (y * gate_ref[rows, v_cols].astype(F32)).astype(o_ref.dtype)
        return carry

    lax.fori_loop(0, nblk, body, 0)


def _retention(p, bsz, seq):
    t = p.shape[0]
    blk = RET_BLOCK
    g = RET_GROUP
    log_gamma = jnp.log(1.0 - 2.0 ** (-5.0 - jnp.arange(RET_HEADS, dtype=F32)))
    lg = jnp.broadcast_to(log_gamma[:, None, None], (RET_HEADS, 1, blk))
    return pl.pallas_call(
        _retention_kernel,
        grid=(bsz, RET_HEADS // g),
        in_specs=[
            pl.BlockSpec((seq, g * RET_DK), lambda b, h: (b, C_RQ // (g * RET_DK) + h)),
            pl.BlockSpec((seq, g * RET_DK), lambda b, h: (b, C_RK // (g * RET_DK) + h)),
            pl.BlockSpec((seq, g * RET_DV), lambda b, h: (b, C_RV // (g * RET_DV) + h)),
            pl.BlockSpec((seq, g * RET_DV), lambda b, h: (b, C_RG // (g * RET_DV) + h)),
            pl.BlockSpec((g, 1, blk), lambda b, h: (h, 0, 0)),
        ],
        out_specs=pl.BlockSpec((seq, g * RET_DV), lambda b, h: (b, h)),
        out_shape=jax.ShapeDtypeStruct((t, RET_V_W), BF16),
        scratch_shapes=[
            pltpu.VMEM((g, RET_DK, RET_DV), F32),
            pltpu.VMEM((g, blk, blk), F32),
            pltpu.VMEM((g, blk, RET_DV), F32),
            pltpu.VMEM((g, blk, RET_DK), F32),
        ],
        compiler_params=_params(("parallel", "parallel")),
        name="retention",
    )(p, p, p, p, lg)


ATT_BLOCK = 512
ATT_HEADS = 2


def _attention_kernel(q_ref, k_ref, v_ref, gate_ref, bias_ref, o_ref, s_scr, m_scr, acc_scr):
    seq = q_ref.shape[0]
    tq = s_scr.shape[1]
    nq = seq // tq

    m_scr[...] = jnp.full_like(m_scr, -jnp.inf)
    acc_scr[...] = jnp.zeros_like(acc_scr)

    def rows(blk):
        return pl.ds(pl.multiple_of(blk * tq, tq), tq)

    def slot(next_pair, pair, diagonal=False):
        masked = diagonal
        for hh in range(ATT_HEADS):
            qk_cols = slice(hh * MLA_QK_PAD, (hh + 1) * MLA_QK_PAD)
            v_cols = slice(hh * MLA_V_PAD, (hh + 1) * MLA_V_PAD)
            m_cols = slice(hh * LANES, (hh + 1) * LANES)
            if pair is not None:
                restart = jnp.where(pair[1] == 0, -jnp.inf, 0.0)
                s = s_scr[hh]
                if masked:
                    s = s + bias_ref[...]
                m_old = m_scr[:, m_cols] + restart
                m_new = jnp.maximum(m_old, jnp.max(s, axis=-1, keepdims=True))
                alpha = jnp.exp2(m_old - m_new)
                p = jnp.exp2(s - jnp.concatenate([m_new] * (tq // LANES), axis=1))
                m_scr[:, m_cols] = m_new
                v = v_ref[rows(pair[1]), v_cols]
                pv = jnp.dot(p.astype(BF16), v, preferred_element_type=F32)
                rescale = jnp.concatenate([alpha] * (MLA_V_PAD // LANES), axis=1)
                acc_scr[:, v_cols] = acc_scr[:, v_cols] * rescale + pv
            if next_pair is not None:
                q = q_ref[rows(next_pair[0]), qk_cols]
                k = k_ref[rows(next_pair[1]), qk_cols]
                s_scr[hh] = lax.dot_general(q, k, (((1,), (1,)), ((), ())), preferred_element_type=F32)

        if diagonal:
            qi = pair[0]
            for hh in range(ATT_HEADS):
                tot = acc_scr[:, hh * MLA_V_PAD:(hh + 1) * MLA_V_PAD]
                o = tot[:, :MLA_DV] / tot[:, MLA_DV:]
                cols = slice(hh * MLA_DV, (hh + 1) * MLA_DV)
                g = gate_ref[rows(qi), cols].astype(F32)
                o_ref[rows(qi), cols] = (o * g).astype(o_ref.dtype)

    zero = jnp.int32(0)
    slot((zero, zero), None)

    def query_block(qi, carry):
        def two_pairs(i, c):
            kb = 2 * i
            slot((qi, kb + 1), (qi, kb))
            slot((qi, kb + 2), (qi, kb + 1))
            return c

        lax.fori_loop(0, qi // 2, two_pairs, 0)

        @pl.when(qi % 2 == 1)
        def _():
            slot((qi, qi), (qi, qi - 1))

        @pl.when(qi < nq - 1)
        def _():
            slot((qi + 1, zero), (qi, qi), diagonal=True)

        @pl.when(qi == nq - 1)
        def _():
            slot(None, (qi, qi), diagonal=True)

        return carry

    lax.fori_loop(0, nq, query_block, 0)


def _attention(qc, kc, v, p, bsz, seq):
    t = qc.shape[0]
    tq = min(seq, ATT_BLOCK)
    nh = ATT_HEADS
    r = lax.broadcasted_iota(jnp.int32, (tq, tq), 0) // CHUNK
    c = lax.broadcasted_iota(jnp.int32, (tq, tq), 1) // CHUNK
    bias = jnp.where(c <= r, 0.0, NEG_INF * LOG2E).astype(F32)
    return pl.pallas_call(
        _attention_kernel,
        grid=(bsz, MLA_HEADS // nh),
        in_specs=[
            pl.BlockSpec((seq, nh * MLA_QK_PAD), lambda b, h: (b, h)),
            pl.BlockSpec((seq, nh * MLA_QK_PAD), lambda b, h: (b, h)),
            pl.BlockSpec((seq, nh * MLA_V_PAD), lambda b, h: (b, h)),
            pl.BlockSpec((seq, nh * MLA_DV), lambda b, h: (b, C_MG // (nh * MLA_DV) + h)),
            pl.BlockSpec((tq, tq), lambda b, h: (0, 0)),
        ],
        out_specs=pl.BlockSpec((seq, nh * MLA_DV), lambda b, h: (b, h)),
        out_shape=jax.ShapeDtypeStruct((t, MLA_V_W), BF16),
        scratch_shapes=[
            pltpu.VMEM((nh, tq, tq), F32),
            pltpu.VMEM((tq, nh * LANES), F32),
            pltpu.VMEM((tq, nh * MLA_V_PAD), F32),
        ],
        compiler_params=_params(("parallel", "parallel")),
        name="attention",
    )(qc, kc, v, p, bias)


PROJ_CHUNK = 256


def _merge_kernel(a_ref, b_ref, wr_ref, wm_ref, ga_ref, gb_ref, o_ref):
    for c0 in range(0, o_ref.shape[1], PROJ_CHUNK):
        cols = slice(c0, c0 + PROJ_CHUNK)
        y_ret = jnp.dot(a_ref[...], wr_ref[:, cols], preferred_element_type=F32)
        y_mla = jnp.dot(b_ref[...], wm_ref[:, cols], preferred_element_type=F32)
        merged = ga_ref[:, cols].astype(F32) * y_ret + gb_ref[:, cols].astype(F32) * y_mla
        o_ref[:, cols] = merged.astype(o_ref.dtype)


def _merge(a, bm, p, w_ret, w_mla, layer):
    t, d_in = a.shape
    d = w_ret.shape[2]
    tm = min(t, 1024)
    tn = 512
    return pl.pallas_call(
        _merge_kernel,
        grid=(t // tm, d // tn),
        in_specs=[
            pl.BlockSpec((tm, d_in), lambda i, j: (i, 0)),
            pl.BlockSpec((tm, d_in), lambda i, j: (i, 0)),
            pl.BlockSpec((None, d_in, tn), lambda i, j: (layer, 0, j)),
            pl.BlockSpec((None, d_in, tn), lambda i, j: (layer, 0, j)),
            pl.BlockSpec((tm, tn), lambda i, j: (i, C_BG // tn + j)),
            pl.BlockSpec((tm, tn), lambda i, j: (i, (C_BG + d) // tn + j)),
        ],
        out_specs=pl.BlockSpec((tm, tn), lambda i, j: (i, j)),
        out_shape=jax.ShapeDtypeStruct((t, d), BF16),
        compiler_params=_params(("parallel", "arbitrary")),
        name="merge_proj",
    )(a, bm, w_ret, w_mla, p, p)


def _out_kernel(m_ref, w_ref, x_ref, gate_ref, o_ref):
    for c0 in range(0, o_ref.shape[1], PROJ_CHUNK):
        cols = slice(c0, c0 + PROJ_CHUNK)
        out = jnp.dot(m_ref[...], w_ref[:, cols], preferred_element_type=F32)
        o_ref[:, cols] = x_ref[:, cols] + gate_ref[:, cols] * out


def _out_proj(merged, w_out, x2, mod4, layer, seq):
    t, d = x2.shape
    tm = min(seq, 1024)
    per_b = seq // tm
    tn = 1024
    gate_blk = 2 * d // tn
    return pl.pallas_call(
        _out_kernel,
        grid=(t // tm, d // tn),
        in_specs=[
            pl.BlockSpec((tm, d), lambda i, j: (i, 0)),
            pl.BlockSpec((None, d, tn), lambda i, j: (layer, 0, j)),
            pl.BlockSpec((tm, tn), lambda i, j: (i, j)),
            pl.BlockSpec((None, None, 1, tn), lambda i, j: (layer, i // per_b, 0, gate_blk + j)),
        ],
        out_specs=pl.BlockSpec((tm, tn), lambda i, j: (i, j)),
        out_shape=jax.ShapeDtypeStruct((t, d), F32),
        compiler_params=_params(("parallel", "arbitrary")),
        name="out_proj",
    )(merged, w_out, x2, mod4)


def _final_norm_kernel(x_ref, g_ref, o_ref):
    x = x_ref[...]
    inv = lax.rsqrt(jnp.mean(x * x, axis=-1, keepdims=True) + EPS)
    o_ref[...] = (x * inv) * g_ref[...]


def _final_norm(x2, g_final):
    t, d = x2.shape
    tm = min(t, 512)
    return pl.pallas_call(
        _final_norm_kernel,
        grid=(t // tm,),
        in_specs=[pl.BlockSpec((tm, d), lambda i: (i, 0)), pl.BlockSpec((1, d), lambda i: (0, 0))],
        out_specs=pl.BlockSpec((tm, d), lambda i: (i, 0)),
        out_shape=jax.ShapeDtypeStruct((t, d), F32),
        compiler_params=_params(("parallel",)),
        name="final_norm",
    )(x2, g_final.reshape(1, d))


def kernel(x, c, positions, w_mod, b_mod, g_norm, w_in, g_cq, g_ckv, w_uq, w_ukv,
           w_ret_proj, w_mla_proj, w_out, g_final):
    bsz, seq, d = x.shape
    depth = w_in.shape[0]
    t = bsz * seq
    assert seq % RET_BLOCK == 0 and seq % min(seq, ATT_BLOCK) == 0 and d % IN_TN == 0

    kr_lo = C_CKV + MLA_KV_RANK
    w_lo = w_in[:, :, :kr_lo].astype(BF16)
    w_hi = w_in[:, :, kr_lo + MLA_ROPE:].astype(BF16)
    w_kr = jnp.pad(w_in[:, :, kr_lo:kr_lo + MLA_ROPE], ((0, 0), (0, 0), (0, LANES - MLA_ROPE))).astype(BF16)
    w_uq_p = jnp.pad(
        w_uq.reshape(depth, MLA_Q_RANK, MLA_HEADS, MLA_NOPE + MLA_ROPE),
        ((0, 0), (0, 0), (0, 0), (0, MLA_QK_PAD - MLA_NOPE - MLA_ROPE)),
    ).reshape(depth, MLA_Q_RANK, MLA_HEADS * MLA_QK_PAD).astype(BF16)
    w_ukv4 = w_ukv.reshape(depth, MLA_KV_RANK, MLA_HEADS, MLA_NOPE + MLA_DV)
    w_k = w_ukv4[..., :MLA_NOPE].reshape(depth, MLA_KV_RANK, MLA_HEADS * MLA_NOPE).astype(BF16)
    w_v = w_ukv4[..., MLA_NOPE:].reshape(depth, MLA_KV_RANK, MLA_V_W).astype(BF16)
    w_ret = w_ret_proj.astype(BF16)
    w_mla = w_mla_proj.astype(BF16)
    w_o = w_out.astype(BF16)

    tabs = _rope_tables(positions)
    mod = _modulation(c, w_mod, b_mod)
    mod4 = mod.reshape(depth, mod.shape[1], 1, 3 * d)
    g_norm3 = g_norm.reshape(depth, 1, d)
    g_cq3 = g_cq.reshape(depth, 1, MLA_Q_RANK)
    g_ckv3 = g_ckv.reshape(depth, 1, MLA_KV_RANK)

    x2 = x.reshape(t, d)
    for layer in range(depth):
        p, kr = _in_proj(x2, mod4, g_norm3, w_lo, w_hi, w_kr, tabs, g_cq3, g_ckv3, layer, seq)
        qc, kc, v = _latent_up(p, kr, w_uq_p, w_k, w_v, tabs, layer)
        a = _retention(p, bsz, seq)
        bm = _attention(qc, kc, v, p, bsz, seq)
        merged = _merge(a, bm, p, w_ret, w_mla, layer)
        x2 = _out_proj(merged, w_o, x2, mod4, layer, seq)
    return _final_norm(x2, g_final).reshape(bsz, seq, d)
```

```python
import jax
import jax.numpy as jnp
from jax import lax
from jax.experimental import pallas as pl
from jax.experimental.pallas import tpu as pltpu

F32 = jnp.float32
BF16 = jnp.bfloat16

CHUNK = 64
EPS = 1e-6
NEG_INF = -1e30
ROPE_BASE = 10000.0
RET_HEADS = 8
RET_DK = 128
RET_DV = 256
RET_QK_W = RET_HEADS * RET_DK
RET_V_W = RET_HEADS * RET_DV
MLA_HEADS = 16
MLA_Q_RANK = 512
MLA_KV_RANK = 512
MLA_NOPE = 128
MLA_ROPE = 64
MLA_DV = 128
MLA_V_W = MLA_HEADS * MLA_DV
MLA_QK_PAD = 256
MLA_V_PAD = 2 * MLA_DV
LOG2E = 1.4426950408889634

LANES = 128
VMEM_LIMIT = 56 * 1024 * 1024

C_RQ = 0
C_RK = C_RQ + RET_QK_W
C_RV = C_RK + RET_QK_W
C_RG = C_RV + RET_V_W
C_CQ = C_RG + RET_V_W
C_CKV = C_CQ + MLA_Q_RANK
C_MG = C_CKV + MLA_KV_RANK
C_BG = C_MG + MLA_V_W
IN_TN = 1024
IN_CHUNK = 256


def _params(sem):
    return pltpu.CompilerParams(dimension_semantics=sem, vmem_limit_bytes=VMEM_LIMIT)


def _sigmoid(x):
    return 1.0 / (1.0 + jnp.exp(-x))


def _rope_tables_kernel(pos_ref, invr_ref, invm_ref, cr_ref, sr_ref, cm_ref, sm1_ref, sm2_ref):
    pos = pos_ref[...].astype(F32)
    lane = lax.broadcasted_iota(jnp.int32, cr_ref.shape, 1)
    ang_r = pos * invr_ref[...]
    cr_ref[...] = jnp.cos(ang_r)
    sin_r = jnp.sin(ang_r)
    sr_ref[...] = jnp.where(lane < RET_DK // 2, -sin_r, sin_r)
    ang_m = pos * invm_ref[...]
    cos_m = jnp.cos(ang_m)
    sin_m = jnp.sin(ang_m)
    half = MLA_ROPE // 2
    cm_ref[...] = jnp.where(lane < MLA_ROPE, cos_m, 0.0)
    sm1_ref[...] = jnp.where(lane < half, -sin_m, 0.0)
    sm2_ref[...] = jnp.where((lane >= half) & (lane < MLA_ROPE), sin_m, 0.0)


def _rope_tables(positions):
    t = positions.size
    tm = min(t, 1024)
    inv_r = 1.0 / (ROPE_BASE ** (jnp.arange(0, RET_DK, 2, dtype=F32) / RET_DK))
    inv_m = 1.0 / (ROPE_BASE ** (jnp.arange(0, MLA_ROPE, 2, dtype=F32) / MLA_ROPE))
    invr = jnp.concatenate([inv_r, inv_r])[None, :]
    invm = jnp.concatenate([inv_m, inv_m, jnp.zeros((LANES - MLA_ROPE,), F32)])[None, :]
    row = pl.BlockSpec((tm, LANES), lambda i: (i, 0))
    const = pl.BlockSpec((1, LANES), lambda i: (0, 0))
    return pl.pallas_call(
        _rope_tables_kernel,
        grid=(t // tm,),
        in_specs=[pl.BlockSpec((tm, 1), lambda i: (i, 0)), const, const],
        out_specs=[row] * 5,
        out_shape=[jax.ShapeDtypeStruct((t, LANES), F32)] * 5,
        compiler_params=_params(("parallel",)),
        name="rope_tables",
    )(positions.reshape(t, 1), invr, invm)


def _mod_kernel(c_ref, w_ref, b_ref, o_ref):
    c = c_ref[...]
    ca = (c * _sigmoid(c)).astype(BF16)
    o_ref[...] = jnp.dot(ca, w_ref[...].astype(BF16), preferred_element_type=F32) + b_ref[...]


def _modulation(c, w_mod, b_mod):
    depth, d, n = w_mod.shape
    bsz = c.shape[0]
    rows = 8
    cp = jnp.pad(c, ((0, rows - bsz), (0, 0)))
    tn = 512
    return pl.pallas_call(
        _mod_kernel,
        grid=(depth, n // tn),
        in_specs=[
            pl.BlockSpec((rows, d), lambda l, j: (0, 0)),
            pl.BlockSpec((None, d, tn), lambda l, j: (l, 0, j)),
            pl.BlockSpec((None, 1, tn), lambda l, j: (l, 0, j)),
        ],
        out_specs=pl.BlockSpec((None, rows, tn), lambda l, j: (l, 0, j)),
        out_shape=jax.ShapeDtypeStruct((depth, rows, n), F32),
        compiler_params=_params(("parallel", "parallel")),
        name="modulation",
    )(cp, w_mod, b_mod.reshape(depth, 1, n))


def _rope_ret(a, cos, sin):
    outs = []
    for hh in range(a.shape[1] // RET_DK):
        t = a[:, hh * RET_DK:(hh + 1) * RET_DK]
        outs.append(t * cos + pltpu.roll(t, RET_DK // 2, 1) * sin)
    return jnp.concatenate(outs, axis=1) if len(outs) > 1 else outs[0]


def _rope_mla(t, cos, sin_lo, sin_hi):
    half = MLA_ROPE // 2
    return t * cos + pltpu.roll(t, LANES - half, 1) * sin_lo + pltpu.roll(t, half, 1) * sin_hi


def _in_proj_kernel(x_ref, shift_ref, scale_ref, g_ref, wlo_ref, whi_ref, wkr_ref, cr_ref, sr_ref,
                    cm_ref, sm1_ref, sm2_ref, gcq_ref, gckv_ref, p_ref, kr_ref, h_scr):
    j = pl.program_id(1)

    @pl.when(j == 0)
    def _():
        x = x_ref[...]
        inv = lax.rsqrt(jnp.mean(x * x, axis=-1, keepdims=True) + EPS)
        h = (x * inv) * g_ref[...]
        h = h * (1.0 + scale_ref[...]) + shift_ref[...]
        hb = h.astype(BF16)
        h_scr[...] = hb
        kr = jnp.dot(hb, wkr_ref[...], preferred_element_type=F32)
        kr_ref[...] = _rope_mla(kr, cm_ref[...], sm1_ref[...], sm2_ref[...]).astype(kr_ref.dtype)

    def tile_range(lo, width):
        return (j >= lo // IN_TN) & (j < (lo + width) // IN_TN)

    def project(w_ref, epilogues, chunk=IN_CHUNK):
        for idx, c0 in enumerate(range(0, IN_TN, chunk)):
            acc = jnp.dot(h_scr[...], w_ref[:, c0:c0 + chunk], preferred_element_type=F32)
            p_ref[:, c0:c0 + chunk] = epilogues[idx % len(epilogues)](acc).astype(p_ref.dtype)

    def rope(acc):
        return _rope_ret(acc, cr_ref[...], sr_ref[...])

    def latent_norm(g_latent_ref):
        def apply(acc):
            inv = lax.rsqrt(jnp.mean(acc * acc, axis=-1, keepdims=True) + EPS)
            return (acc * inv) * g_latent_ref[...]
        return apply

    @pl.when(tile_range(C_RQ, RET_QK_W))
    def _():
        project(wlo_ref, [rope])

    @pl.when(tile_range(C_RK, RET_QK_W))
    def _():
        project(wlo_ref, [lambda acc: rope(acc) * (RET_DK ** -0.5)])

    @pl.when(tile_range(C_RV, RET_V_W))
    def _():
        project(wlo_ref, [lambda acc: acc])

    def silu(acc):
        return acc * _sigmoid(acc)

    @pl.when(tile_range(C_RG, RET_V_W))
    def _():
        project(wlo_ref, [silu])

    @pl.when(tile_range(C_MG, MLA_V_W))
    def _():
        project(whi_ref, [silu])

    @pl.when(tile_range(C_CQ, MLA_Q_RANK + MLA_KV_RANK))
    def _():
        project(wlo_ref, [latent_norm(gcq_ref), latent_norm(gckv_ref)], chunk=MLA_Q_RANK)

    @pl.when(j >= C_BG // IN_TN)
    def _():
        project(whi_ref, [_sigmoid])


def _in_proj(x2, mod4, g_norm3, w_lo, w_hi, w_kr, tabs, g_cq3, g_ckv3, layer, seq):
    t, d = x2.shape
    lo_tiles = C_MG // IN_TN
    n = C_MG + w_hi.shape[2]
    tm = min(seq, 1024)
    per_b = seq // tm
    cr, sr, cm, sm1, sm2 = tabs
    tab = pl.BlockSpec((tm, LANES), lambda i, j: (i, 0))
    return pl.pallas_call(
        _in_proj_kernel,
        grid=(t // tm, n // IN_TN),
        in_specs=[
            pl.BlockSpec((tm, d), lambda i, j: (i, 0)),
            pl.BlockSpec((None, None, 1, d), lambda i, j: (layer, i // per_b, 0, 0)),
            pl.BlockSpec((None, None, 1, d), lambda i, j: (layer, i // per_b, 0, 1)),
            pl.BlockSpec((None, 1, d), lambda i, j: (layer, 0, 0)),
            pl.BlockSpec((None, d, IN_TN), lambda i, j: (layer, 0, jnp.minimum(j, lo_tiles - 1))),
            pl.BlockSpec((None, d, IN_TN), lambda i, j: (layer, 0, jnp.maximum(j - lo_tiles, 0))),
            pl.BlockSpec((None, d, LANES), lambda i, j: (layer, 0, 0)),
            tab, tab, tab, tab, tab,
            pl.BlockSpec((None, 1, MLA_Q_RANK), lambda i, j: (layer, 0, 0)),
            pl.BlockSpec((None, 1, MLA_KV_RANK), lambda i, j: (layer, 0, 0)),
        ],
        out_specs=[
            pl.BlockSpec((tm, IN_TN), lambda i, j: (i, j)),
            pl.BlockSpec((tm, LANES), lambda i, j: (i, 0)),
        ],
        out_shape=[
            jax.ShapeDtypeStruct((t, n), BF16),
            jax.ShapeDtypeStruct((t, LANES), BF16),
        ],
        scratch_shapes=[pltpu.VMEM((tm, d), BF16)],
        compiler_params=_params(("parallel", "arbitrary")),
        name="in_proj",
    )(x2, mod4, mod4, g_norm3, w_lo, w_hi, w_kr, cr, sr, cm, sm1, sm2, g_cq3, g_ckv3)


UP_HEADS = 4


def _latent_up_kernel(cq_ref, ckv_ref, wq_ref, wk_ref, wv_ref, kr_ref, cm_ref, sm1_ref, sm2_ref,
                      q_ref, k_ref, v_ref):
    scale = (MLA_NOPE + MLA_ROPE) ** -0.5 * LOG2E
    cq = cq_ref[...]
    ckv = ckv_ref[...]
    cm, s1, s2 = cm_ref[...], sm1_ref[...], sm2_ref[...]
    kr = kr_ref[...]
    for hh in range(UP_HEADS):
        lo = hh * MLA_QK_PAD
        acc = jnp.dot(cq, wq_ref[:, lo:lo + MLA_QK_PAD], preferred_element_type=F32) * scale
        q_ref[:, lo:lo + MLA_NOPE] = acc[:, :MLA_NOPE].astype(q_ref.dtype)
        q_ref[:, lo + MLA_NOPE:lo + MLA_QK_PAD] = _rope_mla(acc[:, MLA_NOPE:], cm, s1, s2).astype(q_ref.dtype)
    kn = jnp.dot(ckv, wk_ref[...], preferred_element_type=F32)
    vv = jnp.dot(ckv, wv_ref[...], preferred_element_type=F32)
    ones = jnp.ones((cq.shape[0], MLA_V_PAD - MLA_DV), v_ref.dtype)
    for hh in range(UP_HEADS):
        lo = hh * MLA_QK_PAD
        k_ref[:, lo:lo + MLA_NOPE] = kn[:, hh * MLA_NOPE:(hh + 1) * MLA_NOPE].astype(k_ref.dtype)
        k_ref[:, lo + MLA_NOPE:lo + MLA_QK_PAD] = kr
        vlo = hh * MLA_V_PAD
        v_ref[:, vlo:vlo + MLA_DV] = vv[:, hh * MLA_DV:(hh + 1) * MLA_DV].astype(v_ref.dtype)
        v_ref[:, vlo + MLA_DV:vlo + MLA_V_PAD] = ones


def _latent_up(p, kr, w_uq_p, w_k, w_v, tabs, layer):
    t = p.shape[0]
    tm = min(t, 1024)
    _, _, cm, sm1, sm2 = tabs
    tab = pl.BlockSpec((tm, LANES), lambda i, j: (i, 0))

    def head_cols(width):
        return pl.BlockSpec((tm, UP_HEADS * width), lambda i, j: (i, j))

    return pl.pallas_call(
        _latent_up_kernel,
        grid=(t // tm, MLA_HEADS // UP_HEADS),
        in_specs=[
            pl.BlockSpec((tm, MLA_Q_RANK), lambda i, j: (i, C_CQ // MLA_Q_RANK)),
            pl.BlockSpec((tm, MLA_KV_RANK), lambda i, j: (i, C_CKV // MLA_KV_RANK)),
            pl.BlockSpec((None, MLA_Q_RANK, UP_HEADS * MLA_QK_PAD), lambda i, j: (layer, 0, j)),
            pl.BlockSpec((None, MLA_KV_RANK, UP_HEADS * MLA_NOPE), lambda i, j: (layer, 0, j)),
            pl.BlockSpec((None, MLA_KV_RANK, UP_HEADS * MLA_DV), lambda i, j: (layer, 0, j)),
            tab, tab, tab, tab,
        ],
        out_specs=[head_cols(MLA_QK_PAD), head_cols(MLA_QK_PAD), head_cols(MLA_V_PAD)],
        out_shape=[
            jax.ShapeDtypeStruct((t, MLA_HEADS * MLA_QK_PAD), BF16),
            jax.ShapeDtypeStruct((t, MLA_HEADS * MLA_QK_PAD), BF16),
            jax.ShapeDtypeStruct((t, MLA_HEADS * MLA_V_PAD), BF16),
        ],
        compiler_params=_params(("parallel", "arbitrary")),
        name="latent_up",
    )(p, p, w_uq_p, w_k, w_v, kr, cm, sm1, sm2)


RET_BLOCK = 256


RET_GROUP = 2


def _retention_kernel(q_ref, k_ref, v_ref, gate_ref, lg_ref, o_ref, state, dmat, xi, zeta):
    blk = dmat.shape[1]
    nblk = q_ref.shape[0] // blk

    state[...] = jnp.zeros_like(state)
    for hh in range(RET_GROUP):
        lg = lg_ref[hh]
        r = lax.broadcasted_iota(jnp.int32, (blk, blk), 0)
        c = lax.broadcasted_iota(jnp.int32, (blk, blk), 1)
        decay = jnp.exp(jnp.abs(r - c).astype(F32) * lg)
        dmat[hh] = jnp.where((c // CHUNK) <= (r // CHUNK), decay, 0.0)
        rx = lax.broadcasted_iota(jnp.int32, xi.shape[1:], 0).astype(F32)
        xi[hh] = jnp.exp((rx + 1.0) * lg[:, :RET_DV])
        rz = lax.broadcasted_iota(jnp.int32, zeta.shape[1:], 0).astype(F32)
        zeta[hh] = jnp.exp((blk - 1.0 - rz) * lg[:, :RET_DK])

    def body(j, carry):
        rows = pl.ds(pl.multiple_of(j * blk, blk), blk)
        for hh in range(RET_GROUP):
            qk_cols = slice(hh * RET_DK, (hh + 1) * RET_DK)
            v_cols = slice(hh * RET_DV, (hh + 1) * RET_DV)
            q = q_ref[rows, qk_cols]
            k = k_ref[rows, qk_cols]
            v = v_ref[rows, v_cols]
            s = lax.dot_general(q, k, (((1,), (1,)), ((), ())), preferred_element_type=F32)
            o = jnp.dot((s * dmat[hh]).astype(BF16), v, preferred_element_type=F32)
            st = state[hh]
            o = o + xi[hh] * jnp.dot(q, st.astype(BF16), preferred_element_type=F32)
            kz_t = (k.astype(F32) * zeta[hh]).T.astype(BF16)
            block_decay = jnp.exp(blk * lg_ref[hh][:, :RET_DV])
            state[hh] = st * block_decay + jnp.dot(kz_t, v, preferred_element_type=F32)

            mu = jnp.mean(o, axis=-1, keepdims=True)
            dlt = o - mu
            var = jnp.mean(dlt * dlt, axis=-1, keepdims=True)
            y = dlt * lax.rsqrt(var + EPS)
            o_ref[rows, v_cols] = (y * gate_ref[rows, v_cols].astype(F32)).astype(o_ref.dtype)
        return carry

    lax.fori_loop(0, nblk, body, 0, unroll=4)


def _retention(p, bsz, seq):
    t = p.shape[0]
    blk = RET_BLOCK
    g = RET_GROUP
    log_gamma = jnp.log(1.0 - 2.0 ** (-5.0 - jnp.arange(RET_HEADS, dtype=F32)))
    lg = jnp.broadcast_to(log_gamma[:, None, None], (RET_HEADS, 1, blk))
    return pl.pallas_call(
        _retention_kernel,
        grid=(bsz, RET_HEADS // g),
        in_specs=[
            pl.BlockSpec((seq, g * RET_DK), lambda b, h: (b, C_RQ // (g * RET_DK) + h)),
            pl.BlockSpec((seq, g * RET_DK), lambda b, h: (b, C_RK // (g * RET_DK) + h)),
            pl.BlockSpec((seq, g * RET_DV), lambda b, h: (b, C_RV // (g * RET_DV) + h)),
            pl.BlockSpec((seq, g * RET_DV), lambda b, h: (b, C_RG // (g * RET_DV) + h)),
            pl.BlockSpec((g, 1, blk), lambda b, h: (h, 0, 0)),
        ],
        out_specs=pl.BlockSpec((seq, g * RET_DV), lambda b, h: (b, h)),
        out_shape=jax.ShapeDtypeStruct((t, RET_V_W), BF16),
        scratch_shapes=[
            pltpu.VMEM((g, RET_DK, RET_DV), F32),
            pltpu.VMEM((g, blk, blk), F32),
            pltpu.VMEM((g, blk, RET_DV), F32),
            pltpu.VMEM((g, blk, RET_DK), F32),
        ],
        compiler_params=_params(("parallel", "parallel")),
        name="retention",
    )(p, p, p, p, lg)


ATT_BLOCK = 512
ATT_HEADS = 2


def _attention_kernel(q_ref, k_ref, v_ref, gate_ref, bias_ref, o_ref, s_scr, m_scr, acc_scr):
    seq = q_ref.shape[0]
    tq = s_scr.shape[1]
    nq = seq // tq

    m_scr[...] = jnp.full_like(m_scr, -jnp.inf)
    acc_scr[...] = jnp.zeros_like(acc_scr)

    def rows(blk):
        return pl.ds(pl.multiple_of(blk * tq, tq), tq)

    def slot(next_pair, pair, diagonal=False):
        masked = diagonal
        for hh in range(ATT_HEADS):
            qk_cols = slice(hh * MLA_QK_PAD, (hh + 1) * MLA_QK_PAD)
            v_cols = slice(hh * MLA_V_PAD, (hh + 1) * MLA_V_PAD)
            m_cols = slice(hh * LANES, (hh + 1) * LANES)
            if pair is not None:
                restart = jnp.where(pair[1] == 0, -jnp.inf, 0.0)
                s = s_scr[hh]
                if masked:
                    s = s + bias_ref[...]
                m_old = m_scr[:, m_cols] + restart
                m_new = jnp.maximum(m_old, jnp.max(s, axis=-1, keepdims=True))
                alpha = jnp.exp2(m_old - m_new)
                p = jnp.exp2(s - jnp.concatenate([m_new] * (tq // LANES), axis=1))
                m_scr[:, m_cols] = m_new
                v = v_ref[rows(pair[1]), v_cols]
                pv = jnp.dot(p.astype(BF16), v, preferred_element_type=F32)
                rescale = jnp.concatenate([alpha] * (MLA_V_PAD // LANES), axis=1)
                acc_scr[:, v_cols] = acc_scr[:, v_cols] * rescale + pv
            if next_pair is not None:
                q = q_ref[rows(next_pair[0]), qk_cols]
                k = k_ref[rows(next_pair[1]), qk_cols]
                s_scr[hh] = lax.dot_general(q, k, (((1,), (1,)), ((), ())), preferred_element_type=F32)

        if diagonal:
            qi = pair[0]
            for hh in range(ATT_HEADS):
                tot = acc_scr[:, hh * MLA_V_PAD:(hh + 1) * MLA_V_PAD]
                o = tot[:, :MLA_DV] / tot[:, MLA_DV:]
                cols = slice(hh * MLA_DV, (hh + 1) * MLA_DV)
                g = gate_ref[rows(qi), cols].astype(F32)
                o_ref[rows(qi), cols] = (o * g).astype(o_ref.dtype)

    zero = jnp.int32(0)
    slot((zero, zero), None)

    def query_block(qi, carry):
        def two_pairs(i, c):
            kb = 2 * i
            slot((qi, kb + 1), (qi, kb))
            slot((qi, kb + 2), (qi, kb + 1))
            return c

        lax.fori_loop(0, qi // 2, two_pairs, 0)

        @pl.when(qi % 2 == 1)
        def _():
            slot((qi, qi), (qi, qi - 1))

        @pl.when(qi < nq - 1)
        def _():
            slot((qi + 1, zero), (qi, qi), diagonal=True)

        @pl.when(qi == nq - 1)
        def _():
            slot(None, (qi, qi), diagonal=True)

        return carry

    lax.fori_loop(0, nq, query_block, 0)


def _attention(qc, kc, v, p, bsz, seq):
    t = qc.shape[0]
    tq = min(seq, ATT_BLOCK)
    nh = ATT_HEADS
    r = lax.broadcasted_iota(jnp.int32, (tq, tq), 0) // CHUNK
    c = lax.broadcasted_iota(jnp.int32, (tq, tq), 1) // CHUNK
    bias = jnp.where(c <= r, 0.0, NEG_INF * LOG2E).astype(F32)
    return pl.pallas_call(
        _attention_kernel,
        grid=(bsz, MLA_HEADS // nh),
        in_specs=[
            pl.BlockSpec((seq, nh * MLA_QK_PAD), lambda b, h: (b, h)),
            pl.BlockSpec((seq, nh * MLA_QK_PAD), lambda b, h: (b, h)),
            pl.BlockSpec((seq, nh * MLA_V_PAD), lambda b, h: (b, h)),
            pl.BlockSpec((seq, nh * MLA_DV), lambda b, h: (b, C_MG // (nh * MLA_DV) + h)),
            pl.BlockSpec((tq, tq), lambda b, h: (0, 0)),
        ],
        out_specs=pl.BlockSpec((seq, nh * MLA_DV), lambda b, h: (b, h)),
        out_shape=jax.ShapeDtypeStruct((t, MLA_V_W), BF16),
        scratch_shapes=[
            pltpu.VMEM((nh, tq, tq), F32),
            pltpu.VMEM((tq, nh * LANES), F32),
            pltpu.VMEM((tq, nh * MLA_V_PAD), F32),
        ],
        compiler_params=_params(("parallel", "parallel")),
        name="attention",
    )(qc, kc, v, p, bias)


PROJ_CHUNK = 256


def _merge_kernel(a_ref, b_ref, wr_ref, wm_ref, ga_ref, gb_ref, o_ref):
    for c0 in range(0, o_ref.shape[1], PROJ_CHUNK):
        cols = slice(c0, c0 + PROJ_CHUNK)
        y_ret = jnp.dot(a_ref[...], wr_ref[:, cols], preferred_element_type=F32)
        y_mla = jnp.dot(b_ref[...], wm_ref[:, cols], preferred_element_type=F32)
        merged = ga_ref[:, cols].astype(F32) * y_ret + gb_ref[:, cols].astype(F32) * y_mla
        o_ref[:, cols] = merged.astype(o_ref.dtype)


def _merge(a, bm, p, w_ret, w_mla, layer):
    t, d_in = a.shape
    d = w_ret.shape[2]
    tm = min(t, 1024)
    tn = 512
    return pl.pallas_call(
        _merge_kernel,
        grid=(t // tm, d // tn),
        in_specs=[
            pl.BlockSpec((tm, d_in), lambda i, j: (i, 0)),
            pl.BlockSpec((tm, d_in), lambda i, j: (i, 0)),
            pl.BlockSpec((None, d_in, tn), lambda i, j: (layer, 0, j)),
            pl.BlockSpec((None, d_in, tn), lambda i, j: (layer, 0, j)),
            pl.BlockSpec((tm, tn), lambda i, j: (i, C_BG // tn + j)),
            pl.BlockSpec((tm, tn), lambda i, j: (i, (C_BG + d) // tn + j)),
        ],
        out_specs=pl.BlockSpec((tm, tn), lambda i, j: (i, j)),
        out_shape=jax.ShapeDtypeStruct((t, d), BF16),
        compiler_params=_params(("parallel", "arbitrary")),
        name="merge_proj",
    )(a, bm, w_ret, w_mla, p, p)


def _out_kernel(m_ref, w_ref, x_ref, gate_ref, o_ref):
    for c0 in range(0, o_ref.shape[1], PROJ_CHUNK):
        cols = slice(c0, c0 + PROJ_CHUNK)
        out = jnp.dot(m_ref[...], w_ref[:, cols], preferred_element_type=F32)
        o_ref[:, cols] = x_ref[:, cols] + gate_ref[:, cols] * out


def _out_norm_kernel(m_ref, w_ref, x_ref, gate_ref, g_ref, o_ref):
    _out_kernel(m_ref, w_ref, x_ref, gate_ref, o_ref)
    y = o_ref[...]
    inv = lax.rsqrt(jnp.mean(y * y, axis=-1, keepdims=True) + EPS)
    o_ref[...] = (y * inv) * g_ref[...]


def _out_proj(merged, w_out, x2, mod4, layer, seq, g_final=None):
    t, d = x2.shape
    fuse_norm = g_final is not None
    tm = min(seq, 512 if fuse_norm else 1024)
    per_b = seq // tm
    tn = d if fuse_norm else 1024
    gate_blk = 2 * d // tn
    in_specs = [
        pl.BlockSpec((tm, d), lambda i, j: (i, 0)),
        pl.BlockSpec((None, d, tn), lambda i, j: (layer, 0, j)),
        pl.BlockSpec((tm, tn), lambda i, j: (i, j)),
        pl.BlockSpec((None, None, 1, tn), lambda i, j: (layer, i // per_b, 0, gate_blk + j)),
    ]
    args = [merged, w_out, x2, mod4]
    if fuse_norm:
        in_specs.append(pl.BlockSpec((1, d), lambda i, j: (0, 0)))
        args.append(g_final.reshape(1, d))
    return pl.pallas_call(
        _out_norm_kernel if fuse_norm else _out_kernel,
        grid=(t // tm, d // tn),
        in_specs=in_specs,
        out_specs=pl.BlockSpec((tm, tn), lambda i, j: (i, j)),
        out_shape=jax.ShapeDtypeStruct((t, d), F32),
        compiler_params=_params(("parallel", "arbitrary")),
        name="out_norm" if fuse_norm else "out_proj",
    )(*args)


def kernel(x, c, positions, w_mod, b_mod, g_norm, w_in, g_cq, g_ckv, w_uq, w_ukv,
           w_ret_proj, w_mla_proj, w_out, g_final):
    bsz, seq, d = x.shape
    depth = w_in.shape[0]
    t = bsz * seq
    assert seq % RET_BLOCK == 0 and seq % min(seq, ATT_BLOCK) == 0 and d % IN_TN == 0

    kr_lo = C_CKV + MLA_KV_RANK
    w_in_b = w_in.astype(BF16)
    w_lo = w_in_b
    w_hi = w_in_b[:, :, kr_lo + MLA_ROPE:]
    w_kr = jnp.pad(w_in[:, :, kr_lo:kr_lo + MLA_ROPE], ((0, 0), (0, 0), (0, LANES - MLA_ROPE))).astype(BF16)
    w_uq_p = jnp.pad(
        w_uq.reshape(depth, MLA_Q_RANK, MLA_HEADS, MLA_NOPE + MLA_ROPE),
        ((0, 0), (0, 0), (0, 0), (0, MLA_QK_PAD - MLA_NOPE - MLA_ROPE)),
    ).reshape(depth, MLA_Q_RANK, MLA_HEADS * MLA_QK_PAD).astype(BF16)
    w_ukv4 = w_ukv.reshape(depth, MLA_KV_RANK, MLA_HEADS, MLA_NOPE + MLA_DV)
    w_k = w_ukv4[..., :MLA_NOPE].reshape(depth, MLA_KV_RANK, MLA_HEADS * MLA_NOPE).astype(BF16)
    w_v = w_ukv4[..., MLA_NOPE:].reshape(depth, MLA_KV_RANK, MLA_V_W).astype(BF16)
    w_ret = w_ret_proj.astype(BF16)
    w_mla = w_mla_proj.astype(BF16)
    w_o = w_out.astype(BF16)

    tabs = _rope_tables(positions)
    mod = _modulation(c, w_mod, b_mod)
    mod4 = mod.reshape(depth, mod.shape[1], 1, 3 * d)
    g_norm3 = g_norm.reshape(depth, 1, d)
    g_cq3 = g_cq.reshape(depth, 1, MLA_Q_RANK)
    g_ckv3 = g_ckv.reshape(depth, 1, MLA_KV_RANK)

    x2 = x.reshape(t, d)
    for layer in range(depth):
        p, kr = _in_proj(x2, mod4, g_norm3, w_lo, w_hi, w_kr, tabs, g_cq3, g_ckv3, layer, seq)
        qc, kc, v = _latent_up(p, kr, w_uq_p, w_k, w_v, tabs, layer)
        a = _retention(p, bsz, seq)
        bm = _attention(qc, kc, v, p, bsz, seq)
        merged = _merge(a, bm, p, w_ret, w_mla, layer)
        last = layer == depth - 1
        x2 = _out_proj(merged, w_o, x2, mod4, layer, seq, g_final if last else None)
    return x2.reshape(bsz, seq, d)
```

```python
import jax
import jax.numpy as jnp
from jax import lax
from jax.experimental import pallas as pl
from jax.experimental.pallas import tpu as pltpu

F32 = jnp.float32
BF16 = jnp.bfloat16

CHUNK = 64
EPS = 1e-6
NEG_INF = -1e30
ROPE_BASE = 10000.0
RET_HEADS = 8
RET_DK = 128
RET_DV = 256
RET_QK_W = RET_HEADS * RET_DK
RET_V_W = RET_HEADS * RET_DV
MLA_HEADS = 16
MLA_Q_RANK = 512
MLA_KV_RANK = 512
MLA_NOPE = 128
MLA_ROPE = 64
MLA_DV = 128
MLA_V_W = MLA_HEADS * MLA_DV
MLA_QK_PAD = 256
MLA_V_PAD = 2 * MLA_DV
LOG2E = 1.4426950408889634

LANES = 128
VMEM_LIMIT = 56 * 1024 * 1024

C_RQ = 0
C_RK = C_RQ + RET_QK_W
C_RV = C_RK + RET_QK_W
C_RG = C_RV + RET_V_W
C_CQ = C_RG + RET_V_W
C_CKV = C_CQ + MLA_Q_RANK
C_MG = C_CKV + MLA_KV_RANK
C_BG = C_MG + MLA_V_W
IN_TN = 1024
IN_CHUNK = 256


def _params(sem):
    return pltpu.CompilerParams(dimension_semantics=sem, vmem_limit_bytes=VMEM_LIMIT)


def _sigmoid(x):
    return 0.5 * jnp.tanh(0.5 * x) + 0.5


def _rope_tables_kernel(pos_ref, invr_ref, invm_ref, cr_ref, sr_ref, cm_ref, sm1_ref, sm2_ref):
    pos = pos_ref[...].astype(F32)
    lane = lax.broadcasted_iota(jnp.int32, cr_ref.shape, 1)
    ang_r = pos * invr_ref[...]
    cr_ref[...] = jnp.cos(ang_r)
    sin_r = jnp.sin(ang_r)
    sr_ref[...] = jnp.where(lane < RET_DK // 2, -sin_r, sin_r)
    ang_m = pos * invm_ref[...]
    cos_m = jnp.cos(ang_m)
    sin_m = jnp.sin(ang_m)
    half = MLA_ROPE // 2
    cm_ref[...] = jnp.where(lane < MLA_ROPE, cos_m, 0.0)
    sm1_ref[...] = jnp.where(lane < half, -sin_m, 0.0)
    sm2_ref[...] = jnp.where((lane >= half) & (lane < MLA_ROPE), sin_m, 0.0)


def _rope_tables(positions):
    t = positions.size
    tm = min(t, 1024)
    inv_r = 1.0 / (ROPE_BASE ** (jnp.arange(0, RET_DK, 2, dtype=F32) / RET_DK))
    inv_m = 1.0 / (ROPE_BASE ** (jnp.arange(0, MLA_ROPE, 2, dtype=F32) / MLA_ROPE))
    invr = jnp.concatenate([inv_r, inv_r])[None, :]
    invm = jnp.concatenate([inv_m, inv_m, jnp.zeros((LANES - MLA_ROPE,), F32)])[None, :]
    row = pl.BlockSpec((tm, LANES), lambda i: (i, 0))
    const = pl.BlockSpec((1, LANES), lambda i: (0, 0))
    return pl.pallas_call(
        _rope_tables_kernel,
        grid=(t // tm,),
        in_specs=[pl.BlockSpec((tm, 1), lambda i: (i, 0)), const, const],
        out_specs=[row] * 5,
        out_shape=[jax.ShapeDtypeStruct((t, LANES), F32)] * 5,
        compiler_params=_params(("parallel",)),
        name="rope_tables",
    )(positions.reshape(t, 1), invr, invm)


def _mod_kernel(c_ref, w_ref, b_ref, o_ref):
    c = c_ref[...]
    ca = (c * _sigmoid(c)).astype(BF16)
    o_ref[...] = jnp.dot(ca, w_ref[...].astype(BF16), preferred_element_type=F32) + b_ref[...]


def _modulation(c, w_mod, b_mod):
    depth, d, n = w_mod.shape
    bsz = c.shape[0]
    rows = 8
    cp = jnp.pad(c, ((0, rows - bsz), (0, 0)))
    tn = 512
    return pl.pallas_call(
        _mod_kernel,
        grid=(depth, n // tn),
        in_specs=[
            pl.BlockSpec((rows, d), lambda l, j: (0, 0)),
            pl.BlockSpec((None, d, tn), lambda l, j: (l, 0, j)),
            pl.BlockSpec((None, 1, tn), lambda l, j: (l, 0, j)),
        ],
        out_specs=pl.BlockSpec((None, rows, tn), lambda l, j: (l, 0, j)),
        out_shape=jax.ShapeDtypeStruct((depth, rows, n), F32),
        compiler_params=_params(("parallel", "parallel")),
        name="modulation",
    )(cp, w_mod, b_mod.reshape(depth, 1, n))


def _rope_ret(a, cos, sin):
    outs = []
    for hh in range(a.shape[1] // RET_DK):
        t = a[:, hh * RET_DK:(hh + 1) * RET_DK]
        outs.append(t * cos + pltpu.roll(t, RET_DK // 2, 1) * sin)
    return jnp.concatenate(outs, axis=1) if len(outs) > 1 else outs[0]


def _rope_mla(t, cos, sin_lo, sin_hi):
    half = MLA_ROPE // 2
    return t * cos + pltpu.roll(t, LANES - half, 1) * sin_lo + pltpu.roll(t, half, 1) * sin_hi


def _in_proj_kernel(x_ref, shift_ref, scale_ref, g_ref, wlo_ref, whi_ref, wkr_ref, cr_ref, sr_ref,
                    cm_ref, sm1_ref, sm2_ref, gcq_ref, gckv_ref, p_ref, kr_ref, h_scr):
    j = pl.program_id(1)

    @pl.when(j == 0)
    def _():
        x = x_ref[...]
        inv = lax.rsqrt(jnp.mean(x * x, axis=-1, keepdims=True) + EPS)
        h = (x * inv) * g_ref[...]
        h = h * (1.0 + scale_ref[...]) + shift_ref[...]
        hb = h.astype(BF16)
        h_scr[...] = hb
        kr = jnp.dot(hb, wkr_ref[...], preferred_element_type=F32)
        kr_ref[...] = _rope_mla(kr, cm_ref[...], sm1_ref[...], sm2_ref[...]).astype(kr_ref.dtype)

    def tile_range(lo, width):
        return (j >= lo // IN_TN) & (j < (lo + width) // IN_TN)

    def project(w_ref, epilogues, chunk=IN_CHUNK):
        for idx, c0 in enumerate(range(0, IN_TN, chunk)):
            acc = jnp.dot(h_scr[...], w_ref[:, c0:c0 + chunk], preferred_element_type=F32)
            p_ref[:, c0:c0 + chunk] = epilogues[idx % len(epilogues)](acc).astype(p_ref.dtype)

    def rope(acc):
        return _rope_ret(acc, cr_ref[...], sr_ref[...])

    def latent_norm(g_latent_ref):
        def apply(acc):
            inv = lax.rsqrt(jnp.mean(acc * acc, axis=-1, keepdims=True) + EPS)
            return (acc * inv) * g_latent_ref[...]
        return apply

    @pl.when(tile_range(C_RQ, RET_QK_W))
    def _():
        project(wlo_ref, [rope])

    @pl.when(tile_range(C_RK, RET_QK_W))
    def _():
        project(wlo_ref, [lambda acc: rope(acc) * (RET_DK ** -0.5)])

    @pl.when(tile_range(C_RV, RET_V_W))
    def _():
        project(wlo_ref, [lambda acc: acc])

    def silu(acc):
        return acc * _sigmoid(acc)

    @pl.when(tile_range(C_RG, RET_V_W))
    def _():
        project(wlo_ref, [silu])

    @pl.when(tile_range(C_MG, MLA_V_W))
    def _():
        project(whi_ref, [silu])

    @pl.when(tile_range(C_CQ, MLA_Q_RANK + MLA_KV_RANK))
    def _():
        project(wlo_ref, [latent_norm(gcq_ref), latent_norm(gckv_ref)], chunk=MLA_Q_RANK)

    @pl.when(j >= C_BG // IN_TN)
    def _():
        project(whi_ref, [_sigmoid])


def _in_proj(x2, mod4, g_norm3, w_lo, w_hi, w_kr, tabs, g_cq3, g_ckv3, layer, seq):
    t, d = x2.shape
    lo_tiles = C_MG // IN_TN
    n = C_MG + w_hi.shape[2]
    tm = min(seq, 1024)
    per_b = seq // tm
    cr, sr, cm, sm1, sm2 = tabs
    tab = pl.BlockSpec((tm, LANES), lambda i, j: (i, 0))
    return pl.pallas_call(
        _in_proj_kernel,
        grid=(t // tm, n // IN_TN),
        in_specs=[
            pl.BlockSpec((tm, d), lambda i, j: (i, 0)),
            pl.BlockSpec((None, None, 1, d), lambda i, j: (layer, i // per_b, 0, 0)),
            pl.BlockSpec((None, None, 1, d), lambda i, j: (layer, i // per_b, 0, 1)),
            pl.BlockSpec((None, 1, d), lambda i, j: (layer, 0, 0)),
            pl.BlockSpec((None, d, IN_TN), lambda i, j: (layer, 0, jnp.minimum(j, lo_tiles - 1))),
            pl.BlockSpec((None, d, IN_TN), lambda i, j: (layer, 0, jnp.maximum(j - lo_tiles, 0))),
            pl.BlockSpec((None, d, LANES), lambda i, j: (layer, 0, 0)),
            tab, tab, tab, tab, tab,
            pl.BlockSpec((None, 1, MLA_Q_RANK), lambda i, j: (layer, 0, 0)),
            pl.BlockSpec((None, 1, MLA_KV_RANK), lambda i, j: (layer, 0, 0)),
        ],
        out_specs=[
            pl.BlockSpec((tm, IN_TN), lambda i, j: (i, j)),
            pl.BlockSpec((tm, LANES), lambda i, j: (i, 0)),
        ],
        out_shape=[
            jax.ShapeDtypeStruct((t, n), BF16),
            jax.ShapeDtypeStruct((t, LANES), BF16),
        ],
        scratch_shapes=[pltpu.VMEM((tm, d), BF16)],
        compiler_params=_params(("parallel", "arbitrary")),
        name="in_proj",
    )(x2, mod4, mod4, g_norm3, w_lo, w_hi, w_kr, cr, sr, cm, sm1, sm2, g_cq3, g_ckv3)


UP_HEADS = 4


def _latent_up_kernel(cq_ref, ckv_ref, wq_ref, wk_ref, wv_ref, kr_ref, cm_ref, sm1_ref, sm2_ref,
                      q_ref, k_ref, v_ref):
    scale = (MLA_NOPE + MLA_ROPE) ** -0.5 * LOG2E
    cq = cq_ref[...]
    ckv = ckv_ref[...]
    cm, s1, s2 = cm_ref[...], sm1_ref[...], sm2_ref[...]
    kr = kr_ref[...]
    for hh in range(UP_HEADS):
        lo = hh * MLA_QK_PAD
        acc = jnp.dot(cq, wq_ref[:, lo:lo + MLA_QK_PAD], preferred_element_type=F32) * scale
        q_ref[:, lo:lo + MLA_NOPE] = acc[:, :MLA_NOPE].astype(q_ref.dtype)
        q_ref[:, lo + MLA_NOPE:lo + MLA_QK_PAD] = _rope_mla(acc[:, MLA_NOPE:], cm, s1, s2).astype(q_ref.dtype)
    kn = jnp.dot(ckv, wk_ref[...], preferred_element_type=F32)
    vv = jnp.dot(ckv, wv_ref[...], preferred_element_type=F32)
    ones = jnp.ones((cq.shape[0], MLA_V_PAD - MLA_DV), v_ref.dtype)
    for hh in range(UP_HEADS):
        lo = hh * MLA_QK_PAD
        k_ref[:, lo:lo + MLA_NOPE] = kn[:, hh * MLA_NOPE:(hh + 1) * MLA_NOPE].astype(k_ref.dtype)
        k_ref[:, lo + MLA_NOPE:lo + MLA_QK_PAD] = kr
        vlo = hh * MLA_V_PAD
        v_ref[:, vlo:vlo + MLA_DV] = vv[:, hh * MLA_DV:(hh + 1) * MLA_DV].astype(v_ref.dtype)
        v_ref[:, vlo + MLA_DV:vlo + MLA_V_PAD] = ones


def _latent_up(p, kr, w_uq_p, w_k, w_v, tabs, layer):
    t = p.shape[0]
    tm = min(t, 1024)
    _, _, cm, sm1, sm2 = tabs
    tab = pl.BlockSpec((tm, LANES), lambda i, j: (i, 0))

    def head_cols(width):
        return pl.BlockSpec((tm, UP_HEADS * width), lambda i, j: (i, j))

    return pl.pallas_call(
        _latent_up_kernel,
        grid=(t // tm, MLA_HEADS // UP_HEADS),
        in_specs=[
            pl.BlockSpec((tm, MLA_Q_RANK), lambda i, j: (i, C_CQ // MLA_Q_RANK)),
            pl.BlockSpec((tm, MLA_KV_RANK), lambda i, j: (i, C_CKV // MLA_KV_RANK)),
            pl.BlockSpec((None, MLA_Q_RANK, UP_HEADS * MLA_QK_PAD), lambda i, j: (layer, 0, j)),
            pl.BlockSpec((None, MLA_KV_RANK, UP_HEADS * MLA_NOPE), lambda i, j: (layer, 0, j)),
            pl.BlockSpec((None, MLA_KV_RANK, UP_HEADS * MLA_DV), lambda i, j: (layer, 0, j)),
            tab, tab, tab, tab,
        ],
        out_specs=[head_cols(MLA_QK_PAD), head_cols(MLA_QK_PAD), head_cols(MLA_V_PAD)],
        out_shape=[
            jax.ShapeDtypeStruct((t, MLA_HEADS * MLA_QK_PAD), BF16),
            jax.ShapeDtypeStruct((t, MLA_HEADS * MLA_QK_PAD), BF16),
            jax.ShapeDtypeStruct((t, MLA_HEADS * MLA_V_PAD), BF16),
        ],
        compiler_params=_params(("parallel", "arbitrary")),
        name="latent_up",
    )(p, p, w_uq_p, w_k, w_v, kr, cm, sm1, sm2)


RET_BLOCK = 256


RET_GROUP = 2


def _retention_kernel(q_ref, k_ref, v_ref, gate_ref, lg_ref, o_ref, state, dmat, xi, zeta):
    blk = dmat.shape[1]
    nblk = q_ref.shape[0] // blk

    state[...] = jnp.zeros_like(state)
    for hh in range(RET_GROUP):
        lg = lg_ref[hh]
        r = lax.broadcasted_iota(jnp.int32, (blk, blk), 0)
        c = lax.broadcasted_iota(jnp.int32, (blk, blk), 1)
        decay = jnp.exp(jnp.abs(r - c).astype(F32) * lg)
        dmat[hh] = jnp.where((c // CHUNK) <= (r // CHUNK), decay, 0.0)
        rx = lax.broadcasted_iota(jnp.int32, xi.shape[1:], 0).astype(F32)
        xi[hh] = jnp.exp((rx + 1.0) * lg[:, :RET_DV])
        rz = lax.broadcasted_iota(jnp.int32, zeta.shape[1:], 0).astype(F32)
        zeta[hh] = jnp.exp((blk - 1.0 - rz) * lg[:, :RET_DK])

    def body(j, carry):
        rows = pl.ds(pl.multiple_of(j * blk, blk), blk)
        for hh in range(RET_GROUP):
            qk_cols = slice(hh * RET_DK, (hh + 1) * RET_DK)
            v_cols = slice(hh * RET_DV, (hh + 1) * RET_DV)
            q = q_ref[rows, qk_cols]
            k = k_ref[rows, qk_cols]
            v = v_ref[rows, v_cols]
            s = lax.dot_general(q, k, (((1,), (1,)), ((), ())), preferred_element_type=F32)
            o = jnp.dot((s * dmat[hh]).astype(BF16), v, preferred_element_type=F32)
            st = state[hh]
            o = o + xi[hh] * jnp.dot(q, st.astype(BF16), preferred_element_type=F32)
            kz_t = (k.astype(F32) * zeta[hh]).T.astype(BF16)
            block_decay = jnp.exp(blk * lg_ref[hh][:, :RET_DV])
            state[hh] = st * block_decay + jnp.dot(kz_t, v, preferred_element_type=F32)

            mu = jnp.mean(o, axis=-1, keepdims=True)
            dlt = o - mu
            var = jnp.mean(dlt * dlt, axis=-1, keepdims=True)
            y = dlt * lax.rsqrt(var + EPS)
            o_ref[rows, v_cols] = (y * gate_ref[rows, v_cols].astype(F32)).astype(o_ref.dtype)
        return carry

    lax.fori_loop(0, nblk, body, 0, unroll=4)


def _retention(p, bsz, seq):
    t = p.shape[0]
    blk = RET_BLOCK
    g = RET_GROUP
    log_gamma = jnp.log(1.0 - 2.0 ** (-5.0 - jnp.arange(RET_HEADS, dtype=F32)))
    lg = jnp.broadcast_to(log_gamma[:, None, None], (RET_HEADS, 1, blk))
    return pl.pallas_call(
        _retention_kernel,
        grid=(bsz, RET_HEADS // g),
        in_specs=[
            pl.BlockSpec((seq, g * RET_DK), lambda b, h: (b, C_RQ // (g * RET_DK) + h)),
            pl.BlockSpec((seq, g * RET_DK), lambda b, h: (b, C_RK // (g * RET_DK) + h)),
            pl.BlockSpec((seq, g * RET_DV), lambda b, h: (b, C_RV // (g * RET_DV) + h)),
            pl.BlockSpec((seq, g * RET_DV), lambda b, h: (b, C_RG // (g * RET_DV) + h)),
            pl.BlockSpec((g, 1, blk), lambda b, h: (h, 0, 0)),
        ],
        out_specs=pl.BlockSpec((seq, g * RET_DV), lambda b, h: (b, h)),
        out_shape=jax.ShapeDtypeStruct((t, RET_V_W), BF16),
        scratch_shapes=[
            pltpu.VMEM((g, RET_DK, RET_DV), F32),
            pltpu.VMEM((g, blk, blk), F32),
            pltpu.VMEM((g, blk, RET_DV), F32),
            pltpu.VMEM((g, blk, RET_DK), F32),
        ],
        compiler_params=_params(("parallel", "parallel")),
        name="retention",
    )(p, p, p, p, lg)


ATT_BLOCK = 512
ATT_HEADS = 2


def _attention_kernel(q_ref, k_ref, v_ref, gate_ref, bias_ref, o_ref, s_scr, m_scr, acc_scr):
    seq = q_ref.shape[0]
    tq = s_scr.shape[1]
    nq = seq // tq

    def rows(blk):
        return slice(blk * tq, (blk + 1) * tq)

    def slot(next_pair, pair):
        for hh in range(ATT_HEADS):
            qk_cols = slice(hh * MLA_QK_PAD, (hh + 1) * MLA_QK_PAD)
            v_cols = slice(hh * MLA_V_PAD, (hh + 1) * MLA_V_PAD)
            m_cols = slice(hh * LANES, (hh + 1) * LANES)
            if pair is not None:
                qi, kb = pair
                s = s_scr[hh]
                if kb == qi:
                    s = s + bias_ref[...]
                m_cur = jnp.max(s, axis=-1, keepdims=True)
                if kb == 0:
                    m_new = jnp.broadcast_to(m_cur, (tq, LANES))
                else:
                    m_old = m_scr[:, m_cols]
                    m_new = jnp.maximum(m_old, m_cur)
                p = jnp.exp2(s - jnp.concatenate([m_new] * (tq // LANES), axis=1))
                v = v_ref[rows(kb), v_cols]
                pv = jnp.dot(p.astype(BF16), v, preferred_element_type=F32)
                if kb == 0:
                    acc = pv
                else:
                    alpha = jnp.exp2(m_old - m_new)
                    acc = acc_scr[:, v_cols] * jnp.concatenate([alpha] * (MLA_V_PAD // LANES), axis=1) + pv
                if kb == qi:
                    cols = slice(hh * MLA_DV, (hh + 1) * MLA_DV)
                    o = acc[:, :MLA_DV] / acc[:, MLA_DV:]
                    o_ref[rows(qi), cols] = (o * gate_ref[rows(qi), cols].astype(F32)).astype(o_ref.dtype)
                else:
                    acc_scr[:, v_cols] = acc
                    m_scr[:, m_cols] = m_new
            if next_pair is not None:
                q = q_ref[rows(next_pair[0]), qk_cols]
                k = k_ref[rows(next_pair[1]), qk_cols]
                s_scr[hh] = lax.dot_general(q, k, (((1,), (1,)), ((), ())), preferred_element_type=F32)

    slot((0, 0), None)
    for qi in range(nq):
        @pl.when(pl.program_id(0) + qi >= 0)
        def _(qi=qi):
            for kb in range(qi + 1):
                if kb < qi:
                    nxt = (qi, kb + 1)
                else:
                    nxt = (qi + 1, 0) if qi + 1 < nq else None
                slot(nxt, (qi, kb))


def _attention(qc, kc, v, p, bsz, seq):
    t = qc.shape[0]
    tq = min(seq, ATT_BLOCK)
    nh = ATT_HEADS
    r = lax.broadcasted_iota(jnp.int32, (tq, tq), 0) // CHUNK
    c = lax.broadcasted_iota(jnp.int32, (tq, tq), 1) // CHUNK
    bias = jnp.where(c <= r, 0.0, NEG_INF * LOG2E).astype(F32)
    return pl.pallas_call(
        _attention_kernel,
        grid=(bsz, MLA_HEADS // nh),
        in_specs=[
            pl.BlockSpec((seq, nh * MLA_QK_PAD), lambda b, h: (b, h)),
            pl.BlockSpec((seq, nh * MLA_QK_PAD), lambda b, h: (b, h)),
            pl.BlockSpec((seq, nh * MLA_V_PAD), lambda b, h: (b, h)),
            pl.BlockSpec((seq, nh * MLA_DV), lambda b, h: (b, C_MG // (nh * MLA_DV) + h)),
            pl.BlockSpec((tq, tq), lambda b, h: (0, 0)),
        ],
        out_specs=pl.BlockSpec((seq, nh * MLA_DV), lambda b, h: (b, h)),
        out_shape=jax.ShapeDtypeStruct((t, MLA_V_W), BF16),
        scratch_shapes=[
            pltpu.VMEM((nh, tq, tq), F32),
            pltpu.VMEM((tq, nh * LANES), F32),
            pltpu.VMEM((tq, nh * MLA_V_PAD), F32),
        ],
        compiler_params=_params(("parallel", "parallel")),
        name="attention",
    )(qc, kc, v, p, bias)


PROJ_CHUNK = 256


def _merge_kernel(a_ref, b_ref, wr_ref, wm_ref, ga_ref, gb_ref, o_ref):
    for c0 in range(0, o_ref.shape[1], PROJ_CHUNK):
        cols = slice(c0, c0 + PROJ_CHUNK)
        y_ret = jnp.dot(a_ref[...], wr_ref[:, cols], preferred_element_type=F32)
        y_mla = jnp.dot(b_ref[...], wm_ref[:, cols], preferred_element_type=F32)
        merged = ga_ref[:, cols].astype(F32) * y_ret + gb_ref[:, cols].astype(F32) * y_mla
        o_ref[:, cols] = merged.astype(o_ref.dtype)


def _merge(a, bm, p, w_ret, w_mla, layer):
    t, d_in = a.shape
    d = w_ret.shape[2]
    tm = min(t, 1024)
    tn = 512
    return pl.pallas_call(
        _merge_kernel,
        grid=(t // tm, d // tn),
        in_specs=[
            pl.BlockSpec((tm, d_in), lambda i, j: (i, 0)),
            pl.BlockSpec((tm, d_in), lambda i, j: (i, 0)),
            pl.BlockSpec((None, d_in, tn), lambda i, j: (layer, 0, j)),
            pl.BlockSpec((None, d_in, tn), lambda i, j: (layer, 0, j)),
            pl.BlockSpec((tm, tn), lambda i, j: (i, C_BG // tn + j)),
            pl.BlockSpec((tm, tn), lambda i, j: (i, (C_BG + d) // tn + j)),
        ],
        out_specs=pl.BlockSpec((tm, tn), lambda i, j: (i, j)),
        out_shape=jax.ShapeDtypeStruct((t, d), BF16),
        compiler_params=_params(("parallel", "arbitrary")),
        name="merge_proj",
    )(a, bm, w_ret, w_mla, p, p)


def _out_kernel(m_ref, w_ref, x_ref, gate_ref, o_ref):
    for c0 in range(0, o_ref.shape[1], PROJ_CHUNK):
        cols = slice(c0, c0 + PROJ_CHUNK)
        out = jnp.dot(m_ref[...], w_ref[:, cols], preferred_element_type=F32)
        o_ref[:, cols] = x_ref[:, cols] + gate_ref[:, cols] * out


def _out_norm_kernel(m_ref, w_ref, x_ref, gate_ref, g_ref, o_ref):
    _out_kernel(m_ref, w_ref, x_ref, gate_ref, o_ref)
    y = o_ref[...]
    inv = lax.rsqrt(jnp.mean(y * y, axis=-1, keepdims=True) + EPS)
    o_ref[...] = (y * inv) * g_ref[...]


def _out_proj(merged, w_out, x2, mod4, layer, seq, g_final=None):
    t, d = x2.shape
    fuse_norm = g_final is not None
    tm = min(seq, 512 if fuse_norm else 1024)
    per_b = seq // tm
    tn = d if fuse_norm else 1024
    gate_blk = 2 * d // tn
    in_specs = [
        pl.BlockSpec((tm, d), lambda i, j: (i, 0)),
        pl.BlockSpec((None, d, tn), lambda i, j: (layer, 0, j)),
        pl.BlockSpec((tm, tn), lambda i, j: (i, j)),
        pl.BlockSpec((None, None, 1, tn), lambda i, j: (layer, i // per_b, 0, gate_blk + j)),
    ]
    args = [merged, w_out, x2, mod4]
    if fuse_norm:
        in_specs.append(pl.BlockSpec((1, d), lambda i, j: (0, 0)))
        args.append(g_final.reshape(1, d))
    return pl.pallas_call(
        _out_norm_kernel if fuse_norm else _out_kernel,
        grid=(t // tm, d // tn),
        in_specs=in_specs,
        out_specs=pl.BlockSpec((tm, tn), lambda i, j: (i, j)),
        out_shape=jax.ShapeDtypeStruct((t, d), F32),
        compiler_params=_params(("parallel", "arbitrary")),
        name="out_norm" if fuse_norm else "out_proj",
    )(*args)


def kernel(x, c, positions, w_mod, b_mod, g_norm, w_in, g_cq, g_ckv, w_uq, w_ukv,
           w_ret_proj, w_mla_proj, w_out, g_final):
    bsz, seq, d = x.shape
    depth = w_in.shape[0]
    t = bsz * seq
    assert seq % RET_BLOCK == 0 and seq % min(seq, ATT_BLOCK) == 0 and d % IN_TN == 0

    kr_lo = C_CKV + MLA_KV_RANK
    w_in_b = w_in.astype(BF16)
    w_lo = w_in_b
    w_hi = w_in_b[:, :, kr_lo + MLA_ROPE:]
    w_kr = jnp.pad(w_in[:, :, kr_lo:kr_lo + MLA_ROPE], ((0, 0), (0, 0), (0, LANES - MLA_ROPE))).astype(BF16)
    w_uq_p = jnp.pad(
        w_uq.reshape(depth, MLA_Q_RANK, MLA_HEADS, MLA_NOPE + MLA_ROPE),
        ((0, 0), (0, 0), (0, 0), (0, MLA_QK_PAD - MLA_NOPE - MLA_ROPE)),
    ).reshape(depth, MLA_Q_RANK, MLA_HEADS * MLA_QK_PAD).astype(BF16)
    w_ukv4 = w_ukv.reshape(depth, MLA_KV_RANK, MLA_HEADS, MLA_NOPE + MLA_DV)
    w_k = w_ukv4[..., :MLA_NOPE].reshape(depth, MLA_KV_RANK, MLA_HEADS * MLA_NOPE).astype(BF16)
    w_v = w_ukv4[..., MLA_NOPE:].reshape(depth, MLA_KV_RANK, MLA_V_W).astype(BF16)
    w_ret = w_ret_proj.astype(BF16)
    w_mla = w_mla_proj.astype(BF16)
    w_o = w_out.astype(BF16)

    tabs = _rope_tables(positions)
    mod = _modulation(c, w_mod, b_mod)
    mod4 = mod.reshape(depth, mod.shape[1], 1, 3 * d)
    g_norm3 = g_norm.reshape(depth, 1, d)
    g_cq3 = g_cq.reshape(depth, 1, MLA_Q_RANK)
    g_ckv3 = g_ckv.reshape(depth, 1, MLA_KV_RANK)

    x2 = x.reshape(t, d)
    for layer in range(depth):
        p, kr = _in_proj(x2, mod4, g_norm3, w_lo, w_hi, w_kr, tabs, g_cq3, g_ckv3, layer, seq)
        qc, kc, v = _latent_up(p, kr, w_uq_p, w_k, w_v, tabs, layer)
        a = _retention(p, bsz, seq)
        bm = _attention(qc, kc, v, p, bsz, seq)
        merged = _merge(a, bm, p, w_ret, w_mla, layer)
        last = layer == depth - 1
        x2 = _out_proj(merged, w_o, x2, mod4, layer, seq, g_final if last else None)
    return x2.reshape(bsz, seq, d)
```

```python
import jax
import jax.numpy as jnp
from jax import lax
from jax.experimental import pallas as pl
from jax.experimental.pallas import tpu as pltpu

F32 = jnp.float32
BF16 = jnp.bfloat16

CHUNK = 64
EPS = 1e-6
NEG_INF = -1e30
ROPE_BASE = 10000.0
RET_HEADS = 8
RET_DK = 128
RET_DV = 256
RET_QK_W = RET_HEADS * RET_DK
RET_V_W = RET_HEADS * RET_DV
MLA_HEADS = 16
MLA_Q_RANK = 512
MLA_KV_RANK = 512
MLA_NOPE = 128
MLA_ROPE = 64
MLA_DV = 128
MLA_V_W = MLA_HEADS * MLA_DV
MLA_QK_PAD = 256
MLA_V_PAD = 2 * MLA_DV
LOG2E = 1.4426950408889634

LANES = 128
VMEM_LIMIT = 56 * 1024 * 1024

C_RQ = 0
C_RK = C_RQ + RET_QK_W
C_RV = C_RK + RET_QK_W
C_RG = C_RV + RET_V_W
C_CQ = C_RG + RET_V_W
C_CKV = C_CQ + MLA_Q_RANK
C_MG = C_CKV + MLA_KV_RANK
C_BG = C_MG + MLA_V_W
IN_TN = 1024
IN_CHUNK = 256


def _params(sem):
    return pltpu.CompilerParams(dimension_semantics=sem, vmem_limit_bytes=VMEM_LIMIT)


def _sigmoid(x):
    return 0.5 * jnp.tanh(0.5 * x) + 0.5


def _rope_tables_kernel(pos_ref, invr_ref, invm_ref, cr_ref, sr_ref, cm_ref, sm1_ref, sm2_ref):
    pos = pos_ref[...].astype(F32)
    lane = lax.broadcasted_iota(jnp.int32, cr_ref.shape, 1)
    ang_r = pos * invr_ref[...]
    cr_ref[...] = jnp.cos(ang_r)
    sin_r = jnp.sin(ang_r)
    sr_ref[...] = jnp.where(lane < RET_DK // 2, -sin_r, sin_r)
    ang_m = pos * invm_ref[...]
    cos_m = jnp.cos(ang_m)
    sin_m = jnp.sin(ang_m)
    half = MLA_ROPE // 2
    cm_ref[...] = jnp.where(lane < MLA_ROPE, cos_m, 0.0)
    sm1_ref[...] = jnp.where(lane < half, -sin_m, 0.0)
    sm2_ref[...] = jnp.where((lane >= half) & (lane < MLA_ROPE), sin_m, 0.0)


def _rope_tables(positions):
    t = positions.size
    tm = min(t, 1024)
    inv_r = 1.0 / (ROPE_BASE ** (jnp.arange(0, RET_DK, 2, dtype=F32) / RET_DK))
    inv_m = 1.0 / (ROPE_BASE ** (jnp.arange(0, MLA_ROPE, 2, dtype=F32) / MLA_ROPE))
    invr = jnp.concatenate([inv_r, inv_r])[None, :]
    invm = jnp.concatenate([inv_m, inv_m, jnp.zeros((LANES - MLA_ROPE,), F32)])[None, :]
    row = pl.BlockSpec((tm, LANES), lambda i: (i, 0))
    const = pl.BlockSpec((1, LANES), lambda i: (0, 0))
    return pl.pallas_call(
        _rope_tables_kernel,
        grid=(t // tm,),
        in_specs=[pl.BlockSpec((tm, 1), lambda i: (i, 0)), const, const],
        out_specs=[row] * 5,
        out_shape=[jax.ShapeDtypeStruct((t, LANES), F32)] * 5,
        compiler_params=_params(("parallel",)),
        name="rope_tables",
    )(positions.reshape(t, 1), invr, invm)


def _mod_kernel(c_ref, w_ref, b_ref, o_ref):
    c = c_ref[...]
    ca = (c * _sigmoid(c)).astype(BF16)
    o_ref[...] = jnp.dot(ca, w_ref[...].astype(BF16), preferred_element_type=F32) + b_ref[...]


def _modulation(c, w_mod, b_mod):
    depth, d, n = w_mod.shape
    bsz = c.shape[0]
    rows = 8
    cp = jnp.pad(c, ((0, rows - bsz), (0, 0)))
    tn = 512
    return pl.pallas_call(
        _mod_kernel,
        grid=(depth, n // tn),
        in_specs=[
            pl.BlockSpec((rows, d), lambda l, j: (0, 0)),
            pl.BlockSpec((None, d, tn), lambda l, j: (l, 0, j)),
            pl.BlockSpec((None, 1, tn), lambda l, j: (l, 0, j)),
        ],
        out_specs=pl.BlockSpec((None, rows, tn), lambda l, j: (l, 0, j)),
        out_shape=jax.ShapeDtypeStruct((depth, rows, n), F32),
        compiler_params=_params(("parallel", "parallel")),
        name="modulation",
    )(cp, w_mod, b_mod.reshape(depth, 1, n))


def _cast_kernel(a_ref, o_ref):
    o_ref[...] = a_ref[...].astype(o_ref.dtype)


def _cast_shift_kernel(a_ref, b_ref, o_ref):
    window = jnp.concatenate([a_ref[...], b_ref[...]], axis=1)
    o_ref[...] = window[:, MLA_ROPE:MLA_ROPE + o_ref.shape[1]].astype(o_ref.dtype)


def _cast_w_in(w_in, kr_lo):
    depth, d, n = w_in.shape
    tr = 512
    n_hi = n - kr_lo - MLA_ROPE
    lo_tiles = kr_lo // IN_TN
    assert kr_lo % IN_TN == 0 and n_hi % IN_TN == 0 and d % tr == 0
    w_lo = pl.pallas_call(
        _cast_kernel,
        grid=(depth, d // tr, lo_tiles),
        in_specs=[pl.BlockSpec((None, tr, IN_TN), lambda l, r, j: (l, r, j))],
        out_specs=pl.BlockSpec((None, tr, IN_TN), lambda l, r, j: (l, r, j)),
        out_shape=jax.ShapeDtypeStruct((depth, d, kr_lo), BF16),
        compiler_params=_params(("parallel", "parallel", "parallel")),
        name="cast_w_lo",
    )(w_in)
    lanes_per_tile = IN_TN // LANES
    w_hi = pl.pallas_call(
        _cast_shift_kernel,
        grid=(depth, d // tr, n_hi // IN_TN),
        in_specs=[
            pl.BlockSpec((None, tr, IN_TN), lambda l, r, j: (l, r, lo_tiles + j)),
            pl.BlockSpec((None, tr, LANES), lambda l, r, j: (l, r, (lo_tiles + j + 1) * lanes_per_tile)),
        ],
        out_specs=pl.BlockSpec((None, tr, IN_TN), lambda l, r, j: (l, r, j)),
        out_shape=jax.ShapeDtypeStruct((depth, d, n_hi), BF16),
        compiler_params=_params(("parallel", "parallel", "parallel")),
        name="cast_w_hi",
    )(w_in, w_in)
    return w_lo, w_hi


def _rope_ret(a, cos, sin):
    outs = []
    for hh in range(a.shape[1] // RET_DK):
        t = a[:, hh * RET_DK:(hh + 1) * RET_DK]
        outs.append(t * cos + pltpu.roll(t, RET_DK // 2, 1) * sin)
    return jnp.concatenate(outs, axis=1) if len(outs) > 1 else outs[0]


def _rope_mla(t, cos, sin_lo, sin_hi):
    half = MLA_ROPE // 2
    return t * cos + pltpu.roll(t, LANES - half, 1) * sin_lo + pltpu.roll(t, half, 1) * sin_hi


def _in_proj_kernel(x_ref, shift_ref, scale_ref, g_ref, wlo_ref, whi_ref, wkr_ref, cr_ref, sr_ref,
                    cm_ref, sm1_ref, sm2_ref, gcq_ref, gckv_ref, p_ref, kr_ref, h_scr):
    j = pl.program_id(1)

    @pl.when(j == 0)
    def _():
        x = x_ref[...]
        inv = lax.rsqrt(jnp.mean(x * x, axis=-1, keepdims=True) + EPS)
        h = (x * inv) * g_ref[...]
        h = h * (1.0 + scale_ref[...]) + shift_ref[...]
        hb = h.astype(BF16)
        h_scr[...] = hb
        kr = jnp.dot(hb, wkr_ref[...], preferred_element_type=F32)
        kr_ref[...] = _rope_mla(kr, cm_ref[...], sm1_ref[...], sm2_ref[...]).astype(kr_ref.dtype)

    def tile_range(lo, width):
        return (j >= lo // IN_TN) & (j < (lo + width) // IN_TN)

    def project(w_ref, epilogues, chunk=IN_CHUNK):
        for idx, c0 in enumerate(range(0, IN_TN, chunk)):
            acc = jnp.dot(h_scr[...], w_ref[:, c0:c0 + chunk], preferred_element_type=F32)
            p_ref[:, c0:c0 + chunk] = epilogues[idx % len(epilogues)](acc).astype(p_ref.dtype)

    def rope(acc):
        return _rope_ret(acc, cr_ref[...], sr_ref[...])

    def latent_norm(g_latent_ref):
        def apply(acc):
            inv = lax.rsqrt(jnp.mean(acc * acc, axis=-1, keepdims=True) + EPS)
            return (acc * inv) * g_latent_ref[...]
        return apply

    @pl.when(tile_range(C_RQ, RET_QK_W))
    def _():
        project(wlo_ref, [rope])

    @pl.when(tile_range(C_RK, RET_QK_W))
    def _():
        project(wlo_ref, [lambda acc: rope(acc) * (RET_DK ** -0.5)])

    @pl.when(tile_range(C_RV, RET_V_W))
    def _():
        project(wlo_ref, [lambda acc: acc])

    def silu(acc):
        return acc * _sigmoid(acc)

    @pl.when(tile_range(C_RG, RET_V_W))
    def _():
        project(wlo_ref, [silu])

    @pl.when(tile_range(C_MG, MLA_V_W))
    def _():
        project(whi_ref, [silu])

    @pl.when(tile_range(C_CQ, MLA_Q_RANK + MLA_KV_RANK))
    def _():
        project(wlo_ref, [latent_norm(gcq_ref), latent_norm(gckv_ref)], chunk=MLA_Q_RANK)

    @pl.when(j >= C_BG // IN_TN)
    def _():
        project(whi_ref, [_sigmoid])


def _in_proj(x2, mod4, g_norm3, w_lo, w_hi, w_kr, tabs, g_cq3, g_ckv3, layer, seq):
    t, d = x2.shape
    lo_tiles = C_MG // IN_TN
    n = C_MG + w_hi.shape[2]
    tm = min(seq, 1024)
    per_b = seq // tm
    cr, sr, cm, sm1, sm2 = tabs
    tab = pl.BlockSpec((tm, LANES), lambda i, j: (i, 0))
    return pl.pallas_call(
        _in_proj_kernel,
        grid=(t // tm, n // IN_TN),
        in_specs=[
            pl.BlockSpec((tm, d), lambda i, j: (i, 0)),
            pl.BlockSpec((None, None, 1, d), lambda i, j: (layer, i // per_b, 0, 0)),
            pl.BlockSpec((None, None, 1, d), lambda i, j: (layer, i // per_b, 0, 1)),
            pl.BlockSpec((None, 1, d), lambda i, j: (layer, 0, 0)),
            pl.BlockSpec((None, d, IN_TN), lambda i, j: (layer, 0, jnp.minimum(j, lo_tiles - 1))),
            pl.BlockSpec((None, d, IN_TN), lambda i, j: (layer, 0, jnp.maximum(j - lo_tiles, 0))),
            pl.BlockSpec((None, d, LANES), lambda i, j: (layer, 0, 0)),
            tab, tab, tab, tab, tab,
            pl.BlockSpec((None, 1, MLA_Q_RANK), lambda i, j: (layer, 0, 0)),
            pl.BlockSpec((None, 1, MLA_KV_RANK), lambda i, j: (layer, 0, 0)),
        ],
        out_specs=[
            pl.BlockSpec((tm, IN_TN), lambda i, j: (i, j)),
            pl.BlockSpec((tm, LANES), lambda i, j: (i, 0)),
        ],
        out_shape=[
            jax.ShapeDtypeStruct((t, n), BF16),
            jax.ShapeDtypeStruct((t, LANES), BF16),
        ],
        scratch_shapes=[pltpu.VMEM((tm, d), BF16)],
        compiler_params=_params(("parallel", "arbitrary")),
        name="in_proj",
    )(x2, mod4, mod4, g_norm3, w_lo, w_hi, w_kr, cr, sr, cm, sm1, sm2, g_cq3, g_ckv3)


UP_HEADS = 4


def _latent_up_kernel(cq_ref, ckv_ref, wq_ref, wk_ref, wv_ref, kr_ref, cm_ref, sm1_ref, sm2_ref,
                      q_ref, k_ref, v_ref):
    scale = (MLA_NOPE + MLA_ROPE) ** -0.5 * LOG2E
    cq = cq_ref[...]
    ckv = ckv_ref[...]
    cm, s1, s2 = cm_ref[...], sm1_ref[...], sm2_ref[...]
    kr = kr_ref[...]
    for hh in range(UP_HEADS):
        lo = hh * MLA_QK_PAD
        acc = jnp.dot(cq, wq_ref[:, lo:lo + MLA_QK_PAD], preferred_element_type=F32) * scale
        q_ref[:, lo:lo + MLA_NOPE] = acc[:, :MLA_NOPE].astype(q_ref.dtype)
        q_ref[:, lo + MLA_NOPE:lo + MLA_QK_PAD] = _rope_mla(acc[:, MLA_NOPE:], cm, s1, s2).astype(q_ref.dtype)
    kn = jnp.dot(ckv, wk_ref[...], preferred_element_type=F32)
    vv = jnp.dot(ckv, wv_ref[...], preferred_element_type=F32)
    ones = jnp.ones((cq.shape[0], MLA_V_PAD - MLA_DV), v_ref.dtype)
    for hh in range(UP_HEADS):
        lo = hh * MLA_QK_PAD
        k_ref[:, lo:lo + MLA_NOPE] = kn[:, hh * MLA_NOPE:(hh + 1) * MLA_NOPE].astype(k_ref.dtype)
        k_ref[:, lo + MLA_NOPE:lo + MLA_QK_PAD] = kr
        vlo = hh * MLA_V_PAD
        v_ref[:, vlo:vlo + MLA_DV] = vv[:, hh * MLA_DV:(hh + 1) * MLA_DV].astype(v_ref.dtype)
        v_ref[:, vlo + MLA_DV:vlo + MLA_V_PAD] = ones


def _latent_up(p, kr, w_uq_p, w_k, w_v, tabs, layer):
    t = p.shape[0]
    tm = min(t, 1024)
    _, _, cm, sm1, sm2 = tabs
    tab = pl.BlockSpec((tm, LANES), lambda i, j: (i, 0))

    def head_cols(width):
        return pl.BlockSpec((tm, UP_HEADS * width), lambda i, j: (i, j))

    return pl.pallas_call(
        _latent_up_kernel,
        grid=(t // tm, MLA_HEADS // UP_HEADS),
        in_specs=[
            pl.BlockSpec((tm, MLA_Q_RANK), lambda i, j: (i, C_CQ // MLA_Q_RANK)),
            pl.BlockSpec((tm, MLA_KV_RANK), lambda i, j: (i, C_CKV // MLA_KV_RANK)),
            pl.BlockSpec((None, MLA_Q_RANK, UP_HEADS * MLA_QK_PAD), lambda i, j: (layer, 0, j)),
            pl.BlockSpec((None, MLA_KV_RANK, UP_HEADS * MLA_NOPE), lambda i, j: (layer, 0, j)),
            pl.BlockSpec((None, MLA_KV_RANK, UP_HEADS * MLA_DV), lambda i, j: (layer, 0, j)),
            tab, tab, tab, tab,
        ],
        out_specs=[head_cols(MLA_QK_PAD), head_cols(MLA_QK_PAD), head_cols(MLA_V_PAD)],
        out_shape=[
            jax.ShapeDtypeStruct((t, MLA_HEADS * MLA_QK_PAD), BF16),
            jax.ShapeDtypeStruct((t, MLA_HEADS * MLA_QK_PAD), BF16),
            jax.ShapeDtypeStruct((t, MLA_HEADS * MLA_V_PAD), BF16),
        ],
        compiler_params=_params(("parallel", "arbitrary")),
        name="latent_up",
    )(p, p, w_uq_p, w_k, w_v, kr, cm, sm1, sm2)


RET_BLOCK = 256


RET_GROUP = 2


def _retention_kernel(q_ref, k_ref, v_ref, gate_ref, lg_ref, o_ref, state, dmat, xi, zeta):
    blk = dmat.shape[1]
    nblk = q_ref.shape[0] // blk

    state[...] = jnp.zeros_like(state)
    for hh in range(RET_GROUP):
        lg = lg_ref[hh]
        r = lax.broadcasted_iota(jnp.int32, (blk, blk), 0)
        c = lax.broadcasted_iota(jnp.int32, (blk, blk), 1)
        decay = jnp.exp(jnp.abs(r - c).astype(F32) * lg)
        dmat[hh] = jnp.where((c // CHUNK) <= (r // CHUNK), decay, 0.0)
        rx = lax.broadcasted_iota(jnp.int32, xi.shape[1:], 0).astype(F32)
        xi[hh] = jnp.exp((rx + 1.0) * lg[:, :RET_DV])
        rz = lax.broadcasted_iota(jnp.int32, zeta.shape[1:], 0).astype(F32)
        zeta[hh] = jnp.exp((blk - 1.0 - rz) * lg[:, :RET_DK])

    def body(j, carry):
        rows = pl.ds(pl.multiple_of(j * blk, blk), blk)
        for hh in range(RET_GROUP):
            qk_cols = slice(hh * RET_DK, (hh + 1) * RET_DK)
            v_cols = slice(hh * RET_DV, (hh + 1) * RET_DV)
            q = q_ref[rows, qk_cols]
            k = k_ref[rows, qk_cols]
            v = v_ref[rows, v_cols]
            s = lax.dot_general(q, k, (((1,), (1,)), ((), ())), preferred_element_type=F32)
            o = jnp.dot((s * dmat[hh]).astype(BF16), v, preferred_element_type=F32)
            st = state[hh]
            o = o + xi[hh] * jnp.dot(q, st.astype(BF16), preferred_element_type=F32)
            kz_t = (k.astype(F32) * zeta[hh]).T.astype(BF16)
            block_decay = jnp.exp(blk * lg_ref[hh][:, :RET_DV])
            state[hh] = st * block_decay + jnp.dot(kz_t, v, preferred_element_type=F32)

            mu = jnp.mean(o, axis=-1, keepdims=True)
            dlt = o - mu
            var = jnp.mean(dlt * dlt, axis=-1, keepdims=True)
            y = dlt * lax.rsqrt(var + EPS)
            o_ref[rows, v_cols] = (y * gate_ref[rows, v_cols].astype(F32)).astype(o_ref.dtype)
        return carry

    lax.fori_loop(0, nblk, body, 0, unroll=4)


def _retention(p, bsz, seq):
    t = p.shape[0]
    blk = RET_BLOCK
    g = RET_GROUP
    log_gamma = jnp.log(1.0 - 2.0 ** (-5.0 - jnp.arange(RET_HEADS, dtype=F32)))
    lg = jnp.broadcast_to(log_gamma[:, None, None], (RET_HEADS, 1, blk))
    return pl.pallas_call(
        _retention_kernel,
        grid=(bsz, RET_HEADS // g),
        in_specs=[
            pl.BlockSpec((seq, g * RET_DK), lambda b, h: (b, C_RQ // (g * RET_DK) + h)),
            pl.BlockSpec((seq, g * RET_DK), lambda b, h: (b, C_RK // (g * RET_DK) + h)),
            pl.BlockSpec((seq, g * RET_DV), lambda b, h: (b, C_RV // (g * RET_DV) + h)),
            pl.BlockSpec((seq, g * RET_DV), lambda b, h: (b, C_RG // (g * RET_DV) + h)),
            pl.BlockSpec((g, 1, blk), lambda b, h: (h, 0, 0)),
        ],
        out_specs=pl.BlockSpec((seq, g * RET_DV), lambda b, h: (b, h)),
        out_shape=jax.ShapeDtypeStruct((t, RET_V_W), BF16),
        scratch_shapes=[
            pltpu.VMEM((g, RET_DK, RET_DV), F32),
            pltpu.VMEM((g, blk, blk), F32),
            pltpu.VMEM((g, blk, RET_DV), F32),
            pltpu.VMEM((g, blk, RET_DK), F32),
        ],
        compiler_params=_params(("parallel", "parallel")),
        name="retention",
    )(p, p, p, p, lg)


ATT_BLOCK = 512
ATT_HEADS = 2


def _attention_kernel(q_ref, k_ref, v_ref, gate_ref, bias_ref, o_ref, s_scr, m_scr, acc_scr):
    seq = q_ref.shape[0]
    tq = s_scr.shape[1]
    nq = seq // tq

    def rows(blk):
        return slice(blk * tq, (blk + 1) * tq)

    def slot(next_pair, pair):
        for hh in range(ATT_HEADS):
            qk_cols = slice(hh * MLA_QK_PAD, (hh + 1) * MLA_QK_PAD)
            v_cols = slice(hh * MLA_V_PAD, (hh + 1) * MLA_V_PAD)
            m_cols = slice(hh * LANES, (hh + 1) * LANES)
            if pair is not None:
                qi, kb = pair
                s = s_scr[hh]
                if kb == qi:
                    s = s + bias_ref[...]
                m_cur = jnp.max(s, axis=-1, keepdims=True)
                if kb == 0:
                    m_new = jnp.broadcast_to(m_cur, (tq, LANES))
                else:
                    m_old = m_scr[:, m_cols]
                    m_new = jnp.maximum(m_old, m_cur)
                p = jnp.exp2(s - jnp.concatenate([m_new] * (tq // LANES), axis=1))
                v = v_ref[rows(kb), v_cols]
                pv = jnp.dot(p.astype(BF16), v, preferred_element_type=F32)
                if kb == 0:
                    acc = pv
                else:
                    alpha = jnp.exp2(m_old - m_new)
                    acc = acc_scr[:, v_cols] * jnp.concatenate([alpha] * (MLA_V_PAD // LANES), axis=1) + pv
                if kb == qi:
                    cols = slice(hh * MLA_DV, (hh + 1) * MLA_DV)
                    o = acc[:, :MLA_DV] / acc[:, MLA_DV:]
                    o_ref[rows(qi), cols] = (o * gate_ref[rows(qi), cols].astype(F32)).astype(o_ref.dtype)
                else:
                    acc_scr[:, v_cols] = acc
                    m_scr[:, m_cols] = m_new
            if next_pair is not None:
                q = q_ref[rows(next_pair[0]), qk_cols]
                k = k_ref[rows(next_pair[1]), qk_cols]
                s_scr[hh] = lax.dot_general(q, k, (((1,), (1,)), ((), ())), preferred_element_type=F32)

    slot((0, 0), None)
    for qi in range(nq):
        @pl.when(pl.program_id(0) + qi >= 0)
        def _(qi=qi):
            for kb in range(qi + 1):
                if kb < qi:
                    nxt = (qi, kb + 1)
                else:
                    nxt = (qi + 1, 0) if qi + 1 < nq else None
                slot(nxt, (qi, kb))


def _attention(qc, kc, v, p, bsz, seq):
    t = qc.shape[0]
    tq = min(seq, ATT_BLOCK)
    nh = ATT_HEADS
    r = lax.broadcasted_iota(jnp.int32, (tq, tq), 0) // CHUNK
    c = lax.broadcasted_iota(jnp.int32, (tq, tq), 1) // CHUNK
    bias = jnp.where(c <= r, 0.0, NEG_INF * LOG2E).astype(F32)
    return pl.pallas_call(
        _attention_kernel,
        grid=(bsz, MLA_HEADS // nh),
        in_specs=[
            pl.BlockSpec((seq, nh * MLA_QK_PAD), lambda b, h: (b, h)),
            pl.BlockSpec((seq, nh * MLA_QK_PAD), lambda b, h: (b, h)),
            pl.BlockSpec((seq, nh * MLA_V_PAD), lambda b, h: (b, h)),
            pl.BlockSpec((seq, nh * MLA_DV), lambda b, h: (b, C_MG // (nh * MLA_DV) + h)),
            pl.BlockSpec((tq, tq), lambda b, h: (0, 0)),
        ],
        out_specs=pl.BlockSpec((seq, nh * MLA_DV), lambda b, h: (b, h)),
        out_shape=jax.ShapeDtypeStruct((t, MLA_V_W), BF16),
        scratch_shapes=[
            pltpu.VMEM((nh, tq, tq), F32),
            pltpu.VMEM((tq, nh * LANES), F32),
            pltpu.VMEM((tq, nh * MLA_V_PAD), F32),
        ],
        compiler_params=_params(("parallel", "parallel")),
        name="attention",
    )(qc, kc, v, p, bias)


PROJ_CHUNK = 256


def _merge_kernel(a_ref, b_ref, wr_ref, wm_ref, ga_ref, gb_ref, o_ref):
    for c0 in range(0, o_ref.shape[1], PROJ_CHUNK):
        cols = slice(c0, c0 + PROJ_CHUNK)
        y_ret = jnp.dot(a_ref[...], wr_ref[:, cols], preferred_element_type=F32)
        y_mla = jnp.dot(b_ref[...], wm_ref[:, cols], preferred_element_type=F32)
        merged = ga_ref[:, cols].astype(F32) * y_ret + gb_ref[:, cols].astype(F32) * y_mla
        o_ref[:, cols] = merged.astype(o_ref.dtype)


def _merge(a, bm, p, w_ret, w_mla, layer):
    t, d_in = a.shape
    d = w_ret.shape[2]
    tm = min(t, 1024)
    tn = 512
    return pl.pallas_call(
        _merge_kernel,
        grid=(t // tm, d // tn),
        in_specs=[
            pl.BlockSpec((tm, d_in), lambda i, j: (i, 0)),
            pl.BlockSpec((tm, d_in), lambda i, j: (i, 0)),
            pl.BlockSpec((None, d_in, tn), lambda i, j: (layer, 0, j)),
            pl.BlockSpec((None, d_in, tn), lambda i, j: (layer, 0, j)),
            pl.BlockSpec((tm, tn), lambda i, j: (i, C_BG // tn + j)),
            pl.BlockSpec((tm, tn), lambda i, j: (i, (C_BG + d) // tn + j)),
        ],
        out_specs=pl.BlockSpec((tm, tn), lambda i, j: (i, j)),
        out_shape=jax.ShapeDtypeStruct((t, d), BF16),
        compiler_params=_params(("parallel", "arbitrary")),
        name="merge_proj",
    )(a, bm, w_ret, w_mla, p, p)


def _out_kernel(m_ref, w_ref, x_ref, gate_ref, o_ref):
    for c0 in range(0, o_ref.shape[1], PROJ_CHUNK):
        cols = slice(c0, c0 + PROJ_CHUNK)
        out = jnp.dot(m_ref[...], w_ref[:, cols], preferred_element_type=F32)
        o_ref[:, cols] = x_ref[:, cols] + gate_ref[:, cols] * out


def _out_norm_kernel(m_ref, w_ref, x_ref, gate_ref, g_ref, o_ref):
    _out_kernel(m_ref, w_ref, x_ref, gate_ref, o_ref)
    y = o_ref[...]
    inv = lax.rsqrt(jnp.mean(y * y, axis=-1, keepdims=True) + EPS)
    o_ref[...] = (y * inv) * g_ref[...]


def _out_proj(merged, w_out, x2, mod4, layer, seq, g_final=None):
    t, d = x2.shape
    fuse_norm = g_final is not None
    tm = min(seq, 512 if fuse_norm else 1024)
    per_b = seq // tm
    tn = d if fuse_norm else 1024
    gate_blk = 2 * d // tn
    in_specs = [
        pl.BlockSpec((tm, d), lambda i, j: (i, 0)),
        pl.BlockSpec((None, d, tn), lambda i, j: (layer, 0, j)),
        pl.BlockSpec((tm, tn), lambda i, j: (i, j)),
        pl.BlockSpec((None, None, 1, tn), lambda i, j: (layer, i // per_b, 0, gate_blk + j)),
    ]
    args = [merged, w_out, x2, mod4]
    if fuse_norm:
        in_specs.append(pl.BlockSpec((1, d), lambda i, j: (0, 0)))
        args.append(g_final.reshape(1, d))
    return pl.pallas_call(
        _out_norm_kernel if fuse_norm else _out_kernel,
        grid=(t // tm, d // tn),
        in_specs=in_specs,
        out_specs=pl.BlockSpec((tm, tn), lambda i, j: (i, j)),
        out_shape=jax.ShapeDtypeStruct((t, d), F32),
        compiler_params=_params(("parallel", "arbitrary")),
        name="out_norm" if fuse_norm else "out_proj",
    )(*args)


def kernel(x, c, positions, w_mod, b_mod, g_norm, w_in, g_cq, g_ckv, w_uq, w_ukv,
           w_ret_proj, w_mla_proj, w_out, g_final):
    bsz, seq, d = x.shape
    depth = w_in.shape[0]
    t = bsz * seq
    assert seq % RET_BLOCK == 0 and seq % min(seq, ATT_BLOCK) == 0 and d % IN_TN == 0

    kr_lo = C_CKV + MLA_KV_RANK
    w_lo, w_hi = _cast_w_in(w_in, kr_lo)
    w_kr = jnp.pad(w_in[:, :, kr_lo:kr_lo + MLA_ROPE], ((0, 0), (0, 0), (0, LANES - MLA_ROPE))).astype(BF16)
    w_uq_p = jnp.pad(
        w_uq.reshape(depth, MLA_Q_RANK, MLA_HEADS, MLA_NOPE + MLA_ROPE),
        ((0, 0), (0, 0), (0, 0), (0, MLA_QK_PAD - MLA_NOPE - MLA_ROPE)),
    ).reshape(depth, MLA_Q_RANK, MLA_HEADS * MLA_QK_PAD).astype(BF16)
    w_ukv4 = w_ukv.reshape(depth, MLA_KV_RANK, MLA_HEADS, MLA_NOPE + MLA_DV)
    w_k = w_ukv4[..., :MLA_NOPE].reshape(depth, MLA_KV_RANK, MLA_HEADS * MLA_NOPE).astype(BF16)
    w_v = w_ukv4[..., MLA_NOPE:].reshape(depth, MLA_KV_RANK, MLA_V_W).astype(BF16)
    w_ret = w_ret_proj.astype(BF16)
    w_mla = w_mla_proj.astype(BF16)
    w_o = w_out.astype(BF16)

    tabs = _rope_tables(positions)
    mod = _modulation(c, w_mod, b_mod)
    mod4 = mod.reshape(depth, mod.shape[1], 1, 3 * d)
    g_norm3 = g_norm.reshape(depth, 1, d)
    g_cq3 = g_cq.reshape(depth, 1, MLA_Q_RANK)
    g_ckv3 = g_ckv.reshape(depth, 1, MLA_KV_RANK)

    x2 = x.reshape(t, d)
    for layer in range(depth):
        p, kr = _in_proj(x2, mod4, g_norm3, w_lo, w_hi, w_kr, tabs, g_cq3, g_ckv3, layer, seq)
        qc, kc, v = _latent_up(p, kr, w_uq_p, w_k, w_v, tabs, layer)
        a = _retention(p, bsz, seq)
        bm = _attention(qc, kc, v, p, bsz, seq)
        merged = _merge(a, bm, p, w_ret, w_mla, layer)
        last = layer == depth - 1
        x2 = _out_proj(merged, w_o, x2, mod4, layer, seq, g_final if last else None)
    return x2.reshape(bsz, seq, d)
```

```python
import jax
import jax.numpy as jnp
from jax import lax
from jax.experimental import pallas as pl
from jax.experimental.pallas import tpu as pltpu

F32 = jnp.float32
BF16 = jnp.bfloat16

CHUNK = 64
EPS = 1e-6
NEG_INF = -1e30
ROPE_BASE = 10000.0
RET_HEADS = 8
RET_DK = 128
RET_DV = 256
RET_QK_W = RET_HEADS * RET_DK
RET_V_W = RET_HEADS * RET_DV
MLA_HEADS = 16
MLA_Q_RANK = 512
MLA_KV_RANK = 512
MLA_NOPE = 128
MLA_ROPE = 64
MLA_DV = 128
MLA_V_W = MLA_HEADS * MLA_DV
MLA_QK_PAD = 256
MLA_V_PAD = 2 * MLA_DV
LOG2E = 1.4426950408889634

LANES = 128
VMEM_LIMIT = 56 * 1024 * 1024

C_RQ = 0
C_RK = C_RQ + RET_QK_W
C_RV = C_RK + RET_QK_W
C_RG = C_RV + RET_V_W
C_CQ = C_RG + RET_V_W
C_CKV = C_CQ + MLA_Q_RANK
C_MG = C_CKV + MLA_KV_RANK
C_BG = C_MG + MLA_V_W
IN_TN = 1024
IN_CHUNK = 256


def _params(sem):
    return pltpu.CompilerParams(dimension_semantics=sem, vmem_limit_bytes=VMEM_LIMIT)


def _sigmoid(x):
    return 0.5 * jnp.tanh(0.5 * x) + 0.5


def _rope_tables_kernel(pos_ref, inv_ref, cr_ref, sr_ref, cm_ref, sm1_ref, sm2_ref):
    pos = pos_ref[...].astype(F32)
    lane = lax.broadcasted_iota(jnp.int32, cr_ref.shape, 1)
    ang = pos * inv_ref[...]
    c = jnp.cos(ang)
    s = jnp.sin(ang)
    ret_half = RET_DK // 2
    half = MLA_ROPE // 2
    c_up, s_up = pltpu.roll(c, LANES - ret_half, 1), pltpu.roll(s, LANES - ret_half, 1)
    c_up2, s_up2 = pltpu.roll(c, LANES - half, 1), pltpu.roll(s, LANES - half, 1)
    cr_ref[...] = jnp.where(lane < ret_half, c, pltpu.roll(c, ret_half, 1))
    sr_ref[...] = jnp.where(lane < ret_half, -s, pltpu.roll(s, ret_half, 1))
    in_lo = lane < half
    in_hi = (lane >= half) & (lane < MLA_ROPE)
    cm_ref[...] = jnp.where(in_lo, c_up, jnp.where(in_hi, c_up2, 0.0))
    sm1_ref[...] = jnp.where(in_lo, -s_up, 0.0)
    sm2_ref[...] = jnp.where(in_hi, s_up2, 0.0)


def _rope_tables(positions):
    t = positions.size
    tm = min(t, 1024)
    inv_r = 1.0 / (ROPE_BASE ** (jnp.arange(0, RET_DK, 2, dtype=F32) / RET_DK))
    inv_m = 1.0 / (ROPE_BASE ** (jnp.arange(0, MLA_ROPE, 2, dtype=F32) / MLA_ROPE))
    inv = jnp.concatenate([inv_r, inv_m, jnp.zeros((LANES - RET_DK // 2 - MLA_ROPE // 2,), F32)])[None, :]
    row = pl.BlockSpec((tm, LANES), lambda i: (i, 0))
    return pl.pallas_call(
        _rope_tables_kernel,
        grid=(t // tm,),
        in_specs=[pl.BlockSpec((tm, 1), lambda i: (i, 0)), pl.BlockSpec((1, LANES), lambda i: (0, 0))],
        out_specs=[row] * 5,
        out_shape=[jax.ShapeDtypeStruct((t, LANES), F32)] * 5,
        compiler_params=_params(("parallel",)),
        name="rope_tables",
    )(positions.reshape(t, 1), inv)


def _mod_kernel(c_ref, w_ref, b_ref, o_ref):
    c = c_ref[...]
    ca = (c * _sigmoid(c)).astype(BF16)
    o_ref[...] = jnp.dot(ca, w_ref[...].astype(BF16), preferred_element_type=F32) + b_ref[...]


def _modulation(c, w_mod, b_mod):
    depth, d, n = w_mod.shape
    bsz = c.shape[0]
    rows = 8
    cp = jnp.pad(c, ((0, rows - bsz), (0, 0)))
    tn = 512
    return pl.pallas_call(
        _mod_kernel,
        grid=(depth, n // tn),
        in_specs=[
            pl.BlockSpec((rows, d), lambda l, j: (0, 0)),
            pl.BlockSpec((None, d, tn), lambda l, j: (l, 0, j)),
            pl.BlockSpec((None, 1, tn), lambda l, j: (l, 0, j)),
        ],
        out_specs=pl.BlockSpec((None, rows, tn), lambda l, j: (l, 0, j)),
        out_shape=jax.ShapeDtypeStruct((depth, rows, n), F32),
        compiler_params=_params(("parallel", "parallel")),
        name="modulation",
    )(cp, w_mod, b_mod.reshape(depth, 1, n))


def _rope_ret(a, cos, sin):
    outs = []
    for hh in range(a.shape[1] // RET_DK):
        t = a[:, hh * RET_DK:(hh + 1) * RET_DK]
        outs.append(t * cos + pltpu.roll(t, RET_DK // 2, 1) * sin)
    return jnp.concatenate(outs, axis=1) if len(outs) > 1 else outs[0]


def _rope_mla(t, cos, sin_lo, sin_hi):
    half = MLA_ROPE // 2
    return t * cos + pltpu.roll(t, LANES - half, 1) * sin_lo + pltpu.roll(t, half, 1) * sin_hi


def _in_proj_kernel(x_ref, shift_ref, scale_ref, g_ref, wlo_ref, whi_ref, wkr_ref, cr_ref, sr_ref,
                    cm_ref, sm1_ref, sm2_ref, gcq_ref, gckv_ref, p_ref, kr_ref, h_scr):
    j = pl.program_id(1)

    @pl.when(j == 0)
    def _():
        x = x_ref[...]
        inv = lax.rsqrt(jnp.mean(x * x, axis=-1, keepdims=True) + EPS)
        h = (x * inv) * g_ref[...]
        h = h * (1.0 + scale_ref[...]) + shift_ref[...]
        hb = h.astype(BF16)
        h_scr[...] = hb
        kr = jnp.dot(hb, wkr_ref[...], preferred_element_type=F32)
        kr_ref[...] = _rope_mla(kr, cm_ref[...], sm1_ref[...], sm2_ref[...]).astype(kr_ref.dtype)

    def tile_range(lo, width):
        return (j >= lo // IN_TN) & (j < (lo + width) // IN_TN)

    def project(w_ref, epilogues, chunk=IN_CHUNK):
        for idx, c0 in enumerate(range(0, IN_TN, chunk)):
            acc = jnp.dot(h_scr[...], w_ref[:, c0:c0 + chunk], preferred_element_type=F32)
            p_ref[:, c0:c0 + chunk] = epilogues[idx % len(epilogues)](acc).astype(p_ref.dtype)

    def rope(acc):
        return _rope_ret(acc, cr_ref[...], sr_ref[...])

    def latent_norm(g_latent_ref):
        def apply(acc):
            inv = lax.rsqrt(jnp.mean(acc * acc, axis=-1, keepdims=True) + EPS)
            return (acc * inv) * g_latent_ref[...]
        return apply

    @pl.when(tile_range(C_RQ, RET_QK_W))
    def _():
        project(wlo_ref, [rope])

    @pl.when(tile_range(C_RK, RET_QK_W))
    def _():
        project(wlo_ref, [lambda acc: rope(acc) * (RET_DK ** -0.5)])

    @pl.when(tile_range(C_RV, RET_V_W))
    def _():
        project(wlo_ref, [lambda acc: acc])

    def silu(acc):
        return acc * _sigmoid(acc)

    @pl.when(tile_range(C_RG, RET_V_W))
    def _():
        project(wlo_ref, [silu])

    @pl.when(tile_range(C_MG, MLA_V_W))
    def _():
        project(whi_ref, [silu])

    @pl.when(tile_range(C_CQ, MLA_Q_RANK + MLA_KV_RANK))
    def _():
        project(wlo_ref, [latent_norm(gcq_ref), latent_norm(gckv_ref)], chunk=MLA_Q_RANK)

    @pl.when(j >= C_BG // IN_TN)
    def _():
        project(whi_ref, [_sigmoid])


def _in_proj(x2, mod4, g_norm3, w_lo, w_hi, w_kr, tabs, g_cq3, g_ckv3, layer, seq):
    t, d = x2.shape
    lo_tiles = C_MG // IN_TN
    n = C_MG + w_hi.shape[2]
    tm = min(seq, 1024)
    per_b = seq // tm
    cr, sr, cm, sm1, sm2 = tabs
    tab = pl.BlockSpec((tm, LANES), lambda i, j: (i, 0))
    return pl.pallas_call(
        _in_proj_kernel,
        grid=(t // tm, n // IN_TN),
        in_specs=[
            pl.BlockSpec((tm, d), lambda i, j: (i, 0)),
            pl.BlockSpec((None, None, 1, d), lambda i, j: (layer, i // per_b, 0, 0)),
            pl.BlockSpec((None, None, 1, d), lambda i, j: (layer, i // per_b, 0, 1)),
            pl.BlockSpec((None, 1, d), lambda i, j: (layer, 0, 0)),
            pl.BlockSpec((None, d, IN_TN), lambda i, j: (layer, 0, jnp.minimum(j, lo_tiles - 1))),
            pl.BlockSpec((None, d, IN_TN), lambda i, j: (layer, 0, jnp.maximum(j - lo_tiles, 0))),
            pl.BlockSpec((None, d, LANES), lambda i, j: (layer, 0, 0)),
            tab, tab, tab, tab, tab,
            pl.BlockSpec((None, 1, MLA_Q_RANK), lambda i, j: (layer, 0, 0)),
            pl.BlockSpec((None, 1, MLA_KV_RANK), lambda i, j: (layer, 0, 0)),
        ],
        out_specs=[
            pl.BlockSpec((tm, IN_TN), lambda i, j: (i, j)),
            pl.BlockSpec((tm, LANES), lambda i, j: (i, 0)),
        ],
        out_shape=[
            jax.ShapeDtypeStruct((t, n), BF16),
            jax.ShapeDtypeStruct((t, LANES), BF16),
        ],
        scratch_shapes=[pltpu.VMEM((tm, d), BF16)],
        compiler_params=_params(("parallel", "arbitrary")),
        name="in_proj",
    )(x2, mod4, mod4, g_norm3, w_lo, w_hi, w_kr, cr, sr, cm, sm1, sm2, g_cq3, g_ckv3)


UP_HEADS = 4


def _latent_up_kernel(cq_ref, ckv_ref, wq_ref, wk_ref, wv_ref, kr_ref, cm_ref, sm1_ref, sm2_ref,
                      q_ref, k_ref, v_ref):
    scale = (MLA_NOPE + MLA_ROPE) ** -0.5 * LOG2E
    cq = cq_ref[...]
    ckv = ckv_ref[...]
    cm, s1, s2 = cm_ref[...], sm1_ref[...], sm2_ref[...]
    kr = kr_ref[...]
    for hh in range(UP_HEADS):
        lo = hh * MLA_QK_PAD
        acc = jnp.dot(cq, wq_ref[:, lo:lo + MLA_QK_PAD], preferred_element_type=F32) * scale
        q_ref[:, lo:lo + MLA_NOPE] = acc[:, :MLA_NOPE].astype(q_ref.dtype)
        q_ref[:, lo + MLA_NOPE:lo + MLA_QK_PAD] = _rope_mla(acc[:, MLA_NOPE:], cm, s1, s2).astype(q_ref.dtype)
    kn = jnp.dot(ckv, wk_ref[...], preferred_element_type=F32)
    vv = jnp.dot(ckv, wv_ref[...], preferred_element_type=F32)
    ones = jnp.ones((cq.shape[0], MLA_V_PAD - MLA_DV), v_ref.dtype)
    for hh in range(UP_HEADS):
        lo = hh * MLA_QK_PAD
        k_ref[:, lo:lo + MLA_NOPE] = kn[:, hh * MLA_NOPE:(hh + 1) * MLA_NOPE].astype(k_ref.dtype)
        k_ref[:, lo + MLA_NOPE:lo + MLA_QK_PAD] = kr
        vlo = hh * MLA_V_PAD
        v_ref[:, vlo:vlo + MLA_DV] = vv[:, hh * MLA_DV:(hh + 1) * MLA_DV].astype(v_ref.dtype)
        v_ref[:, vlo + MLA_DV:vlo + MLA_V_PAD] = ones


def _latent_up(p, kr, w_uq_p, w_k, w_v, tabs, layer):
    t = p.shape[0]
    tm = min(t, 1024)
    _, _, cm, sm1, sm2 = tabs
    tab = pl.BlockSpec((tm, LANES), lambda i, j: (i, 0))

    def head_cols(width):
        return pl.BlockSpec((tm, UP_HEADS * width), lambda i, j: (i, j))

    return pl.pallas_call(
        _latent_up_kernel,
        grid=(t // tm, MLA_HEADS // UP_HEADS),
        in_specs=[
            pl.BlockSpec((tm, MLA_Q_RANK), lambda i, j: (i, C_CQ // MLA_Q_RANK)),
            pl.BlockSpec((tm, MLA_KV_RANK), lambda i, j: (i, C_CKV // MLA_KV_RANK)),
            pl.BlockSpec((None, MLA_Q_RANK, UP_HEADS * MLA_QK_PAD), lambda i, j: (layer, 0, j)),
            pl.BlockSpec((None, MLA_KV_RANK, UP_HEADS * MLA_NOPE), lambda i, j: (layer, 0, j)),
            pl.BlockSpec((None, MLA_KV_RANK, UP_HEADS * MLA_DV), lambda i, j: (layer, 0, j)),
            tab, tab, tab, tab,
        ],
        out_specs=[head_cols(MLA_QK_PAD), head_cols(MLA_QK_PAD), head_cols(MLA_V_PAD)],
        out_shape=[
            jax.ShapeDtypeStruct((t, MLA_HEADS * MLA_QK_PAD), BF16),
            jax.ShapeDtypeStruct((t, MLA_HEADS * MLA_QK_PAD), BF16),
            jax.ShapeDtypeStruct((t, MLA_HEADS * MLA_V_PAD), BF16),
        ],
        compiler_params=_params(("parallel", "arbitrary")),
        name="latent_up",
    )(p, p, w_uq_p, w_k, w_v, kr, cm, sm1, sm2)


RET_BLOCK = 256


RET_GROUP = 2


def _retention_kernel(q_ref, k_ref, v_ref, gate_ref, lg_ref, o_ref, state, dmat, xi, zeta):
    blk = dmat.shape[1]
    nblk = q_ref.shape[0] // blk

    state[...] = jnp.zeros_like(state)
    for hh in range(RET_GROUP):
        lg = lg_ref[hh]
        r = lax.broadcasted_iota(jnp.int32, (blk, blk), 0)
        c = lax.broadcasted_iota(jnp.int32, (blk, blk), 1)
        decay = jnp.exp(jnp.abs(r - c).astype(F32) * lg)
        dmat[hh] = jnp.where((c // CHUNK) <= (r // CHUNK), decay, 0.0)
        rx = lax.broadcasted_iota(jnp.int32, xi.shape[1:], 0).astype(F32)
        xi[hh] = jnp.exp((rx + 1.0) * lg[:, :RET_DV])
        rz = lax.broadcasted_iota(jnp.int32, zeta.shape[1:], 0).astype(F32)
        zeta[hh] = jnp.exp((blk - 1.0 - rz) * lg[:, :RET_DK])

    def body(j, carry):
        rows = pl.ds(pl.multiple_of(j * blk, blk), blk)
        for hh in range(RET_GROUP):
            qk_cols = slice(hh * RET_DK, (hh + 1) * RET_DK)
            v_cols = slice(hh * RET_DV, (hh + 1) * RET_DV)
            q = q_ref[rows, qk_cols]
            k = k_ref[rows, qk_cols]
            v = v_ref[rows, v_cols]
            s = lax.dot_general(q, k, (((1,), (1,)), ((), ())), preferred_element_type=F32)
            o = jnp.dot((s * dmat[hh]).astype(BF16), v, preferred_element_type=F32)
            st = state[hh]
            o = o + xi[hh] * jnp.dot(q, st.astype(BF16), preferred_element_type=F32)
            kz_t = (k.astype(F32) * zeta[hh]).T.astype(BF16)
            block_decay = jnp.exp(blk * lg_ref[hh][:, :RET_DV])
            state[hh] = st * block_decay + jnp.dot(kz_t, v, preferred_element_type=F32)

            mu = jnp.mean(o, axis=-1, keepdims=True)
            dlt = o - mu
            var = jnp.mean(dlt * dlt, axis=-1, keepdims=True)
            y = dlt * lax.rsqrt(var + EPS)
            o_ref[rows, v_cols] = (y * gate_ref[rows, v_cols].astype(F32)).astype(o_ref.dtype)
        return carry

    lax.fori_loop(0, nblk, body, 0, unroll=4)


def _retention(p, bsz, seq):
    t = p.shape[0]
    blk = RET_BLOCK
    g = RET_GROUP
    log_gamma = jnp.log(1.0 - 2.0 ** (-5.0 - jnp.arange(RET_HEADS, dtype=F32)))
    lg = jnp.broadcast_to(log_gamma[:, None, None], (RET_HEADS, 1, blk))
    return pl.pallas_call(
        _retention_kernel,
        grid=(bsz, RET_HEADS // g),
        in_specs=[
            pl.BlockSpec((seq, g * RET_DK), lambda b, h: (b, C_RQ // (g * RET_DK) + h)),
            pl.BlockSpec((seq, g * RET_DK), lambda b, h: (b, C_RK // (g * RET_DK) + h)),
            pl.BlockSpec((seq, g * RET_DV), lambda b, h: (b, C_RV // (g * RET_DV) + h)),
            pl.BlockSpec((seq, g * RET_DV), lambda b, h: (b, C_RG // (g * RET_DV) + h)),
            pl.BlockSpec((g, 1, blk), lambda b, h: (h, 0, 0)),
        ],
        out_specs=pl.BlockSpec((seq, g * RET_DV), lambda b, h: (b, h)),
        out_shape=jax.ShapeDtypeStruct((t, RET_V_W), BF16),
        scratch_shapes=[
            pltpu.VMEM((g, RET_DK, RET_DV), F32),
            pltpu.VMEM((g, blk, blk), F32),
            pltpu.VMEM((g, blk, RET_DV), F32),
            pltpu.VMEM((g, blk, RET_DK), F32),
        ],
        compiler_params=_params(("parallel", "parallel")),
        name="retention",
    )(p, p, p, p, lg)


ATT_BLOCK = 512
ATT_HEADS = 2
ATT_REGION_PAIRS = 11


def _attention_kernel(q_ref, k_ref, v_ref, gate_ref, bias_ref, o_ref, s_scr, m_scr, acc_scr):
    seq = q_ref.shape[0]
    tq = s_scr.shape[1]
    nq = seq // tq

    def rows(blk):
        return slice(blk * tq, (blk + 1) * tq)

    def slot(next_pair, pair):
        for hh in range(ATT_HEADS):
            qk_cols = slice(hh * MLA_QK_PAD, (hh + 1) * MLA_QK_PAD)
            v_cols = slice(hh * MLA_V_PAD, (hh + 1) * MLA_V_PAD)
            m_cols = slice(hh * LANES, (hh + 1) * LANES)
            if pair is not None:
                qi, kb = pair
                s = s_scr[hh]
                if kb == qi:
                    s = s + bias_ref[...]
                m_cur = jnp.max(s, axis=-1, keepdims=True)
                if kb == 0:
                    m_new = jnp.broadcast_to(m_cur, (tq, LANES))
                else:
                    m_old = m_scr[:, m_cols]
                    m_new = jnp.maximum(m_old, m_cur)
                p = jnp.exp2(s - jnp.concatenate([m_new] * (tq // LANES), axis=1))
                v = v_ref[rows(kb), v_cols]
                pv = jnp.dot(p.astype(BF16), v, preferred_element_type=F32)
                if kb == 0:
                    acc = pv
                else:
                    alpha = jnp.exp2(m_old - m_new)
                    acc = acc_scr[:, v_cols] * jnp.concatenate([alpha] * (MLA_V_PAD // LANES), axis=1) + pv
                if kb == qi:
                    cols = slice(hh * MLA_DV, (hh + 1) * MLA_DV)
                    o = acc[:, :MLA_DV] / acc[:, MLA_DV:]
                    o_ref[rows(qi), cols] = (o * gate_ref[rows(qi), cols].astype(F32)).astype(o_ref.dtype)
                else:
                    acc_scr[:, v_cols] = acc
                    m_scr[:, m_cols] = m_new
            if next_pair is not None:
                q = q_ref[rows(next_pair[0]), qk_cols]
                k = k_ref[rows(next_pair[1]), qk_cols]
                s_scr[hh] = lax.dot_general(q, k, (((1,), (1,)), ((), ())), preferred_element_type=F32)

    pairs = [(qi, kb) for qi in range(nq) for kb in range(qi + 1)]
    regions, current = [], []
    for qi in range(nq):
        if current and len(current) + qi + 1 > ATT_REGION_PAIRS:
            regions.append(current)
            current = []
        current += [(qi, kb) for kb in range(qi + 1)]
    regions.append(current)

    for r, region in enumerate(regions):
        @pl.when(pl.program_id(0) + r >= 0)
        def _(r=r, region=region):
            if r == 0:
                slot(pairs[0], None)
            for pair in region:
                n = pairs.index(pair)
                slot(pairs[n + 1] if n + 1 < len(pairs) else None, pair)


def _attention(qc, kc, v, p, bsz, seq):
    t = qc.shape[0]
    tq = min(seq, ATT_BLOCK)
    nh = ATT_HEADS
    r = lax.broadcasted_iota(jnp.int32, (tq, tq), 0) // CHUNK
    c = lax.broadcasted_iota(jnp.int32, (tq, tq), 1) // CHUNK
    bias = jnp.where(c <= r, 0.0, NEG_INF * LOG2E).astype(F32)
    return pl.pallas_call(
        _attention_kernel,
        grid=(bsz, MLA_HEADS // nh),
        in_specs=[
            pl.BlockSpec((seq, nh * MLA_QK_PAD), lambda b, h: (b, h)),
            pl.BlockSpec((seq, nh * MLA_QK_PAD), lambda b, h: (b, h)),
            pl.BlockSpec((seq, nh * MLA_V_PAD), lambda b, h: (b, h)),
            pl.BlockSpec((seq, nh * MLA_DV), lambda b, h: (b, C_MG // (nh * MLA_DV) + h)),
            pl.BlockSpec((tq, tq), lambda b, h: (0, 0)),
        ],
        out_specs=pl.BlockSpec((seq, nh * MLA_DV), lambda b, h: (b, h)),
        out_shape=jax.ShapeDtypeStruct((t, MLA_V_W), BF16),
        scratch_shapes=[
            pltpu.VMEM((nh, tq, tq), F32),
            pltpu.VMEM((tq, nh * LANES), F32),
            pltpu.VMEM((tq, nh * MLA_V_PAD), F32),
        ],
        compiler_params=_params(("parallel", "parallel")),
        name="attention",
    )(qc, kc, v, p, bias)


PROJ_CHUNK = 256


def _merge_kernel(a_ref, b_ref, wr_ref, wm_ref, ga_ref, gb_ref, o_ref):
    for c0 in range(0, o_ref.shape[1], PROJ_CHUNK):
        cols = slice(c0, c0 + PROJ_CHUNK)
        y_ret = jnp.dot(a_ref[...], wr_ref[:, cols], preferred_element_type=F32)
        y_mla = jnp.dot(b_ref[...], wm_ref[:, cols], preferred_element_type=F32)
        merged = ga_ref[:, cols].astype(F32) * y_ret + gb_ref[:, cols].astype(F32) * y_mla
        o_ref[:, cols] = merged.astype(o_ref.dtype)


def _merge(a, bm, p, w_ret, w_mla, layer):
    t, d_in = a.shape
    d = w_ret.shape[2]
    tm = min(t, 1024)
    tn = 512
    return pl.pallas_call(
        _merge_kernel,
        grid=(t // tm, d // tn),
        in_specs=[
            pl.BlockSpec((tm, d_in), lambda i, j: (i, 0)),
            pl.BlockSpec((tm, d_in), lambda i, j: (i, 0)),
            pl.BlockSpec((None, d_in, tn), lambda i, j: (layer, 0, j)),
            pl.BlockSpec((None, d_in, tn), lambda i, j: (layer, 0, j)),
            pl.BlockSpec((tm, tn), lambda i, j: (i, C_BG // tn + j)),
            pl.BlockSpec((tm, tn), lambda i, j: (i, (C_BG + d) // tn + j)),
        ],
        out_specs=pl.BlockSpec((tm, tn), lambda i, j: (i, j)),
        out_shape=jax.ShapeDtypeStruct((t, d), BF16),
        compiler_params=_params(("parallel", "arbitrary")),
        name="merge_proj",
    )(a, bm, w_ret, w_mla, p, p)


def _out_kernel(m_ref, w_ref, x_ref, gate_ref, o_ref):
    for c0 in range(0, o_ref.shape[1], PROJ_CHUNK):
        cols = slice(c0, c0 + PROJ_CHUNK)
        out = jnp.dot(m_ref[...], w_ref[:, cols], preferred_element_type=F32)
        o_ref[:, cols] = x_ref[:, cols] + gate_ref[:, cols] * out


def _out_norm_kernel(m_ref, w_ref, x_ref, gate_ref, g_ref, o_ref):
    _out_kernel(m_ref, w_ref, x_ref, gate_ref, o_ref)
    y = o_ref[...]
    inv = lax.rsqrt(jnp.mean(y * y, axis=-1, keepdims=True) + EPS)
    o_ref[...] = (y * inv) * g_ref[...]


def _out_proj(merged, w_out, x2, mod4, layer, seq, g_final=None):
    t, d = x2.shape
    fuse_norm = g_final is not None
    tm = min(seq, 512 if fuse_norm else 1024)
    per_b = seq // tm
    tn = d if fuse_norm else 1024
    gate_blk = 2 * d // tn
    in_specs = [
        pl.BlockSpec((tm, d), lambda i, j: (i, 0)),
        pl.BlockSpec((None, d, tn), lambda i, j: (layer, 0, j)),
        pl.BlockSpec((tm, tn), lambda i, j: (i, j)),
        pl.BlockSpec((None, None, 1, tn), lambda i, j: (layer, i // per_b, 0, gate_blk + j)),
    ]
    args = [merged, w_out, x2, mod4]
    if fuse_norm:
        in_specs.append(pl.BlockSpec((1, d), lambda i, j: (0, 0)))
        args.append(g_final.reshape(1, d))
    return pl.pallas_call(
        _out_norm_kernel if fuse_norm else _out_kernel,
        grid=(t // tm, d // tn),
        in_specs=in_specs,
        out_specs=pl.BlockSpec((tm, tn), lambda i, j: (i, j)),
        out_shape=jax.ShapeDtypeStruct((t, d), F32),
        compiler_params=_params(("parallel", "arbitrary")),
        name="out_norm" if fuse_norm else "out_proj",
    )(*args)


def kernel(x, c, positions, w_mod, b_mod, g_norm, w_in, g_cq, g_ckv, w_uq, w_ukv,
           w_ret_proj, w_mla_proj, w_out, g_final):
    bsz, seq, d = x.shape
    depth = w_in.shape[0]
    t = bsz * seq
    assert seq % RET_BLOCK == 0 and seq % min(seq, ATT_BLOCK) == 0 and d % IN_TN == 0

    kr_lo = C_CKV + MLA_KV_RANK
    w_in_b = w_in.astype(BF16)
    w_lo = w_in_b
    w_hi = w_in_b[:, :, kr_lo + MLA_ROPE:]
    w_kr = jnp.pad(w_in[:, :, kr_lo:kr_lo + MLA_ROPE], ((0, 0), (0, 0), (0, LANES - MLA_ROPE))).astype(BF16)
    w_uq_p = jnp.pad(
        w_uq.reshape(depth, MLA_Q_RANK, MLA_HEADS, MLA_NOPE + MLA_ROPE),
        ((0, 0), (0, 0), (0, 0), (0, MLA_QK_PAD - MLA_NOPE - MLA_ROPE)),
    ).reshape(depth, MLA_Q_RANK, MLA_HEADS * MLA_QK_PAD).astype(BF16)
    w_ukv4 = w_ukv.reshape(depth, MLA_KV_RANK, MLA_HEADS, MLA_NOPE + MLA_DV)
    w_k = w_ukv4[..., :MLA_NOPE].reshape(depth, MLA_KV_RANK, MLA_HEADS * MLA_NOPE).astype(BF16)
    w_v = w_ukv4[..., MLA_NOPE:].reshape(depth, MLA_KV_RANK, MLA_V_W).astype(BF16)
    w_ret = w_ret_proj.astype(BF16)
    w_mla = w_mla_proj.astype(BF16)
    w_o = w_out.astype(BF16)

    tabs = _rope_tables(positions)
    mod = _modulation(c, w_mod, b_mod)
    mod4 = mod.reshape(depth, mod.shape[1], 1, 3 * d)
    g_norm3 = g_norm.reshape(depth, 1, d)
    g_cq3 = g_cq.reshape(depth, 1, MLA_Q_RANK)
    g_ckv3 = g_ckv.reshape(depth, 1, MLA_KV_RANK)

    x2 = x.reshape(t, d)
    for layer in range(depth):
        p, kr = _in_proj(x2, mod4, g_norm3, w_lo, w_hi, w_kr, tabs, g_cq3, g_ckv3, layer, seq)
        qc, kc, v = _latent_up(p, kr, w_uq_p, w_k, w_v, tabs, layer)
        a = _retention(p, bsz, seq)
        bm = _attention(qc, kc, v, p, bsz, seq)
        merged = _merge(a, bm, p, w_ret, w_mla, layer)
        last = layer == depth - 1
        x2 = _out_proj(merged, w_o, x2, mod4, layer, seq, g_final if last else None)
    return x2.reshape(bsz, seq, d)
```

```python
import jax
import jax.numpy as jnp
from jax import lax
from jax.experimental import pallas as pl
from jax.experimental.pallas import tpu as pltpu

F32 = jnp.float32
BF16 = jnp.bfloat16

CHUNK = 64
EPS = 1e-6
NEG_INF = -1e30
ROPE_BASE = 10000.0
RET_HEADS = 8
RET_DK = 128
RET_DV = 256
RET_QK_W = RET_HEADS * RET_DK
RET_V_W = RET_HEADS * RET_DV
MLA_HEADS = 16
MLA_Q_RANK = 512
MLA_KV_RANK = 512
MLA_NOPE = 128
MLA_ROPE = 64
MLA_DV = 128
MLA_V_W = MLA_HEADS * MLA_DV
MLA_QK_PAD = 256
MLA_V_PAD = 2 * MLA_DV
LOG2E = 1.4426950408889634

LANES = 128
VMEM_LIMIT = 56 * 1024 * 1024

C_RQ = 0
C_RK = C_RQ + RET_QK_W
C_RV = C_RK + RET_QK_W
C_RG = C_RV + RET_V_W
C_CQ = C_RG + RET_V_W
C_CKV = C_CQ + MLA_Q_RANK
C_MG = C_CKV + MLA_KV_RANK
C_BG = C_MG + MLA_V_W
IN_TN = 1024
IN_CHUNK = 256


def _params(sem):
    return pltpu.CompilerParams(dimension_semantics=sem, vmem_limit_bytes=VMEM_LIMIT)


def _sigmoid(x):
    return 0.5 * jnp.tanh(0.5 * x) + 0.5


def _rope_tables_kernel(pos_ref, inv_ref, cr_ref, sr_ref, cm_ref, sm1_ref, sm2_ref):
    pos = pos_ref[...].astype(F32)
    lane = lax.broadcasted_iota(jnp.int32, cr_ref.shape, 1)
    ang = pos * inv_ref[...]
    c = jnp.cos(ang)
    s = jnp.sin(ang)
    ret_half = RET_DK // 2
    half = MLA_ROPE // 2
    c_up, s_up = pltpu.roll(c, LANES - ret_half, 1), pltpu.roll(s, LANES - ret_half, 1)
    c_up2, s_up2 = pltpu.roll(c, LANES - half, 1), pltpu.roll(s, LANES - half, 1)
    cr_ref[...] = jnp.where(lane < ret_half, c, pltpu.roll(c, ret_half, 1))
    sr_ref[...] = jnp.where(lane < ret_half, -s, pltpu.roll(s, ret_half, 1))
    in_lo = lane < half
    in_hi = (lane >= half) & (lane < MLA_ROPE)
    cm_ref[...] = jnp.where(in_lo, c_up, jnp.where(in_hi, c_up2, 0.0))
    sm1_ref[...] = jnp.where(in_lo, -s_up, 0.0)
    sm2_ref[...] = jnp.where(in_hi, s_up2, 0.0)


def _rope_tables(positions):
    t = positions.size
    tm = min(t, 1024)
    inv_r = 1.0 / (ROPE_BASE ** (jnp.arange(0, RET_DK, 2, dtype=F32) / RET_DK))
    inv_m = 1.0 / (ROPE_BASE ** (jnp.arange(0, MLA_ROPE, 2, dtype=F32) / MLA_ROPE))
    inv = jnp.concatenate([inv_r, inv_m, jnp.zeros((LANES - RET_DK // 2 - MLA_ROPE // 2,), F32)])[None, :]
    row = pl.BlockSpec((tm, LANES), lambda i: (i, 0))
    return pl.pallas_call(
        _rope_tables_kernel,
        grid=(t // tm,),
        in_specs=[pl.BlockSpec((tm, 1), lambda i: (i, 0)), pl.BlockSpec((1, LANES), lambda i: (0, 0))],
        out_specs=[row] * 5,
        out_shape=[jax.ShapeDtypeStruct((t, LANES), F32)] * 5,
        compiler_params=_params(("parallel",)),
        name="rope_tables",
    )(positions.reshape(t, 1), inv)


def _mod_kernel(c_ref, w_ref, b_ref, o_ref):
    c = c_ref[...]
    ca = (c * _sigmoid(c)).astype(BF16)
    o_ref[...] = jnp.dot(ca, w_ref[...].astype(BF16), preferred_element_type=F32) + b_ref[...]


def _modulation(c, w_mod, b_mod):
    depth, d, n = w_mod.shape
    bsz = c.shape[0]
    rows = 8
    cp = jnp.pad(c, ((0, rows - bsz), (0, 0)))
    tn = 512
    return pl.pallas_call(
        _mod_kernel,
        grid=(depth, n // tn),
        in_specs=[
            pl.BlockSpec((rows, d), lambda l, j: (0, 0)),
            pl.BlockSpec((None, d, tn), lambda l, j: (l, 0, j)),
            pl.BlockSpec((None, 1, tn), lambda l, j: (l, 0, j)),
        ],
        out_specs=pl.BlockSpec((None, rows, tn), lambda l, j: (l, 0, j)),
        out_shape=jax.ShapeDtypeStruct((depth, rows, n), F32),
        compiler_params=_params(("parallel", "parallel")),
        name="modulation",
    )(cp, w_mod, b_mod.reshape(depth, 1, n))


def _rope_ret(a, cos, sin):
    outs = []
    for hh in range(a.shape[1] // RET_DK):
        t = a[:, hh * RET_DK:(hh + 1) * RET_DK]
        outs.append(t * cos + pltpu.roll(t, RET_DK // 2, 1) * sin)
    return jnp.concatenate(outs, axis=1) if len(outs) > 1 else outs[0]


def _rope_mla(t, cos, sin_lo, sin_hi):
    half = MLA_ROPE // 2
    return t * cos + pltpu.roll(t, LANES - half, 1) * sin_lo + pltpu.roll(t, half, 1) * sin_hi


def _in_proj_kernel(x_ref, shift_ref, scale_ref, g_ref, wlo_ref, whi_ref, wkr_ref, cr_ref, sr_ref,
                    cm_ref, sm1_ref, sm2_ref, gcq_ref, gckv_ref, p_ref, kr_ref, h_scr):
    j = pl.program_id(1)

    @pl.when(j == 0)
    def _():
        x = x_ref[...]
        inv = lax.rsqrt(jnp.mean(x * x, axis=-1, keepdims=True) + EPS)
        h = (x * inv) * g_ref[...]
        h = h * (1.0 + scale_ref[...]) + shift_ref[...]
        hb = h.astype(BF16)
        h_scr[...] = hb
        kr = jnp.dot(hb, wkr_ref[...], preferred_element_type=F32)
        kr_ref[...] = _rope_mla(kr, cm_ref[...], sm1_ref[...], sm2_ref[...]).astype(kr_ref.dtype)

    def tile_range(lo, width):
        return (j >= lo // IN_TN) & (j < (lo + width) // IN_TN)

    def project(w_ref, epilogues, chunk=IN_CHUNK):
        for idx, c0 in enumerate(range(0, IN_TN, chunk)):
            acc = jnp.dot(h_scr[...], w_ref[:, c0:c0 + chunk], preferred_element_type=F32)
            p_ref[:, c0:c0 + chunk] = epilogues[idx % len(epilogues)](acc).astype(p_ref.dtype)

    def rope(acc):
        return _rope_ret(acc, cr_ref[...], sr_ref[...])

    def latent_norm(g_latent_ref):
        def apply(acc):
            inv = lax.rsqrt(jnp.mean(acc * acc, axis=-1, keepdims=True) + EPS)
            return (acc * inv) * g_latent_ref[...]
        return apply

    @pl.when(tile_range(C_RQ, RET_QK_W))
    def _():
        project(wlo_ref, [rope])

    @pl.when(tile_range(C_RK, RET_QK_W))
    def _():
        project(wlo_ref, [lambda acc: rope(acc) * (RET_DK ** -0.5)])

    @pl.when(tile_range(C_RV, RET_V_W))
    def _():
        project(wlo_ref, [lambda acc: acc])

    def silu(acc):
        return acc * _sigmoid(acc)

    @pl.when(tile_range(C_RG, RET_V_W))
    def _():
        project(wlo_ref, [silu])

    @pl.when(tile_range(C_MG, MLA_V_W))
    def _():
        project(whi_ref, [silu])

    @pl.when(tile_range(C_CQ, MLA_Q_RANK + MLA_KV_RANK))
    def _():
        project(wlo_ref, [latent_norm(gcq_ref), latent_norm(gckv_ref)], chunk=MLA_Q_RANK)

    @pl.when(j >= C_BG // IN_TN)
    def _():
        project(whi_ref, [_sigmoid])


def _in_proj(x2, mod4, g_norm3, w_lo, w_hi, w_kr, tabs, g_cq3, g_ckv3, layer, seq):
    t, d = x2.shape
    lo_tiles = C_MG // IN_TN
    n = C_MG + w_hi.shape[2]
    tm = min(seq, 1024)
    per_b = seq // tm
    cr, sr, cm, sm1, sm2 = tabs
    tab = pl.BlockSpec((tm, LANES), lambda i, j: (i, 0))
    return pl.pallas_call(
        _in_proj_kernel,
        grid=(t // tm, n // IN_TN),
        in_specs=[
            pl.BlockSpec((tm, d), lambda i, j: (i, 0)),
            pl.BlockSpec((None, None, 1, d), lambda i, j: (layer, i // per_b, 0, 0)),
            pl.BlockSpec((None, None, 1, d), lambda i, j: (layer, i // per_b, 0, 1)),
            pl.BlockSpec((None, 1, d), lambda i, j: (layer, 0, 0)),
            pl.BlockSpec((None, d, IN_TN), lambda i, j: (layer, 0, jnp.minimum(j, lo_tiles - 1))),
            pl.BlockSpec((None, d, IN_TN), lambda i, j: (layer, 0, jnp.maximum(j - lo_tiles, 0))),
            pl.BlockSpec((None, d, LANES), lambda i, j: (layer, 0, 0)),
            tab, tab, tab, tab, tab,
            pl.BlockSpec((None, 1, MLA_Q_RANK), lambda i, j: (layer, 0, 0)),
            pl.BlockSpec((None, 1, MLA_KV_RANK), lambda i, j: (layer, 0, 0)),
        ],
        out_specs=[
            pl.BlockSpec((tm, IN_TN), lambda i, j: (i, j)),
            pl.BlockSpec((tm, LANES), lambda i, j: (i, 0)),
        ],
        out_shape=[
            jax.ShapeDtypeStruct((t, n), BF16),
            jax.ShapeDtypeStruct((t, LANES), BF16),
        ],
        scratch_shapes=[pltpu.VMEM((tm, d), BF16)],
        compiler_params=_params(("parallel", "arbitrary")),
        name="in_proj",
    )(x2, mod4, mod4, g_norm3, w_lo, w_hi, w_kr, cr, sr, cm, sm1, sm2, g_cq3, g_ckv3)


UP_HEADS = 4


def _latent_up_kernel(cq_ref, ckv_ref, wq_ref, wk_ref, wv_ref, kr_ref, cm_ref, sm1_ref, sm2_ref,
                      q_ref, k_ref, v_ref):
    scale = (MLA_NOPE + MLA_ROPE) ** -0.5 * LOG2E
    cq = cq_ref[...]
    ckv = ckv_ref[...]
    cm, s1, s2 = cm_ref[...], sm1_ref[...], sm2_ref[...]
    kr = kr_ref[...]
    for hh in range(UP_HEADS):
        lo = hh * MLA_QK_PAD
        acc = jnp.dot(cq, wq_ref[:, lo:lo + MLA_QK_PAD], preferred_element_type=F32) * scale
        q_ref[:, lo:lo + MLA_NOPE] = acc[:, :MLA_NOPE].astype(q_ref.dtype)
        q_ref[:, lo + MLA_NOPE:lo + MLA_QK_PAD] = _rope_mla(acc[:, MLA_NOPE:], cm, s1, s2).astype(q_ref.dtype)
    kn = jnp.dot(ckv, wk_ref[...], preferred_element_type=F32)
    vv = jnp.dot(ckv, wv_ref[...], preferred_element_type=F32)
    ones = jnp.ones((cq.shape[0], MLA_V_PAD - MLA_DV), v_ref.dtype)
    for hh in range(UP_HEADS):
        lo = hh * MLA_QK_PAD
        k_ref[:, lo:lo + MLA_NOPE] = kn[:, hh * MLA_NOPE:(hh + 1) * MLA_NOPE].astype(k_ref.dtype)
        k_ref[:, lo + MLA_NOPE:lo + MLA_QK_PAD] = kr
        vlo = hh * MLA_V_PAD
        v_ref[:, vlo:vlo + MLA_DV] = vv[:, hh * MLA_DV:(hh + 1) * MLA_DV].astype(v_ref.dtype)
        v_ref[:, vlo + MLA_DV:vlo + MLA_V_PAD] = ones


def _latent_up(p, kr, w_uq_p, w_k, w_v, tabs, layer):
    t = p.shape[0]
    tm = min(t, 1024)
    _, _, cm, sm1, sm2 = tabs
    tab = pl.BlockSpec((tm, LANES), lambda i, j: (i, 0))

    def head_cols(width):
        return pl.BlockSpec((tm, UP_HEADS * width), lambda i, j: (i, j))

    return pl.pallas_call(
        _latent_up_kernel,
        grid=(t // tm, MLA_HEADS // UP_HEADS),
        in_specs=[
            pl.BlockSpec((tm, MLA_Q_RANK), lambda i, j: (i, C_CQ // MLA_Q_RANK)),
            pl.BlockSpec((tm, MLA_KV_RANK), lambda i, j: (i, C_CKV // MLA_KV_RANK)),
            pl.BlockSpec((None, MLA_Q_RANK, UP_HEADS * MLA_QK_PAD), lambda i, j: (layer, 0, j)),
            pl.BlockSpec((None, MLA_KV_RANK, UP_HEADS * MLA_NOPE), lambda i, j: (layer, 0, j)),
            pl.BlockSpec((None, MLA_KV_RANK, UP_HEADS * MLA_DV), lambda i, j: (layer, 0, j)),
            tab, tab, tab, tab,
        ],
        out_specs=[head_cols(MLA_QK_PAD), head_cols(MLA_QK_PAD), head_cols(MLA_V_PAD)],
        out_shape=[
            jax.ShapeDtypeStruct((t, MLA_HEADS * MLA_QK_PAD), BF16),
            jax.ShapeDtypeStruct((t, MLA_HEADS * MLA_QK_PAD), BF16),
            jax.ShapeDtypeStruct((t, MLA_HEADS * MLA_V_PAD), BF16),
        ],
        compiler_params=_params(("parallel", "arbitrary")),
        name="latent_up",
    )(p, p, w_uq_p, w_k, w_v, kr, cm, sm1, sm2)


RET_BLOCK = 256


RET_GROUP = 2


def _retention_kernel(q_ref, k_ref, v_ref, gate_ref, lg_ref, o_ref, state, dmat, xi, zeta):
    blk = dmat.shape[1]
    nblk = q_ref.shape[0] // blk

    state[...] = jnp.zeros_like(state)
    for hh in range(RET_GROUP):
        lg = lg_ref[hh]
        r = lax.broadcasted_iota(jnp.int32, (blk, blk), 0)
        c = lax.broadcasted_iota(jnp.int32, (blk, blk), 1)
        decay = jnp.exp(jnp.abs(r - c).astype(F32) * lg)
        dmat[hh] = jnp.where((c // CHUNK) <= (r // CHUNK), decay, 0.0)
        rx = lax.broadcasted_iota(jnp.int32, xi.shape[1:], 0).astype(F32)
        xi[hh] = jnp.exp((rx + 1.0) * lg[:, :RET_DV])
        rz = lax.broadcasted_iota(jnp.int32, zeta.shape[1:], 0).astype(F32)
        zeta[hh] = jnp.exp((blk - 1.0 - rz) * lg[:, :RET_DK])

    def body(j, carry):
        rows = pl.ds(pl.multiple_of(j * blk, blk), blk)
        for hh in range(RET_GROUP):
            qk_cols = slice(hh * RET_DK, (hh + 1) * RET_DK)
            v_cols = slice(hh * RET_DV, (hh + 1) * RET_DV)
            q = q_ref[rows, qk_cols]
            k = k_ref[rows, qk_cols]
            v = v_ref[rows, v_cols]
            s = lax.dot_general(q, k, (((1,), (1,)), ((), ())), preferred_element_type=F32)
            o = jnp.dot((s * dmat[hh]).astype(BF16), v, preferred_element_type=F32)
            st = state[hh]
            o = o + xi[hh] * jnp.dot(q, st.astype(BF16), preferred_element_type=F32)
            kz_t = (k.astype(F32) * zeta[hh]).T.astype(BF16)
            block_decay = jnp.exp(blk * lg_ref[hh][:, :RET_DV])
            state[hh] = st * block_decay + jnp.dot(kz_t, v, preferred_element_type=F32)

            mu = jnp.mean(o, axis=-1, keepdims=True)
            dlt = o - mu
            var = jnp.mean(dlt * dlt, axis=-1, keepdims=True)
            y = dlt * lax.rsqrt(var + EPS)
            o_ref[rows, v_cols] = (y * gate_ref[rows, v_cols].astype(F32)).astype(o_ref.dtype)
        return carry

    lax.fori_loop(0, nblk, body, 0, unroll=4)


def _retention(p, bsz, seq):
    t = p.shape[0]
    blk = RET_BLOCK
    g = RET_GROUP
    log_gamma = jnp.log(1.0 - 2.0 ** (-5.0 - jnp.arange(RET_HEADS, dtype=F32)))
    lg = jnp.broadcast_to(log_gamma[:, None, None], (RET_HEADS, 1, blk))
    return pl.pallas_call(
        _retention_kernel,
        grid=(bsz, RET_HEADS // g),
        in_specs=[
            pl.BlockSpec((seq, g * RET_DK), lambda b, h: (b, C_RQ // (g * RET_DK) + h)),
            pl.BlockSpec((seq, g * RET_DK), lambda b, h: (b, C_RK // (g * RET_DK) + h)),
            pl.BlockSpec((seq, g * RET_DV), lambda b, h: (b, C_RV // (g * RET_DV) + h)),
            pl.BlockSpec((seq, g * RET_DV), lambda b, h: (b, C_RG // (g * RET_DV) + h)),
            pl.BlockSpec((g, 1, blk), lambda b, h: (h, 0, 0)),
        ],
        out_specs=pl.BlockSpec((seq, g * RET_DV), lambda b, h: (b, h)),
        out_shape=jax.ShapeDtypeStruct((t, RET_V_W), BF16),
        scratch_shapes=[
            pltpu.VMEM((g, RET_DK, RET_DV), F32),
            pltpu.VMEM((g, blk, blk), F32),
            pltpu.VMEM((g, blk, RET_DV), F32),
            pltpu.VMEM((g, blk, RET_DK), F32),
        ],
        compiler_params=_params(("parallel", "parallel")),
        name="retention",
    )(p, p, p, p, lg)


ATT_BLOCK = 512
ATT_HEADS = 2
ATT_REGION_PAIRS = 11


def _attention_kernel(q_ref, k_ref, v_ref, gate_ref, bias_ref, o_ref, s_scr, m_scr, acc_scr):
    seq = q_ref.shape[0]
    tq = s_scr.shape[1]
    nq = seq // tq

    def windows(qi, kb):
        if kb == qi and tq % (2 * CHUNK) == 0:
            return ((0, tq // 2, tq // 2), (tq // 2, tq, tq))
        return ((0, tq, tq),)

    def slot(next_pair, pair):
        for hh in range(ATT_HEADS):
            qk_cols = slice(hh * MLA_QK_PAD, (hh + 1) * MLA_QK_PAD)
            v_cols = slice(hh * MLA_V_PAD, (hh + 1) * MLA_V_PAD)
            m_cols = slice(hh * LANES, (hh + 1) * LANES)
            if pair is not None:
                qi, kb = pair
                for r0, r1, nk in windows(qi, kb):
                    s = s_scr[hh, r0:r1, :nk]
                    if kb == qi:
                        s = s + bias_ref[r0:r1, :nk]
                    m_cur = jnp.max(s, axis=-1, keepdims=True)
                    if kb == 0:
                        m_new = jnp.broadcast_to(m_cur, (r1 - r0, LANES))
                    else:
                        m_old = m_scr[r0:r1, m_cols]
                        m_new = jnp.maximum(m_old, m_cur)
                    p = jnp.exp2(s - jnp.concatenate([m_new] * (nk // LANES), axis=1))
                    v = v_ref[kb * tq:kb * tq + nk, v_cols]
                    pv = jnp.dot(p.astype(BF16), v, preferred_element_type=F32)
                    if kb == 0:
                        acc = pv
                    else:
                        alpha = jnp.concatenate([jnp.exp2(m_old - m_new)] * (MLA_V_PAD // LANES), axis=1)
                        acc = acc_scr[r0:r1, v_cols] * alpha + pv
                    if kb == qi:
                        out_rows = slice(qi * tq + r0, qi * tq + r1)
                        cols = slice(hh * MLA_DV, (hh + 1) * MLA_DV)
                        o = acc[:, :MLA_DV] / acc[:, MLA_DV:]
                        o_ref[out_rows, cols] = (o * gate_ref[out_rows, cols].astype(F32)).astype(o_ref.dtype)
                    else:
                        acc_scr[r0:r1, v_cols] = acc
                        m_scr[r0:r1, m_cols] = m_new
            if next_pair is not None:
                nqi, nkb = next_pair
                for r0, r1, nk in windows(nqi, nkb):
                    q = q_ref[nqi * tq + r0:nqi * tq + r1, qk_cols]
                    k = k_ref[nkb * tq:nkb * tq + nk, qk_cols]
                    s_scr[hh, r0:r1, :nk] = lax.dot_general(
                        q, k, (((1,), (1,)), ((), ())), preferred_element_type=F32)

    pairs = [(qi, kb) for qi in range(nq) for kb in range(qi + 1)]
    regions, current = [], []
    for qi in range(nq):
        if current and len(current) + qi + 1 > ATT_REGION_PAIRS:
            regions.append(current)
            current = []
        current += [(qi, kb) for kb in range(qi + 1)]
    regions.append(current)

    for r, region in enumerate(regions):
        @pl.when(pl.program_id(0) + r >= 0)
        def _(r=r, region=region):
            if r == 0:
                slot(pairs[0], None)
            for pair in region:
                n = pairs.index(pair)
                slot(pairs[n + 1] if n + 1 < len(pairs) else None, pair)


def _attention(qc, kc, v, p, bsz, seq):
    t = qc.shape[0]
    tq = min(seq, ATT_BLOCK)
    nh = ATT_HEADS
    r = lax.broadcasted_iota(jnp.int32, (tq, tq), 0) // CHUNK
    c = lax.broadcasted_iota(jnp.int32, (tq, tq), 1) // CHUNK
    bias = jnp.where(c <= r, 0.0, NEG_INF * LOG2E).astype(F32)
    return pl.pallas_call(
        _attention_kernel,
        grid=(bsz, MLA_HEADS // nh),
        in_specs=[
            pl.BlockSpec((seq, nh * MLA_QK_PAD), lambda b, h: (b, h)),
            pl.BlockSpec((seq, nh * MLA_QK_PAD), lambda b, h: (b, h)),
            pl.BlockSpec((seq, nh * MLA_V_PAD), lambda b, h: (b, h)),
            pl.BlockSpec((seq, nh * MLA_DV), lambda b, h: (b, C_MG // (nh * MLA_DV) + h)),
            pl.BlockSpec((tq, tq), lambda b, h: (0, 0)),
        ],
        out_specs=pl.BlockSpec((seq, nh * MLA_DV), lambda b, h: (b, h)),
        out_shape=jax.ShapeDtypeStruct((t, MLA_V_W), BF16),
        scratch_shapes=[
            pltpu.VMEM((nh, tq, tq), F32),
            pltpu.VMEM((tq, nh * LANES), F32),
            pltpu.VMEM((tq, nh * MLA_V_PAD), F32),
        ],
        compiler_params=_params(("parallel", "parallel")),
        name="attention",
    )(qc, kc, v, p, bias)


PROJ_CHUNK = 256


def _merge_kernel(a_ref, b_ref, wr_ref, wm_ref, ga_ref, gb_ref, o_ref):
    for c0 in range(0, o_ref.shape[1], PROJ_CHUNK):
        cols = slice(c0, c0 + PROJ_CHUNK)
        y_ret = jnp.dot(a_ref[...], wr_ref[:, cols], preferred_element_type=F32)
        y_mla = jnp.dot(b_ref[...], wm_ref[:, cols], preferred_element_type=F32)
        merged = ga_ref[:, cols].astype(F32) * y_ret + gb_ref[:, cols].astype(F32) * y_mla
        o_ref[:, cols] = merged.astype(o_ref.dtype)


def _merge(a, bm, p, w_ret, w_mla, layer):
    t, d_in = a.shape
    d = w_ret.shape[2]
    tm = min(t, 1024)
    tn = 512
    return pl.pallas_call(
        _merge_kernel,
        grid=(t // tm, d // tn),
        in_specs=[
            pl.BlockSpec((tm, d_in), lambda i, j: (i, 0)),
            pl.BlockSpec((tm, d_in), lambda i, j: (i, 0)),
            pl.BlockSpec((None, d_in, tn), lambda i, j: (layer, 0, j)),
            pl.BlockSpec((None, d_in, tn), lambda i, j: (layer, 0, j)),
            pl.BlockSpec((tm, tn), lambda i, j: (i, C_BG // tn + j)),
            pl.BlockSpec((tm, tn), lambda i, j: (i, (C_BG + d) // tn + j)),
        ],
        out_specs=pl.BlockSpec((tm, tn), lambda i, j: (i, j)),
        out_shape=jax.ShapeDtypeStruct((t, d), BF16),
        compiler_params=_params(("parallel", "arbitrary")),
        name="merge_proj",
    )(a, bm, w_ret, w_mla, p, p)


def _out_kernel(m_ref, w_ref, x_ref, gate_ref, o_ref):
    for c0 in range(0, o_ref.shape[1], PROJ_CHUNK):
        cols = slice(c0, c0 + PROJ_CHUNK)
        out = jnp.dot(m_ref[...], w_ref[:, cols], preferred_element_type=F32)
        o_ref[:, cols] = x_ref[:, cols] + gate_ref[:, cols] * out


def _out_norm_kernel(m_ref, w_ref, x_ref, gate_ref, g_ref, o_ref):
    _out_kernel(m_ref, w_ref, x_ref, gate_ref, o_ref)
    y = o_ref[...]
    inv = lax.rsqrt(jnp.mean(y * y, axis=-1, keepdims=True) + EPS)
    o_ref[...] = (y * inv) * g_ref[...]


def _out_proj(merged, w_out, x2, mod4, layer, seq, g_final=None):
    t, d = x2.shape
    fuse_norm = g_final is not None
    tm = min(seq, 512 if fuse_norm else 1024)
    per_b = seq // tm
    tn = d if fuse_norm else 1024
    gate_blk = 2 * d // tn
    in_specs = [
        pl.BlockSpec((tm, d), lambda i, j: (i, 0)),
        pl.BlockSpec((None, d, tn), lambda i, j: (layer, 0, j)),
        pl.BlockSpec((tm, tn), lambda i, j: (i, j)),
        pl.BlockSpec((None, None, 1, tn), lambda i, j: (layer, i // per_b, 0, gate_blk + j)),
    ]
    args = [merged, w_out, x2, mod4]
    if fuse_norm:
        in_specs.append(pl.BlockSpec((1, d), lambda i, j: (0, 0)))
        args.append(g_final.reshape(1, d))
    return pl.pallas_call(
        _out_norm_kernel if fuse_norm else _out_kernel,
        grid=(t // tm, d // tn),
        in_specs=in_specs,
        out_specs=pl.BlockSpec((tm, tn), lambda i, j: (i, j)),
        out_shape=jax.ShapeDtypeStruct((t, d), F32),
        compiler_params=_params(("parallel", "arbitrary")),
        name="out_norm" if fuse_norm else "out_proj",
    )(*args)


def kernel(x, c, positions, w_mod, b_mod, g_norm, w_in, g_cq, g_ckv, w_uq, w_ukv,
           w_ret_proj, w_mla_proj, w_out, g_final):
    bsz, seq, d = x.shape
    depth = w_in.shape[0]
    t = bsz * seq
    assert seq % RET_BLOCK == 0 and seq % min(seq, ATT_BLOCK) == 0 and d % IN_TN == 0

    kr_lo = C_CKV + MLA_KV_RANK
    w_in_b = w_in.astype(BF16)
    w_lo = w_in_b
    w_hi = w_in_b[:, :, kr_lo + MLA_ROPE:]
    w_kr = jnp.pad(w_in[:, :, kr_lo:kr_lo + MLA_ROPE], ((0, 0), (0, 0), (0, LANES - MLA_ROPE))).astype(BF16)
    w_uq_p = jnp.pad(
        w_uq.reshape(depth, MLA_Q_RANK, MLA_HEADS, MLA_NOPE + MLA_ROPE),
        ((0, 0), (0, 0), (0, 0), (0, MLA_QK_PAD - MLA_NOPE - MLA_ROPE)),
    ).reshape(depth, MLA_Q_RANK, MLA_HEADS * MLA_QK_PAD).astype(BF16)
    w_ukv4 = w_ukv.reshape(depth, MLA_KV_RANK, MLA_HEADS, MLA_NOPE + MLA_DV)
    w_k = w_ukv4[..., :MLA_NOPE].reshape(depth, MLA_KV_RANK, MLA_HEADS * MLA_NOPE).astype(BF16)
    w_v = w_ukv4[..., MLA_NOPE:].reshape(depth, MLA_KV_RANK, MLA_V_W).astype(BF16)
    w_ret = w_ret_proj.astype(BF16)
    w_mla = w_mla_proj.astype(BF16)
    w_o = w_out.astype(BF16)

    tabs = _rope_tables(positions)
    mod = _modulation(c, w_mod, b_mod)
    mod4 = mod.reshape(depth, mod.shape[1], 1, 3 * d)
    g_norm3 = g_norm.reshape(depth, 1, d)
    g_cq3 = g_cq.reshape(depth, 1, MLA_Q_RANK)
    g_ckv3 = g_ckv.reshape(depth, 1, MLA_KV_RANK)

    x2 = x.reshape(t, d)
    for layer in range(depth):
        p, kr = _in_proj(x2, mod4, g_norm3, w_lo, w_hi, w_kr, tabs, g_cq3, g_ckv3, layer, seq)
        qc, kc, v = _latent_up(p, kr, w_uq_p, w_k, w_v, tabs, layer)
        a = _retention(p, bsz, seq)
        bm = _attention(qc, kc, v, p, bsz, seq)
        merged = _merge(a, bm, p, w_ret, w_mla, layer)
        last = layer == depth - 1
        x2 = _out_proj(merged, w_o, x2, mod4, layer, seq, g_final if last else None)
    return x2.reshape(bsz, seq, d)
```

```python
import jax
import jax.numpy as jnp
from jax import lax
from jax.experimental import pallas as pl
from jax.experimental.pallas import tpu as pltpu

F32 = jnp.float32
BF16 = jnp.bfloat16

CHUNK = 64
EPS = 1e-6
NEG_INF = -1e30
ROPE_BASE = 10000.0
RET_HEADS = 8
RET_DK = 128
RET_DV = 256
RET_QK_W = RET_HEADS * RET_DK
RET_V_W = RET_HEADS * RET_DV
MLA_HEADS = 16
MLA_Q_RANK = 512
MLA_KV_RANK = 512
MLA_NOPE = 128
MLA_ROPE = 64
MLA_DV = 128
MLA_V_W = MLA_HEADS * MLA_DV
MLA_QK_PAD = 256
MLA_V_PAD = 2 * MLA_DV
LOG2E = 1.4426950408889634

LANES = 128
VMEM_LIMIT = 56 * 1024 * 1024

C_RQ = 0
C_RK = C_RQ + RET_QK_W
C_RV = C_RK + RET_QK_W
C_RG = C_RV + RET_V_W
C_CQ = C_RG + RET_V_W
C_CKV = C_CQ + MLA_Q_RANK
C_MG = C_CKV + MLA_KV_RANK
C_BG = C_MG + MLA_V_W
IN_TN = 1024
IN_CHUNK = 256


def _params(sem):
    return pltpu.CompilerParams(dimension_semantics=sem, vmem_limit_bytes=VMEM_LIMIT)


def _sigmoid(x):
    return 0.5 * jnp.tanh(0.5 * x) + 0.5


def _rope_tables_kernel(pos_ref, inv_ref, cr_ref, sr_ref, cm_ref, sm1_ref, sm2_ref):
    pos = pos_ref[...].astype(F32)
    lane = lax.broadcasted_iota(jnp.int32, cr_ref.shape, 1)
    ang = pos * inv_ref[...]
    c = jnp.cos(ang)
    s = jnp.sin(ang)
    ret_half = RET_DK // 2
    half = MLA_ROPE // 2
    c_up, s_up = pltpu.roll(c, LANES - ret_half, 1), pltpu.roll(s, LANES - ret_half, 1)
    c_up2, s_up2 = pltpu.roll(c, LANES - half, 1), pltpu.roll(s, LANES - half, 1)
    cr_ref[...] = jnp.where(lane < ret_half, c, pltpu.roll(c, ret_half, 1))
    sr_ref[...] = jnp.where(lane < ret_half, -s, pltpu.roll(s, ret_half, 1))
    in_lo = lane < half
    in_hi = (lane >= half) & (lane < MLA_ROPE)
    cm_ref[...] = jnp.where(in_lo, c_up, jnp.where(in_hi, c_up2, 0.0))
    sm1_ref[...] = jnp.where(in_lo, -s_up, 0.0)
    sm2_ref[...] = jnp.where(in_hi, s_up2, 0.0)


def _rope_tables(positions):
    t = positions.size
    tm = min(t, 1024)
    inv_r = 1.0 / (ROPE_BASE ** (jnp.arange(0, RET_DK, 2, dtype=F32) / RET_DK))
    inv_m = 1.0 / (ROPE_BASE ** (jnp.arange(0, MLA_ROPE, 2, dtype=F32) / MLA_ROPE))
    inv = jnp.concatenate([inv_r, inv_m, jnp.zeros((LANES - RET_DK // 2 - MLA_ROPE // 2,), F32)])[None, :]
    row = pl.BlockSpec((tm, LANES), lambda i: (i, 0))
    return pl.pallas_call(
        _rope_tables_kernel,
        grid=(t // tm,),
        in_specs=[pl.BlockSpec((tm, 1), lambda i: (i, 0)), pl.BlockSpec((1, LANES), lambda i: (0, 0))],
        out_specs=[row] * 5,
        out_shape=[jax.ShapeDtypeStruct((t, LANES), F32)] * 5,
        compiler_params=_params(("parallel",)),
        name="rope_tables",
    )(positions.reshape(t, 1), inv)


def _mod_kernel(c_ref, w_ref, b_ref, o_ref):
    c = c_ref[...]
    ca = (c * _sigmoid(c)).astype(BF16)
    o_ref[...] = jnp.dot(ca, w_ref[...].astype(BF16), preferred_element_type=F32) + b_ref[...]


def _modulation(c, w_mod, b_mod):
    depth, d, n = w_mod.shape
    bsz = c.shape[0]
    rows = 8
    cp = jnp.pad(c, ((0, rows - bsz), (0, 0)))
    tn = 512
    return pl.pallas_call(
        _mod_kernel,
        grid=(depth, n // tn),
        in_specs=[
            pl.BlockSpec((rows, d), lambda l, j: (0, 0)),
            pl.BlockSpec((None, d, tn), lambda l, j: (l, 0, j)),
            pl.BlockSpec((None, 1, tn), lambda l, j: (l, 0, j)),
        ],
        out_specs=pl.BlockSpec((None, rows, tn), lambda l, j: (l, 0, j)),
        out_shape=jax.ShapeDtypeStruct((depth, rows, n), F32),
        compiler_params=_params(("parallel", "parallel")),
        name="modulation",
    )(cp, w_mod, b_mod.reshape(depth, 1, n))


def _rope_ret(a, cos, sin):
    outs = []
    for hh in range(a.shape[1] // RET_DK):
        t = a[:, hh * RET_DK:(hh + 1) * RET_DK]
        outs.append(t * cos + pltpu.roll(t, RET_DK // 2, 1) * sin)
    return jnp.concatenate(outs, axis=1) if len(outs) > 1 else outs[0]


def _rope_mla(t, cos, sin_lo, sin_hi):
    half = MLA_ROPE // 2
    return t * cos + pltpu.roll(t, LANES - half, 1) * sin_lo + pltpu.roll(t, half, 1) * sin_hi


def _in_proj_kernel(x_ref, shift_ref, scale_ref, g_ref, wlo_ref, whi_ref, wkr_ref, cr_ref, sr_ref,
                    cm_ref, sm1_ref, sm2_ref, gcq_ref, gckv_ref, p_ref, kr_ref, h_scr):
    j = pl.program_id(1)

    @pl.when(j == 0)
    def _():
        x = x_ref[...]
        inv = lax.rsqrt(jnp.mean(x * x, axis=-1, keepdims=True) + EPS)
        h = (x * inv) * g_ref[...]
        h = h * (1.0 + scale_ref[...]) + shift_ref[...]
        hb = h.astype(BF16)
        h_scr[...] = hb
        kr = jnp.dot(hb, wkr_ref[...], preferred_element_type=F32)
        kr_ref[...] = _rope_mla(kr, cm_ref[...], sm1_ref[...], sm2_ref[...]).astype(kr_ref.dtype)

    def tile_range(lo, width):
        return (j >= lo // IN_TN) & (j < (lo + width) // IN_TN)

    def project(w_ref, epilogues, chunk=IN_CHUNK):
        for idx, c0 in enumerate(range(0, IN_TN, chunk)):
            acc = jnp.dot(h_scr[...], w_ref[:, c0:c0 + chunk], preferred_element_type=F32)
            p_ref[:, c0:c0 + chunk] = epilogues[idx % len(epilogues)](acc).astype(p_ref.dtype)

    def rope(acc):
        return _rope_ret(acc, cr_ref[...], sr_ref[...])

    def latent_norm(g_latent_ref):
        def apply(acc):
            inv = lax.rsqrt(jnp.mean(acc * acc, axis=-1, keepdims=True) + EPS)
            return (acc * inv) * g_latent_ref[...]
        return apply

    @pl.when(tile_range(C_RQ, RET_QK_W))
    def _():
        project(wlo_ref, [rope])

    @pl.when(tile_range(C_RK, RET_QK_W))
    def _():
        project(wlo_ref, [lambda acc: rope(acc) * (RET_DK ** -0.5)])

    @pl.when(tile_range(C_RV, RET_V_W))
    def _():
        project(wlo_ref, [lambda acc: acc])

    def silu(acc):
        return acc * _sigmoid(acc)

    @pl.when(tile_range(C_RG, RET_V_W))
    def _():
        project(wlo_ref, [silu])

    @pl.when(tile_range(C_MG, MLA_V_W))
    def _():
        project(whi_ref, [silu])

    @pl.when(tile_range(C_CQ, MLA_Q_RANK + MLA_KV_RANK))
    def _():
        project(wlo_ref, [latent_norm(gcq_ref), latent_norm(gckv_ref)], chunk=MLA_Q_RANK)

    @pl.when(j >= C_BG // IN_TN)
    def _():
        project(whi_ref, [_sigmoid])


def _in_proj(x2, mod4, g_norm3, w_lo, w_hi, w_kr, tabs, g_cq3, g_ckv3, layer, seq):
    t, d = x2.shape
    lo_tiles = C_MG // IN_TN
    n = C_MG + w_hi.shape[2]
    tm = min(seq, 1024)
    per_b = seq // tm
    cr, sr, cm, sm1, sm2 = tabs
    tab = pl.BlockSpec((tm, LANES), lambda i, j: (i, 0))
    return pl.pallas_call(
        _in_proj_kernel,
        grid=(t // tm, n // IN_TN),
        in_specs=[
            pl.BlockSpec((tm, d), lambda i, j: (i, 0)),
            pl.BlockSpec((None, None, 1, d), lambda i, j: (layer, i // per_b, 0, 0)),
            pl.BlockSpec((None, None, 1, d), lambda i, j: (layer, i // per_b, 0, 1)),
            pl.BlockSpec((None, 1, d), lambda i, j: (layer, 0, 0)),
            pl.BlockSpec((None, d, IN_TN), lambda i, j: (layer, 0, jnp.minimum(j, lo_tiles - 1))),
            pl.BlockSpec((None, d, IN_TN), lambda i, j: (layer, 0, jnp.maximum(j - lo_tiles, 0))),
            pl.BlockSpec((None, d, LANES), lambda i, j: (layer, 0, 0)),
            tab, tab, tab, tab, tab,
            pl.BlockSpec((None, 1, MLA_Q_RANK), lambda i, j: (layer, 0, 0)),
            pl.BlockSpec((None, 1, MLA_KV_RANK), lambda i, j: (layer, 0, 0)),
        ],
        out_specs=[
            pl.BlockSpec((tm, IN_TN), lambda i, j: (i, j)),
            pl.BlockSpec((tm, LANES), lambda i, j: (i, 0)),
        ],
        out_shape=[
            jax.ShapeDtypeStruct((t, n), BF16),
            jax.ShapeDtypeStruct((t, LANES), BF16),
        ],
        scratch_shapes=[pltpu.VMEM((tm, d), BF16)],
        compiler_params=_params(("parallel", "arbitrary")),
        name="in_proj",
    )(x2, mod4, mod4, g_norm3, w_lo, w_hi, w_kr, cr, sr, cm, sm1, sm2, g_cq3, g_ckv3)


UP_HEADS = MLA_HEADS


def _latent_up_kernel(cq_ref, ckv_ref, wq_ref, wk_ref, wv_ref, kr_ref, cm_ref, sm1_ref, sm2_ref,
                      q_ref, k_ref, v_ref):
    scale = (MLA_NOPE + MLA_ROPE) ** -0.5 * LOG2E
    cq = cq_ref[...]
    ckv = ckv_ref[...]
    cm, s1, s2 = cm_ref[...], sm1_ref[...], sm2_ref[...]
    kr = kr_ref[...]
    for hh in range(UP_HEADS):
        lo = hh * MLA_QK_PAD
        acc = jnp.dot(cq, wq_ref[:, lo:lo + MLA_QK_PAD], preferred_element_type=F32) * scale
        q_ref[:, lo:lo + MLA_NOPE] = acc[:, :MLA_NOPE].astype(q_ref.dtype)
        q_ref[:, lo + MLA_NOPE:lo + MLA_QK_PAD] = _rope_mla(acc[:, MLA_NOPE:], cm, s1, s2).astype(q_ref.dtype)
    ones = jnp.ones((cq.shape[0], MLA_V_PAD - MLA_DV), v_ref.dtype)
    pair_w = 2 * MLA_NOPE
    for h0 in range(0, UP_HEADS, 2):
        src = slice(h0 * MLA_NOPE, h0 * MLA_NOPE + pair_w)
        kn = jnp.dot(ckv, wk_ref[:, src], preferred_element_type=F32)
        vv = jnp.dot(ckv, wv_ref[:, src], preferred_element_type=F32)
        for hh in range(2):
            lo = (h0 + hh) * MLA_QK_PAD
            k_ref[:, lo:lo + MLA_NOPE] = kn[:, hh * MLA_NOPE:(hh + 1) * MLA_NOPE].astype(k_ref.dtype)
            k_ref[:, lo + MLA_NOPE:lo + MLA_QK_PAD] = kr
            vlo = (h0 + hh) * MLA_V_PAD
            v_ref[:, vlo:vlo + MLA_DV] = vv[:, hh * MLA_DV:(hh + 1) * MLA_DV].astype(v_ref.dtype)
            v_ref[:, vlo + MLA_DV:vlo + MLA_V_PAD] = ones


def _latent_up(p, kr, w_uq_p, w_k, w_v, tabs, layer):
    t = p.shape[0]
    tm = min(t, 512)
    _, _, cm, sm1, sm2 = tabs
    tab = pl.BlockSpec((tm, LANES), lambda i, j: (i, 0))

    def head_cols(width):
        return pl.BlockSpec((tm, UP_HEADS * width), lambda i, j: (i, j))

    return pl.pallas_call(
        _latent_up_kernel,
        grid=(t // tm, MLA_HEADS // UP_HEADS),
        in_specs=[
            pl.BlockSpec((tm, MLA_Q_RANK), lambda i, j: (i, C_CQ // MLA_Q_RANK)),
            pl.BlockSpec((tm, MLA_KV_RANK), lambda i, j: (i, C_CKV // MLA_KV_RANK)),
            pl.BlockSpec((None, MLA_Q_RANK, UP_HEADS * MLA_QK_PAD), lambda i, j: (layer, 0, j)),
            pl.BlockSpec((None, MLA_KV_RANK, UP_HEADS * MLA_NOPE), lambda i, j: (layer, 0, j)),
            pl.BlockSpec((None, MLA_KV_RANK, UP_HEADS * MLA_DV), lambda i, j: (layer, 0, j)),
            tab, tab, tab, tab,
        ],
        out_specs=[head_cols(MLA_QK_PAD), head_cols(MLA_QK_PAD), head_cols(MLA_V_PAD)],
        out_shape=[
            jax.ShapeDtypeStruct((t, MLA_HEADS * MLA_QK_PAD), BF16),
            jax.ShapeDtypeStruct((t, MLA_HEADS * MLA_QK_PAD), BF16),
            jax.ShapeDtypeStruct((t, MLA_HEADS * MLA_V_PAD), BF16),
        ],
        compiler_params=_params(("parallel", "arbitrary")),
        name="latent_up",
    )(p, p, w_uq_p, w_k, w_v, kr, cm, sm1, sm2)


RET_BLOCK = 256


RET_GROUP = 2


def _retention_kernel(q_ref, k_ref, v_ref, gate_ref, lg_ref, o_ref, state, dmat, xi, zeta):
    blk = dmat.shape[1]
    nblk = q_ref.shape[0] // blk

    state[...] = jnp.zeros_like(state)
    for hh in range(RET_GROUP):
        lg = lg_ref[hh]
        r = lax.broadcasted_iota(jnp.int32, (blk, blk), 0)
        c = lax.broadcasted_iota(jnp.int32, (blk, blk), 1)
        decay = jnp.exp(jnp.abs(r - c).astype(F32) * lg)
        dmat[hh] = jnp.where((c // CHUNK) <= (r // CHUNK), decay, 0.0)
        rx = lax.broadcasted_iota(jnp.int32, xi.shape[1:], 0).astype(F32)
        xi[hh] = jnp.exp((rx + 1.0) * lg[:, :RET_DV])
        rz = lax.broadcasted_iota(jnp.int32, zeta.shape[1:], 0).astype(F32)
        zeta[hh] = jnp.exp((blk - 1.0 - rz) * lg[:, :RET_DK])

    def body(j, carry):
        rows = pl.ds(pl.multiple_of(j * blk, blk), blk)
        for hh in range(RET_GROUP):
            qk_cols = slice(hh * RET_DK, (hh + 1) * RET_DK)
            v_cols = slice(hh * RET_DV, (hh + 1) * RET_DV)
            q = q_ref[rows, qk_cols]
            k = k_ref[rows, qk_cols]
            v = v_ref[rows, v_cols]
            s = lax.dot_general(q, k, (((1,), (1,)), ((), ())), preferred_element_type=F32)
            o = jnp.dot((s * dmat[hh]).astype(BF16), v, preferred_element_type=F32)
            st = state[hh]
            o = o + xi[hh] * jnp.dot(q, st.astype(BF16), preferred_element_type=F32)
            kz_t = (k.astype(F32) * zeta[hh]).T.astype(BF16)
            block_decay = jnp.exp(blk * lg_ref[hh][:, :RET_DV])
            state[hh] = st * block_decay + jnp.dot(kz_t, v, preferred_element_type=F32)

            mu = jnp.mean(o, axis=-1, keepdims=True)
            dlt = o - mu
            var = jnp.mean(dlt * dlt, axis=-1, keepdims=True)
            y = dlt * lax.rsqrt(var + EPS)
            o_ref[rows, v_cols] = (y * gate_ref[rows, v_cols].astype(F32)).astype(o_ref.dtype)
        return carry

    lax.fori_loop(0, nblk, body, 0, unroll=4)


def _retention(p, bsz, seq):
    t = p.shape[0]
    blk = RET_BLOCK
    g = RET_GROUP
    log_gamma = jnp.log(1.0 - 2.0 ** (-5.0 - jnp.arange(RET_HEADS, dtype=F32)))
    lg = jnp.broadcast_to(log_gamma[:, None, None], (RET_HEADS, 1, blk))
    return pl.pallas_call(
        _retention_kernel,
        grid=(bsz, RET_HEADS // g),
        in_specs=[
            pl.BlockSpec((seq, g * RET_DK), lambda b, h: (b, C_RQ // (g * RET_DK) + h)),
            pl.BlockSpec((seq, g * RET_DK), lambda b, h: (b, C_RK // (g * RET_DK) + h)),
            pl.BlockSpec((seq, g * RET_DV), lambda b, h: (b, C_RV // (g * RET_DV) + h)),
            pl.BlockSpec((seq, g * RET_DV), lambda b, h: (b, C_RG // (g * RET_DV) + h)),
            pl.BlockSpec((g, 1, blk), lambda b, h: (h, 0, 0)),
        ],
        out_specs=pl.BlockSpec((seq, g * RET_DV), lambda b, h: (b, h)),
        out_shape=jax.ShapeDtypeStruct((t, RET_V_W), BF16),
        scratch_shapes=[
            pltpu.VMEM((g, RET_DK, RET_DV), F32),
            pltpu.VMEM((g, blk, blk), F32),
            pltpu.VMEM((g, blk, RET_DV), F32),
            pltpu.VMEM((g, blk, RET_DK), F32),
        ],
        compiler_params=_params(("parallel", "parallel")),
        name="retention",
    )(p, p, p, p, lg)


ATT_BLOCK = 512
ATT_HEADS = 2
ATT_REGION_PAIRS = 11


def _attention_kernel(q_ref, k_ref, v_ref, gate_ref, bias_ref, o_ref, s_scr, m_scr, acc_scr):
    seq = q_ref.shape[0]
    tq = s_scr.shape[1]
    nq = seq // tq

    def windows(qi, kb):
        if kb == qi and tq % (2 * CHUNK) == 0:
            return ((0, tq // 2, tq // 2), (tq // 2, tq, tq))
        return ((0, tq, tq),)

    def slot(next_pair, pair):
        for hh in range(ATT_HEADS):
            qk_cols = slice(hh * MLA_QK_PAD, (hh + 1) * MLA_QK_PAD)
            v_cols = slice(hh * MLA_V_PAD, (hh + 1) * MLA_V_PAD)
            m_cols = slice(hh * LANES, (hh + 1) * LANES)
            if pair is not None:
                qi, kb = pair
                for r0, r1, nk in windows(qi, kb):
                    s = s_scr[hh, r0:r1, :nk]
                    if kb == qi:
                        s = s + bias_ref[r0:r1, :nk]
                    m_cur = jnp.max(s, axis=-1, keepdims=True)
                    if kb == 0:
                        m_new = jnp.broadcast_to(m_cur, (r1 - r0, LANES))
                    else:
                        m_old = m_scr[r0:r1, m_cols]
                        m_new = jnp.maximum(m_old, m_cur)
                    p = jnp.exp2(s - jnp.concatenate([m_new] * (nk // LANES), axis=1))
                    v = v_ref[kb * tq:kb * tq + nk, v_cols]
                    pv = jnp.dot(p.astype(BF16), v, preferred_element_type=F32)
                    if kb == 0:
                        acc = pv
                    else:
                        alpha = jnp.concatenate([jnp.exp2(m_old - m_new)] * (MLA_V_PAD // LANES), axis=1)
                        acc = acc_scr[r0:r1, v_cols] * alpha + pv
                    if kb == qi:
                        out_rows = slice(qi * tq + r0, qi * tq + r1)
                        cols = slice(hh * MLA_DV, (hh + 1) * MLA_DV)
                        o = acc[:, :MLA_DV] / acc[:, MLA_DV:]
                        o_ref[out_rows, cols] = (o * gate_ref[out_rows, cols].astype(F32)).astype(o_ref.dtype)
                    else:
                        acc_scr[r0:r1, v_cols] = acc
                        m_scr[r0:r1, m_cols] = m_new
            if next_pair is not None:
                nqi, nkb = next_pair
                for r0, r1, nk in windows(nqi, nkb):
                    q = q_ref[nqi * tq + r0:nqi * tq + r1, qk_cols]
                    k = k_ref[nkb * tq:nkb * tq + nk, qk_cols]
                    s_scr[hh, r0:r1, :nk] = lax.dot_general(
                        q, k, (((1,), (1,)), ((), ())), preferred_element_type=F32)

    pairs = [(qi, kb) for qi in range(nq) for kb in range(qi + 1)]
    regions, current = [], []
    for qi in range(nq):
        if current and len(current) + qi + 1 > ATT_REGION_PAIRS:
            regions.append(current)
            current = []
        current += [(qi, kb) for kb in range(qi + 1)]
    regions.append(current)

    for r, region in enumerate(regions):
        @pl.when(pl.program_id(0) + r >= 0)
        def _(r=r, region=region):
            if r == 0:
                slot(pairs[0], None)
            for pair in region:
                n = pairs.index(pair)
                slot(pairs[n + 1] if n + 1 < len(pairs) else None, pair)


def _attention(qc, kc, v, p, bsz, seq):
    t = qc.shape[0]
    tq = min(seq, ATT_BLOCK)
    nh = ATT_HEADS
    r = lax.broadcasted_iota(jnp.int32, (tq, tq), 0) // CHUNK
    c = lax.broadcasted_iota(jnp.int32, (tq, tq), 1) // CHUNK
    bias = jnp.where(c <= r, 0.0, NEG_INF * LOG2E).astype(F32)
    return pl.pallas_call(
        _attention_kernel,
        grid=(bsz, MLA_HEADS // nh),
        in_specs=[
            pl.BlockSpec((seq, nh * MLA_QK_PAD), lambda b, h: (b, h)),
            pl.BlockSpec((seq, nh * MLA_QK_PAD), lambda b, h: (b, h)),
            pl.BlockSpec((seq, nh * MLA_V_PAD), lambda b, h: (b, h)),
            pl.BlockSpec((seq, nh * MLA_DV), lambda b, h: (b, C_MG // (nh * MLA_DV) + h)),
            pl.BlockSpec((tq, tq), lambda b, h: (0, 0)),
        ],
        out_specs=pl.BlockSpec((seq, nh * MLA_DV), lambda b, h: (b, h)),
        out_shape=jax.ShapeDtypeStruct((t, MLA_V_W), BF16),
        scratch_shapes=[
            pltpu.VMEM((nh, tq, tq), F32),
            pltpu.VMEM((tq, nh * LANES), F32),
            pltpu.VMEM((tq, nh * MLA_V_PAD), F32),
        ],
        compiler_params=_params(("parallel", "parallel")),
        name="attention",
    )(qc, kc, v, p, bias)


PROJ_CHUNK = 256


def _merge_kernel(a_ref, b_ref, wr_ref, wm_ref, ga_ref, gb_ref, o_ref):
    for c0 in range(0, o_ref.shape[1], PROJ_CHUNK):
        cols = slice(c0, c0 + PROJ_CHUNK)
        y_ret = jnp.dot(a_ref[...], wr_ref[:, cols], preferred_element_type=F32)
        y_mla = jnp.dot(b_ref[...], wm_ref[:, cols], preferred_element_type=F32)
        merged = ga_ref[:, cols].astype(F32) * y_ret + gb_ref[:, cols].astype(F32) * y_mla
        o_ref[:, cols] = merged.astype(o_ref.dtype)


def _merge(a, bm, p, w_ret, w_mla, layer):
    t, d_in = a.shape
    d = w_ret.shape[2]
    tm = min(t, 1024)
    tn = 512
    return pl.pallas_call(
        _merge_kernel,
        grid=(t // tm, d // tn),
        in_specs=[
            pl.BlockSpec((tm, d_in), lambda i, j: (i, 0)),
            pl.BlockSpec((tm, d_in), lambda i, j: (i, 0)),
            pl.BlockSpec((None, d_in, tn), lambda i, j: (layer, 0, j)),
            pl.BlockSpec((None, d_in, tn), lambda i, j: (layer, 0, j)),
            pl.BlockSpec((tm, tn), lambda i, j: (i, C_BG // tn + j)),
            pl.BlockSpec((tm, tn), lambda i, j: (i, (C_BG + d) // tn + j)),
        ],
        out_specs=pl.BlockSpec((tm, tn), lambda i, j: (i, j)),
        out_shape=jax.ShapeDtypeStruct((t, d), BF16),
        compiler_params=_params(("parallel", "arbitrary")),
        name="merge_proj",
    )(a, bm, w_ret, w_mla, p, p)


def _out_kernel(m_ref, w_ref, x_ref, gate_ref, o_ref):
    for c0 in range(0, o_ref.shape[1], PROJ_CHUNK):
        cols = slice(c0, c0 + PROJ_CHUNK)
        out = jnp.dot(m_ref[...], w_ref[:, cols], preferred_element_type=F32)
        o_ref[:, cols] = x_ref[:, cols] + gate_ref[:, cols] * out


def _out_norm_kernel(m_ref, w_ref, x_ref, gate_ref, g_ref, o_ref):
    _out_kernel(m_ref, w_ref, x_ref, gate_ref, o_ref)
    y = o_ref[...]
    inv = lax.rsqrt(jnp.mean(y * y, axis=-1, keepdims=True) + EPS)
    o_ref[...] = (y * inv) * g_ref[...]


def _out_proj(merged, w_out, x2, mod4, layer, seq, g_final=None):
    t, d = x2.shape
    fuse_norm = g_final is not None
    tm = min(seq, 512)
    per_b = seq // tm
    tn = d
    gate_blk = 2 * d // tn
    in_specs = [
        pl.BlockSpec((tm, d), lambda i, j: (i, 0)),
        pl.BlockSpec((None, d, tn), lambda i, j: (layer, 0, j)),
        pl.BlockSpec((tm, tn), lambda i, j: (i, j)),
        pl.BlockSpec((None, None, 1, tn), lambda i, j: (layer, i // per_b, 0, gate_blk + j)),
    ]
    args = [merged, w_out, x2, mod4]
    if fuse_norm:
        in_specs.append(pl.BlockSpec((1, d), lambda i, j: (0, 0)))
        args.append(g_final.reshape(1, d))
    return pl.pallas_call(
        _out_norm_kernel if fuse_norm else _out_kernel,
        grid=(t // tm, d // tn),
        in_specs=in_specs,
        out_specs=pl.BlockSpec((tm, tn), lambda i, j: (i, j)),
        out_shape=jax.ShapeDtypeStruct((t, d), F32),
        compiler_params=_params(("parallel", "arbitrary")),
        name="out_norm" if fuse_norm else "out_proj",
    )(*args)


def kernel(x, c, positions, w_mod, b_mod, g_norm, w_in, g_cq, g_ckv, w_uq, w_ukv,
           w_ret_proj, w_mla_proj, w_out, g_final):
    bsz, seq, d = x.shape
    depth = w_in.shape[0]
    t = bsz * seq
    assert seq % RET_BLOCK == 0 and seq % min(seq, ATT_BLOCK) == 0 and d % IN_TN == 0

    kr_lo = C_CKV + MLA_KV_RANK
    w_in_b = w_in.astype(BF16)
    w_lo = w_in_b
    w_hi = w_in_b[:, :, kr_lo + MLA_ROPE:]
    w_kr = jnp.pad(w_in[:, :, kr_lo:kr_lo + MLA_ROPE], ((0, 0), (0, 0), (0, LANES - MLA_ROPE))).astype(BF16)
    w_uq_p = jnp.pad(
        w_uq.reshape(depth, MLA_Q_RANK, MLA_HEADS, MLA_NOPE + MLA_ROPE),
        ((0, 0), (0, 0), (0, 0), (0, MLA_QK_PAD - MLA_NOPE - MLA_ROPE)),
    ).reshape(depth, MLA_Q_RANK, MLA_HEADS * MLA_QK_PAD).astype(BF16)
    w_ukv4 = w_ukv.reshape(depth, MLA_KV_RANK, MLA_HEADS, MLA_NOPE + MLA_DV)
    w_k = w_ukv4[..., :MLA_NOPE].reshape(depth, MLA_KV_RANK, MLA_HEADS * MLA_NOPE).astype(BF16)
    w_v = w_ukv4[..., MLA_NOPE:].reshape(depth, MLA_KV_RANK, MLA_V_W).astype(BF16)
    w_ret = w_ret_proj.astype(BF16)
    w_mla = w_mla_proj.astype(BF16)
    w_o = w_out.astype(BF16)

    tabs = _rope_tables(positions)
    mod = _modulation(c, w_mod, b_mod)
    mod4 = mod.reshape(depth, mod.shape[1], 1, 3 * d)
    g_norm3 = g_norm.reshape(depth, 1, d)
    g_cq3 = g_cq.reshape(depth, 1, MLA_Q_RANK)
    g_ckv3 = g_ckv.reshape(depth, 1, MLA_KV_RANK)

    x2 = x.reshape(t, d)
    for layer in range(depth):
        p, kr = _in_proj(x2, mod4, g_norm3, w_lo, w_hi, w_kr, tabs, g_cq3, g_ckv3, layer, seq)
        qc, kc, v = _latent_up(p, kr, w_uq_p, w_k, w_v, tabs, layer)
        a = _retention(p, bsz, seq)
        bm = _attention(qc, kc, v, p, bsz, seq)
        merged = _merge(a, bm, p, w_ret, w_mla, layer)
        last = layer == depth - 1
        x2 = _out_proj(merged, w_o, x2, mod4, layer, seq, g_final if last else None)
    return x2.reshape(bsz, seq, d)
```

```python
import jax
import jax.numpy as jnp
from jax import lax
from jax.experimental import pallas as pl
from jax.experimental.pallas import tpu as pltpu

F32 = jnp.float32
BF16 = jnp.bfloat16

CHUNK = 64
EPS = 1e-6
NEG_INF = -1e30
ROPE_BASE = 10000.0
RET_HEADS = 8
RET_DK = 128
RET_DV = 256
RET_QK_W = RET_HEADS * RET_DK
RET_V_W = RET_HEADS * RET_DV
MLA_HEADS = 16
MLA_Q_RANK = 512
MLA_KV_RANK = 512
MLA_NOPE = 128
MLA_ROPE = 64
MLA_DV = 128
MLA_V_W = MLA_HEADS * MLA_DV
MLA_QK_PAD = 256
MLA_V_PAD = 2 * MLA_DV
LOG2E = 1.4426950408889634

LANES = 128
VMEM_LIMIT = 56 * 1024 * 1024

C_RQ = 0
C_RK = C_RQ + RET_QK_W
C_RV = C_RK + RET_QK_W
C_RG = C_RV + RET_V_W
C_CQ = C_RG + RET_V_W
C_CKV = C_CQ + MLA_Q_RANK
C_MG = C_CKV + MLA_KV_RANK
C_BG = C_MG + MLA_V_W
IN_TN = 1024
IN_CHUNK = 256


def _params(sem):
    return pltpu.CompilerParams(dimension_semantics=sem, vmem_limit_bytes=VMEM_LIMIT)


def _sigmoid(x):
    return 0.5 * jnp.tanh(0.5 * x) + 0.5


def _rope_tables_kernel(pos_ref, inv_ref, cr_ref, sr_ref, cm_ref, sm1_ref, sm2_ref):
    pos = pos_ref[...].astype(F32)
    lane = lax.broadcasted_iota(jnp.int32, cr_ref.shape, 1)
    ang = pos * inv_ref[...]
    c = jnp.cos(ang)
    s = jnp.sin(ang)
    ret_half = RET_DK // 2
    half = MLA_ROPE // 2
    c_up, s_up = pltpu.roll(c, LANES - ret_half, 1), pltpu.roll(s, LANES - ret_half, 1)
    c_up2, s_up2 = pltpu.roll(c, LANES - half, 1), pltpu.roll(s, LANES - half, 1)
    cr_ref[...] = jnp.where(lane < ret_half, c, pltpu.roll(c, ret_half, 1))
    sr_ref[...] = jnp.where(lane < ret_half, -s, pltpu.roll(s, ret_half, 1))
    in_lo = lane < half
    in_hi = (lane >= half) & (lane < MLA_ROPE)
    cm_ref[...] = jnp.where(in_lo, c_up, jnp.where(in_hi, c_up2, 0.0))
    sm1_ref[...] = jnp.where(in_lo, -s_up, 0.0)
    sm2_ref[...] = jnp.where(in_hi, s_up2, 0.0)


def _rope_tables(positions):
    t = positions.size
    tm = min(t, 1024)
    inv_r = 1.0 / (ROPE_BASE ** (jnp.arange(0, RET_DK, 2, dtype=F32) / RET_DK))
    inv_m = 1.0 / (ROPE_BASE ** (jnp.arange(0, MLA_ROPE, 2, dtype=F32) / MLA_ROPE))
    inv = jnp.concatenate([inv_r, inv_m, jnp.zeros((LANES - RET_DK // 2 - MLA_ROPE // 2,), F32)])[None, :]
    row = pl.BlockSpec((tm, LANES), lambda i: (i, 0))
    return pl.pallas_call(
        _rope_tables_kernel,
        grid=(t // tm,),
        in_specs=[pl.BlockSpec((tm, 1), lambda i: (i, 0)), pl.BlockSpec((1, LANES), lambda i: (0, 0))],
        out_specs=[row] * 5,
        out_shape=[jax.ShapeDtypeStruct((t, LANES), F32)] * 5,
        compiler_params=_params(("parallel",)),
        name="rope_tables",
    )(positions.reshape(t, 1), inv)


def _mod_kernel(c_ref, w_ref, b_ref, o_ref):
    c = c_ref[...]
    ca = (c * _sigmoid(c)).astype(BF16)
    o_ref[...] = jnp.dot(ca, w_ref[...].astype(BF16), preferred_element_type=F32) + b_ref[...]


def _modulation(c, w_mod, b_mod):
    depth, d, n = w_mod.shape
    bsz = c.shape[0]
    rows = 8
    cp = jnp.pad(c, ((0, rows - bsz), (0, 0)))
    tn = 512
    return pl.pallas_call(
        _mod_kernel,
        grid=(depth, n // tn),
        in_specs=[
            pl.BlockSpec((rows, d), lambda l, j: (0, 0)),
            pl.BlockSpec((None, d, tn), lambda l, j: (l, 0, j)),
            pl.BlockSpec((None, 1, tn), lambda l, j: (l, 0, j)),
        ],
        out_specs=pl.BlockSpec((None, rows, tn), lambda l, j: (l, 0, j)),
        out_shape=jax.ShapeDtypeStruct((depth, rows, n), F32),
        compiler_params=_params(("parallel", "parallel")),
        name="modulation",
    )(cp, w_mod, b_mod.reshape(depth, 1, n))


def _rope_ret(a, cos, sin):
    outs = []
    for hh in range(a.shape[1] // RET_DK):
        t = a[:, hh * RET_DK:(hh + 1) * RET_DK]
        outs.append(t * cos + pltpu.roll(t, RET_DK // 2, 1) * sin)
    return jnp.concatenate(outs, axis=1) if len(outs) > 1 else outs[0]


def _rope_mla(t, cos, sin_lo, sin_hi):
    half = MLA_ROPE // 2
    return t * cos + pltpu.roll(t, LANES - half, 1) * sin_lo + pltpu.roll(t, half, 1) * sin_hi


def _in_proj_kernel(x_ref, shift_ref, scale_ref, g_ref, wlo_ref, whi_ref, wkr_ref, cr_ref, sr_ref,
                    cm_ref, sm1_ref, sm2_ref, gcq_ref, gckv_ref, p_ref, kr_ref, h_scr):
    j = pl.program_id(1)

    @pl.when(j == 0)
    def _():
        x = x_ref[...]
        inv = lax.rsqrt(jnp.mean(x * x, axis=-1, keepdims=True) + EPS)
        h = (x * inv) * g_ref[...]
        h = h * (1.0 + scale_ref[...]) + shift_ref[...]
        hb = h.astype(BF16)
        h_scr[...] = hb
        kr = jnp.dot(hb, wkr_ref[...], preferred_element_type=F32)
        kr_ref[...] = _rope_mla(kr, cm_ref[...], sm1_ref[...], sm2_ref[...]).astype(kr_ref.dtype)

    def tile_range(lo, width):
        return (j >= lo // IN_TN) & (j < (lo + width) // IN_TN)

    def project(w_ref, epilogues, chunk=IN_CHUNK):
        for idx, c0 in enumerate(range(0, IN_TN, chunk)):
            acc = jnp.dot(h_scr[...], w_ref[:, c0:c0 + chunk], preferred_element_type=F32)
            p_ref[:, c0:c0 + chunk] = epilogues[idx % len(epilogues)](acc).astype(p_ref.dtype)

    def rope(acc):
        return _rope_ret(acc, cr_ref[...], sr_ref[...])

    def latent_norm(g_latent_ref):
        def apply(acc):
            inv = lax.rsqrt(jnp.mean(acc * acc, axis=-1, keepdims=True) + EPS)
            return (acc * inv) * g_latent_ref[...]
        return apply

    @pl.when(tile_range(C_RQ, RET_QK_W))
    def _():
        project(wlo_ref, [rope])

    @pl.when(tile_range(C_RK, RET_QK_W))
    def _():
        project(wlo_ref, [lambda acc: rope(acc) * (RET_DK ** -0.5)])

    @pl.when(tile_range(C_RV, RET_V_W))
    def _():
        project(wlo_ref, [lambda acc: acc])

    def silu(acc):
        return acc * _sigmoid(acc)

    @pl.when(tile_range(C_RG, RET_V_W))
    def _():
        project(wlo_ref, [silu])

    @pl.when(tile_range(C_MG, MLA_V_W))
    def _():
        project(whi_ref, [silu])

    @pl.when(tile_range(C_CQ, MLA_Q_RANK + MLA_KV_RANK))
    def _():
        project(wlo_ref, [latent_norm(gcq_ref), latent_norm(gckv_ref)], chunk=MLA_Q_RANK)

    @pl.when(j >= C_BG // IN_TN)
    def _():
        project(whi_ref, [_sigmoid])


def _in_proj(x2, mod4, g_norm3, w_lo, w_hi, w_kr, tabs, g_cq3, g_ckv3, layer, seq):
    t, d = x2.shape
    lo_tiles = C_MG // IN_TN
    n = C_MG + w_hi.shape[2]
    tm = min(seq, 1024)
    per_b = seq // tm
    cr, sr, cm, sm1, sm2 = tabs
    tab = pl.BlockSpec((tm, LANES), lambda i, j: (i, 0))
    return pl.pallas_call(
        _in_proj_kernel,
        grid=(t // tm, n // IN_TN),
        in_specs=[
            pl.BlockSpec((tm, d), lambda i, j: (i, 0)),
            pl.BlockSpec((None, None, 1, d), lambda i, j: (layer, i // per_b, 0, 0)),
            pl.BlockSpec((None, None, 1, d), lambda i, j: (layer, i // per_b, 0, 1)),
            pl.BlockSpec((None, 1, d), lambda i, j: (layer, 0, 0)),
            pl.BlockSpec((None, d, IN_TN), lambda i, j: (layer, 0, jnp.minimum(j, lo_tiles - 1))),
            pl.BlockSpec((None, d, IN_TN), lambda i, j: (layer, 0, jnp.maximum(j - lo_tiles, 0))),
            pl.BlockSpec((None, d, LANES), lambda i, j: (layer, 0, 0)),
            tab, tab, tab, tab, tab,
            pl.BlockSpec((None, 1, MLA_Q_RANK), lambda i, j: (layer, 0, 0)),
            pl.BlockSpec((None, 1, MLA_KV_RANK), lambda i, j: (layer, 0, 0)),
        ],
        out_specs=[
            pl.BlockSpec((tm, IN_TN), lambda i, j: (i, j)),
            pl.BlockSpec((tm, LANES), lambda i, j: (i, 0)),
        ],
        out_shape=[
            jax.ShapeDtypeStruct((t, n), BF16),
            jax.ShapeDtypeStruct((t, LANES), BF16),
        ],
        scratch_shapes=[pltpu.VMEM((tm, d), BF16)],
        compiler_params=_params(("parallel", "arbitrary")),
        name="in_proj",
    )(x2, mod4, mod4, g_norm3, w_lo, w_hi, w_kr, cr, sr, cm, sm1, sm2, g_cq3, g_ckv3)


UP_HEADS = MLA_HEADS


def _latent_up_kernel(cq_ref, ckv_ref, wq_ref, wk_ref, wv_ref, kr_ref, cm_ref, sm1_ref, sm2_ref,
                      q_ref, k_ref, v_ref):
    scale = (MLA_NOPE + MLA_ROPE) ** -0.5 * LOG2E
    cq = cq_ref[...]
    ckv = ckv_ref[...]
    cm, s1, s2 = cm_ref[...], sm1_ref[...], sm2_ref[...]
    kr = kr_ref[...]
    for hh in range(UP_HEADS):
        lo = hh * MLA_QK_PAD
        acc = jnp.dot(cq, wq_ref[:, lo:lo + MLA_QK_PAD], preferred_element_type=F32) * scale
        q_ref[:, lo:lo + MLA_NOPE] = acc[:, :MLA_NOPE].astype(q_ref.dtype)
        q_ref[:, lo + MLA_NOPE:lo + MLA_QK_PAD] = _rope_mla(acc[:, MLA_NOPE:], cm, s1, s2).astype(q_ref.dtype)
    ones = jnp.ones((cq.shape[0], MLA_V_PAD - MLA_DV), v_ref.dtype)
    pair_w = 2 * MLA_NOPE
    for h0 in range(0, UP_HEADS, 2):
        src = slice(h0 * MLA_NOPE, h0 * MLA_NOPE + pair_w)
        kn = jnp.dot(ckv, wk_ref[:, src], preferred_element_type=F32)
        vv = jnp.dot(ckv, wv_ref[:, src], preferred_element_type=F32)
        for hh in range(2):
            lo = (h0 + hh) * MLA_QK_PAD
            k_ref[:, lo:lo + MLA_NOPE] = kn[:, hh * MLA_NOPE:(hh + 1) * MLA_NOPE].astype(k_ref.dtype)
            k_ref[:, lo + MLA_NOPE:lo + MLA_QK_PAD] = kr
            vlo = (h0 + hh) * MLA_V_PAD
            v_ref[:, vlo:vlo + MLA_DV] = vv[:, hh * MLA_DV:(hh + 1) * MLA_DV].astype(v_ref.dtype)
            v_ref[:, vlo + MLA_DV:vlo + MLA_V_PAD] = ones


def _latent_up(p, kr, w_uq_p, w_k, w_v, tabs, layer):
    t = p.shape[0]
    tm = min(t, 512)
    _, _, cm, sm1, sm2 = tabs
    tab = pl.BlockSpec((tm, LANES), lambda i, j: (i, 0))

    def head_cols(width):
        return pl.BlockSpec((tm, UP_HEADS * width), lambda i, j: (i, j))

    return pl.pallas_call(
        _latent_up_kernel,
        grid=(t // tm, MLA_HEADS // UP_HEADS),
        in_specs=[
            pl.BlockSpec((tm, MLA_Q_RANK), lambda i, j: (i, C_CQ // MLA_Q_RANK)),
            pl.BlockSpec((tm, MLA_KV_RANK), lambda i, j: (i, C_CKV // MLA_KV_RANK)),
            pl.BlockSpec((None, MLA_Q_RANK, UP_HEADS * MLA_QK_PAD), lambda i, j: (layer, 0, j)),
            pl.BlockSpec((None, MLA_KV_RANK, UP_HEADS * MLA_NOPE), lambda i, j: (layer, 0, j)),
            pl.BlockSpec((None, MLA_KV_RANK, UP_HEADS * MLA_DV), lambda i, j: (layer, 0, j)),
            tab, tab, tab, tab,
        ],
        out_specs=[head_cols(MLA_QK_PAD), head_cols(MLA_QK_PAD), head_cols(MLA_V_PAD)],
        out_shape=[
            jax.ShapeDtypeStruct((t, MLA_HEADS * MLA_QK_PAD), BF16),
            jax.ShapeDtypeStruct((t, MLA_HEADS * MLA_QK_PAD), BF16),
            jax.ShapeDtypeStruct((t, MLA_HEADS * MLA_V_PAD), BF16),
        ],
        compiler_params=_params(("parallel", "arbitrary")),
        name="latent_up",
    )(p, p, w_uq_p, w_k, w_v, kr, cm, sm1, sm2)


RET_BLOCK = 256


RET_GROUP = 2


def _retention_kernel(q_ref, k_ref, v_ref, gate_ref, lg_ref, o_ref, state, dmat, xi, zeta):
    blk = dmat.shape[1]
    nblk = q_ref.shape[0] // blk

    state[...] = jnp.zeros_like(state)
    for hh in range(RET_GROUP):
        lg = lg_ref[hh]
        r = lax.broadcasted_iota(jnp.int32, (blk, blk), 0)
        c = lax.broadcasted_iota(jnp.int32, (blk, blk), 1)
        decay = jnp.exp(jnp.abs(r - c).astype(F32) * lg)
        dmat[hh] = jnp.where((c // CHUNK) <= (r // CHUNK), decay, 0.0)
        rx = lax.broadcasted_iota(jnp.int32, xi.shape[1:], 0).astype(F32)
        xi[hh] = jnp.exp((rx + 1.0) * lg[:, :RET_DV])
        rz = lax.broadcasted_iota(jnp.int32, zeta.shape[1:], 0).astype(F32)
        zeta[hh] = jnp.exp((blk - 1.0 - rz) * lg[:, :RET_DK])

    def body(j, carry):
        rows = pl.ds(pl.multiple_of(j * blk, blk), blk)
        for hh in range(RET_GROUP):
            qk_cols = slice(hh * RET_DK, (hh + 1) * RET_DK)
            v_cols = slice(hh * RET_DV, (hh + 1) * RET_DV)
            q = q_ref[rows, qk_cols]
            k = k_ref[rows, qk_cols]
            v = v_ref[rows, v_cols]
            s = lax.dot_general(q, k, (((1,), (1,)), ((), ())), preferred_element_type=F32)
            o = jnp.dot((s * dmat[hh]).astype(BF16), v, preferred_element_type=F32)
            st = state[hh]
            o = o + xi[hh] * jnp.dot(q, st.astype(BF16), preferred_element_type=F32)
            kz_t = (k.astype(F32) * zeta[hh]).T.astype(BF16)
            block_decay = jnp.exp(blk * lg_ref[hh][:, :RET_DV])
            state[hh] = st * block_decay + jnp.dot(kz_t, v, preferred_element_type=F32)

            mu = jnp.mean(o, axis=-1, keepdims=True)
            dlt = o - mu
            var = jnp.mean(dlt * dlt, axis=-1, keepdims=True)
            y = dlt * lax.rsqrt(var + EPS)
            o_ref[rows, v_cols] = y.astype(o_ref.dtype) * gate_ref[rows, v_cols]
        return carry

    lax.fori_loop(0, nblk, body, 0, unroll=4)


def _retention(p, bsz, seq):
    t = p.shape[0]
    blk = RET_BLOCK
    g = RET_GROUP
    log_gamma = jnp.log(1.0 - 2.0 ** (-5.0 - jnp.arange(RET_HEADS, dtype=F32)))
    lg = jnp.broadcast_to(log_gamma[:, None, None], (RET_HEADS, 1, blk))
    return pl.pallas_call(
        _retention_kernel,
        grid=(bsz, RET_HEADS // g),
        in_specs=[
            pl.BlockSpec((seq, g * RET_DK), lambda b, h: (b, C_RQ // (g * RET_DK) + h)),
            pl.BlockSpec((seq, g * RET_DK), lambda b, h: (b, C_RK // (g * RET_DK) + h)),
            pl.BlockSpec((seq, g * RET_DV), lambda b, h: (b, C_RV // (g * RET_DV) + h)),
            pl.BlockSpec((seq, g * RET_DV), lambda b, h: (b, C_RG // (g * RET_DV) + h)),
            pl.BlockSpec((g, 1, blk), lambda b, h: (h, 0, 0)),
        ],
        out_specs=pl.BlockSpec((seq, g * RET_DV), lambda b, h: (b, h)),
        out_shape=jax.ShapeDtypeStruct((t, RET_V_W), BF16),
        scratch_shapes=[
            pltpu.VMEM((g, RET_DK, RET_DV), F32),
            pltpu.VMEM((g, blk, blk), F32),
            pltpu.VMEM((g, blk, RET_DV), F32),
            pltpu.VMEM((g, blk, RET_DK), F32),
        ],
        compiler_params=_params(("parallel", "parallel")),
        name="retention",
    )(p, p, p, p, lg)


ATT_BLOCK = 512
ATT_HEADS = 2
ATT_REGION_PAIRS = 11


def _attention_kernel(q_ref, k_ref, v_ref, gate_ref, bias_ref, o_ref, s_scr, m_scr, acc_scr):
    seq = q_ref.shape[0]
    tq = s_scr.shape[1]
    nq = seq // tq

    def windows(qi, kb):
        if kb == qi and tq % (2 * CHUNK) == 0:
            return ((0, tq // 2, tq // 2), (tq // 2, tq, tq))
        return ((0, tq, tq),)

    def slot(next_pair, pair):
        for hh in range(ATT_HEADS):
            qk_cols = slice(hh * MLA_QK_PAD, (hh + 1) * MLA_QK_PAD)
            v_cols = slice(hh * MLA_V_PAD, (hh + 1) * MLA_V_PAD)
            m_cols = slice(hh * LANES, (hh + 1) * LANES)
            if pair is not None:
                qi, kb = pair
                for r0, r1, nk in windows(qi, kb):
                    s = s_scr[hh, r0:r1, :nk]
                    if kb == qi:
                        s = s + bias_ref[r0:r1, :nk]
                    m_cur = jnp.max(s, axis=-1, keepdims=True)
                    if kb == 0:
                        m_new = jnp.broadcast_to(m_cur, (r1 - r0, LANES))
                    else:
                        m_old = m_scr[r0:r1, m_cols]
                        m_new = jnp.maximum(m_old, m_cur)
                    p = jnp.exp2(s - jnp.concatenate([m_new] * (nk // LANES), axis=1))
                    v = v_ref[kb * tq:kb * tq + nk, v_cols]
                    pv = jnp.dot(p.astype(BF16), v, preferred_element_type=F32)
                    if kb == 0:
                        acc = pv
                    else:
                        alpha = jnp.concatenate([jnp.exp2(m_old - m_new)] * (MLA_V_PAD // LANES), axis=1)
                        acc = acc_scr[r0:r1, v_cols] * alpha + pv
                    if kb == qi:
                        out_rows = slice(qi * tq + r0, qi * tq + r1)
                        cols = slice(hh * MLA_DV, (hh + 1) * MLA_DV)
                        o = acc[:, :MLA_DV] / acc[:, MLA_DV:]
                        o_ref[out_rows, cols] = (o * gate_ref[out_rows, cols].astype(F32)).astype(o_ref.dtype)
                    else:
                        acc_scr[r0:r1, v_cols] = acc
                        m_scr[r0:r1, m_cols] = m_new
            if next_pair is not None:
                nqi, nkb = next_pair
                for r0, r1, nk in windows(nqi, nkb):
                    q = q_ref[nqi * tq + r0:nqi * tq + r1, qk_cols]
                    k = k_ref[nkb * tq:nkb * tq + nk, qk_cols]
                    s_scr[hh, r0:r1, :nk] = lax.dot_general(
                        q, k, (((1,), (1,)), ((), ())), preferred_element_type=F32)

    pairs = [(qi, kb) for qi in range(nq) for kb in range(qi + 1)]
    regions, current = [], []
    for qi in range(nq):
        if current and len(current) + qi + 1 > ATT_REGION_PAIRS:
            regions.append(current)
            current = []
        current += [(qi, kb) for kb in range(qi + 1)]
    regions.append(current)

    for r, region in enumerate(regions):
        @pl.when(pl.program_id(0) + r >= 0)
        def _(r=r, region=region):
            if r == 0:
                slot(pairs[0], None)
            for pair in region:
                n = pairs.index(pair)
                slot(pairs[n + 1] if n + 1 < len(pairs) else None, pair)


def _attention(qc, kc, v, p, bsz, seq):
    t = qc.shape[0]
    tq = min(seq, ATT_BLOCK)
    nh = ATT_HEADS
    r = lax.broadcasted_iota(jnp.int32, (tq, tq), 0) // CHUNK
    c = lax.broadcasted_iota(jnp.int32, (tq, tq), 1) // CHUNK
    bias = jnp.where(c <= r, 0.0, NEG_INF * LOG2E).astype(F32)
    return pl.pallas_call(
        _attention_kernel,
        grid=(bsz, MLA_HEADS // nh),
        in_specs=[
            pl.BlockSpec((seq, nh * MLA_QK_PAD), lambda b, h: (b, h)),
            pl.BlockSpec((seq, nh * MLA_QK_PAD), lambda b, h: (b, h)),
            pl.BlockSpec((seq, nh * MLA_V_PAD), lambda b, h: (b, h)),
            pl.BlockSpec((seq, nh * MLA_DV), lambda b, h: (b, C_MG // (nh * MLA_DV) + h)),
            pl.BlockSpec((tq, tq), lambda b, h: (0, 0)),
        ],
        out_specs=pl.BlockSpec((seq, nh * MLA_DV), lambda b, h: (b, h)),
        out_shape=jax.ShapeDtypeStruct((t, MLA_V_W), BF16),
        scratch_shapes=[
            pltpu.VMEM((nh, tq, tq), F32),
            pltpu.VMEM((tq, nh * LANES), F32),
            pltpu.VMEM((tq, nh * MLA_V_PAD), F32),
        ],
        compiler_params=_params(("parallel", "parallel")),
        name="attention",
    )(qc, kc, v, p, bias)


PROJ_CHUNK = 256


def _merge_kernel(a_ref, b_ref, wr_ref, wm_ref, ga_ref, gb_ref, o_ref):
    for c0 in range(0, o_ref.shape[1], PROJ_CHUNK):
        cols = slice(c0, c0 + PROJ_CHUNK)
        y_ret = jnp.dot(a_ref[...], wr_ref[:, cols], preferred_element_type=F32)
        y_mla = jnp.dot(b_ref[...], wm_ref[:, cols], preferred_element_type=F32)
        merged = ga_ref[:, cols].astype(F32) * y_ret + gb_ref[:, cols].astype(F32) * y_mla
        o_ref[:, cols] = merged.astype(o_ref.dtype)


def _merge(a, bm, p, w_ret, w_mla, layer):
    t, d_in = a.shape
    d = w_ret.shape[2]
    tm = min(t, 1024)
    tn = 1024
    return pl.pallas_call(
        _merge_kernel,
        grid=(t // tm, d // tn),
        in_specs=[
            pl.BlockSpec((tm, d_in), lambda i, j: (i, 0)),
            pl.BlockSpec((tm, d_in), lambda i, j: (i, 0)),
            pl.BlockSpec((None, d_in, tn), lambda i, j: (layer, 0, j)),
            pl.BlockSpec((None, d_in, tn), lambda i, j: (layer, 0, j)),
            pl.BlockSpec((tm, tn), lambda i, j: (i, C_BG // tn + j)),
            pl.BlockSpec((tm, tn), lambda i, j: (i, (C_BG + d) // tn + j)),
        ],
        out_specs=pl.BlockSpec((tm, tn), lambda i, j: (i, j)),
        out_shape=jax.ShapeDtypeStruct((t, d), BF16),
        compiler_params=_params(("parallel", "arbitrary")),
        name="merge_proj",
    )(a, bm, w_ret, w_mla, p, p)


def _out_kernel(m_ref, w_ref, x_ref, gate_ref, o_ref):
    for c0 in range(0, o_ref.shape[1], PROJ_CHUNK):
        cols = slice(c0, c0 + PROJ_CHUNK)
        out = jnp.dot(m_ref[...], w_ref[:, cols], preferred_element_type=F32)
        o_ref[:, cols] = x_ref[:, cols] + gate_ref[:, cols] * out


def _out_norm_kernel(m_ref, w_ref, x_ref, gate_ref, g_ref, o_ref):
    _out_kernel(m_ref, w_ref, x_ref, gate_ref, o_ref)
    y = o_ref[...]
    inv = lax.rsqrt(jnp.mean(y * y, axis=-1, keepdims=True) + EPS)
    o_ref[...] = (y * inv) * g_ref[...]


def _out_proj(merged, w_out, x2, mod4, layer, seq, g_final=None):
    t, d = x2.shape
    fuse_norm = g_final is not None
    tm = min(seq, 512)
    per_b = seq // tm
    tn = d
    gate_blk = 2 * d // tn
    in_specs = [
        pl.BlockSpec((tm, d), lambda i, j: (i, 0)),
        pl.BlockSpec((None, d, tn), lambda i, j: (layer, 0, j)),
        pl.BlockSpec((tm, tn), lambda i, j: (i, j)),
        pl.BlockSpec((None, None, 1, tn), lambda i, j: (layer, i // per_b, 0, gate_blk + j)),
    ]
    args = [merged, w_out, x2, mod4]
    if fuse_norm:
        in_specs.append(pl.BlockSpec((1, d), lambda i, j: (0, 0)))
        args.append(g_final.reshape(1, d))
    return pl.pallas_call(
        _out_norm_kernel if fuse_norm else _out_kernel,
        grid=(t // tm, d // tn),
        in_specs=in_specs,
        out_specs=pl.BlockSpec((tm, tn), lambda i, j: (i, j)),
        out_shape=jax.ShapeDtypeStruct((t, d), F32),
        compiler_params=_params(("parallel", "arbitrary")),
        name="out_norm" if fuse_norm else "out_proj",
    )(*args)


def kernel(x, c, positions, w_mod, b_mod, g_norm, w_in, g_cq, g_ckv, w_uq, w_ukv,
           w_ret_proj, w_mla_proj, w_out, g_final):
    bsz, seq, d = x.shape
    depth = w_in.shape[0]
    t = bsz * seq
    assert seq % RET_BLOCK == 0 and seq % min(seq, ATT_BLOCK) == 0 and d % IN_TN == 0

    kr_lo = C_CKV + MLA_KV_RANK
    w_in_b = w_in.astype(BF16)
    w_lo = w_in_b
    w_hi = w_in_b[:, :, kr_lo + MLA_ROPE:]
    w_kr = jnp.pad(w_in[:, :, kr_lo:kr_lo + MLA_ROPE], ((0, 0), (0, 0), (0, LANES - MLA_ROPE))).astype(BF16)
    w_uq_p = jnp.pad(
        w_uq.reshape(depth, MLA_Q_RANK, MLA_HEADS, MLA_NOPE + MLA_ROPE),
        ((0, 0), (0, 0), (0, 0), (0, MLA_QK_PAD - MLA_NOPE - MLA_ROPE)),
    ).reshape(depth, MLA_Q_RANK, MLA_HEADS * MLA_QK_PAD).astype(BF16)
    w_ukv4 = w_ukv.reshape(depth, MLA_KV_RANK, MLA_HEADS, MLA_NOPE + MLA_DV)
    w_k = w_ukv4[..., :MLA_NOPE].reshape(depth, MLA_KV_RANK, MLA_HEADS * MLA_NOPE).astype(BF16)
    w_v = w_ukv4[..., MLA_NOPE:].reshape(depth, MLA_KV_RANK, MLA_V_W).astype(BF16)
    w_ret = w_ret_proj.astype(BF16)
    w_mla = w_mla_proj.astype(BF16)
    w_o = w_out.astype(BF16)

    tabs = _rope_tables(positions)
    mod = _modulation(c, w_mod, b_mod)
    mod4 = mod.reshape(depth, mod.shape[1], 1, 3 * d)
    g_norm3 = g_norm.reshape(depth, 1, d)
    g_cq3 = g_cq.reshape(depth, 1, MLA_Q_RANK)
    g_ckv3 = g_ckv.reshape(depth, 1, MLA_KV_RANK)

    x2 = x.reshape(t, d)
    for layer in range(depth):
        p, kr = _in_proj(x2, mod4, g_norm3, w_lo, w_hi, w_kr, tabs, g_cq3, g_ckv3, layer, seq)
        qc, kc, v = _latent_up(p, kr, w_uq_p, w_k, w_v, tabs, layer)
        a = _retention(p, bsz, seq)
        bm = _attention(qc, kc, v, p, bsz, seq)
        merged = _merge(a, bm, p, w_ret, w_mla, layer)
        last = layer == depth - 1
        x2 = _out_proj(merged, w_o, x2, mod4, layer, seq, g_final if last else None)
    return x2.reshape(bsz, seq, d)
```

```python
import jax
import jax.numpy as jnp
from jax import lax
from jax.experimental import pallas as pl
from jax.experimental.pallas import tpu as pltpu

F32 = jnp.float32
BF16 = jnp.bfloat16

CHUNK = 64
EPS = 1e-6
NEG_INF = -1e30
ROPE_BASE = 10000.0
RET_HEADS = 8
RET_DK = 128
RET_DV = 256
RET_QK_W = RET_HEADS * RET_DK
RET_V_W = RET_HEADS * RET_DV
MLA_HEADS = 16
MLA_Q_RANK = 512
MLA_KV_RANK = 512
MLA_NOPE = 128
MLA_ROPE = 64
MLA_DV = 128
MLA_V_W = MLA_HEADS * MLA_DV
MLA_QK_PAD = 256
MLA_V_PAD = 2 * MLA_DV
LOG2E = 1.4426950408889634

LANES = 128
VMEM_LIMIT = 56 * 1024 * 1024

C_RQ = 0
C_RK = C_RQ + RET_QK_W
C_RV = C_RK + RET_QK_W
C_RG = C_RV + RET_V_W
C_CQ = C_RG + RET_V_W
C_CKV = C_CQ + MLA_Q_RANK
C_MG = C_CKV + MLA_KV_RANK
C_BG = C_MG + MLA_V_W
IN_TN = 1024
IN_CHUNK = 256


def _params(sem):
    return pltpu.CompilerParams(dimension_semantics=sem, vmem_limit_bytes=VMEM_LIMIT)


def _sigmoid(x):
    return 0.5 * jnp.tanh(0.5 * x) + 0.5


def _rope_tables_kernel(pos_ref, inv_ref, cr_ref, sr_ref, cm_ref, sm1_ref, sm2_ref):
    pos = pos_ref[...].astype(F32)
    lane = lax.broadcasted_iota(jnp.int32, cr_ref.shape, 1)
    ang = pos * inv_ref[...]
    c = jnp.cos(ang)
    s = jnp.sin(ang)
    ret_half = RET_DK // 2
    half = MLA_ROPE // 2
    c_up, s_up = pltpu.roll(c, LANES - ret_half, 1), pltpu.roll(s, LANES - ret_half, 1)
    c_up2, s_up2 = pltpu.roll(c, LANES - half, 1), pltpu.roll(s, LANES - half, 1)
    cr_ref[...] = jnp.where(lane < ret_half, c, pltpu.roll(c, ret_half, 1))
    sr_ref[...] = jnp.where(lane < ret_half, -s, pltpu.roll(s, ret_half, 1))
    in_lo = lane < half
    in_hi = (lane >= half) & (lane < MLA_ROPE)
    cm_ref[...] = jnp.where(in_lo, c_up, jnp.where(in_hi, c_up2, 0.0))
    sm1_ref[...] = jnp.where(in_lo, -s_up, 0.0)
    sm2_ref[...] = jnp.where(in_hi, s_up2, 0.0)


def _rope_tables(positions):
    t = positions.size
    tm = min(t, 1024)
    inv_r = 1.0 / (ROPE_BASE ** (jnp.arange(0, RET_DK, 2, dtype=F32) / RET_DK))
    inv_m = 1.0 / (ROPE_BASE ** (jnp.arange(0, MLA_ROPE, 2, dtype=F32) / MLA_ROPE))
    inv = jnp.concatenate([inv_r, inv_m, jnp.zeros((LANES - RET_DK // 2 - MLA_ROPE // 2,), F32)])[None, :]
    row = pl.BlockSpec((tm, LANES), lambda i: (i, 0))
    return pl.pallas_call(
        _rope_tables_kernel,
        grid=(t // tm,),
        in_specs=[pl.BlockSpec((tm, 1), lambda i: (i, 0)), pl.BlockSpec((1, LANES), lambda i: (0, 0))],
        out_specs=[row] * 5,
        out_shape=[jax.ShapeDtypeStruct((t, LANES), F32)] * 5,
        compiler_params=_params(("parallel",)),
        name="rope_tables",
    )(positions.reshape(t, 1), inv)


def _mod_kernel(c_ref, w_ref, b_ref, o_ref):
    c = c_ref[...]
    ca = (c * _sigmoid(c)).astype(BF16)
    o_ref[...] = jnp.dot(ca, w_ref[...].astype(BF16), preferred_element_type=F32) + b_ref[...]


def _modulation(c, w_mod, b_mod):
    depth, d, n = w_mod.shape
    bsz = c.shape[0]
    rows = 8
    cp = jnp.pad(c, ((0, rows - bsz), (0, 0)))
    tn = 512
    return pl.pallas_call(
        _mod_kernel,
        grid=(depth, n // tn),
        in_specs=[
            pl.BlockSpec((rows, d), lambda l, j: (0, 0)),
            pl.BlockSpec((None, d, tn), lambda l, j: (l, 0, j)),
            pl.BlockSpec((None, 1, tn), lambda l, j: (l, 0, j)),
        ],
        out_specs=pl.BlockSpec((None, rows, tn), lambda l, j: (l, 0, j)),
        out_shape=jax.ShapeDtypeStruct((depth, rows, n), F32),
        compiler_params=_params(("parallel", "parallel")),
        name="modulation",
    )(cp, w_mod, b_mod.reshape(depth, 1, n))


def _rope_ret(a, cos, sin):
    outs = []
    for hh in range(a.shape[1] // RET_DK):
        t = a[:, hh * RET_DK:(hh + 1) * RET_DK]
        outs.append(t * cos + pltpu.roll(t, RET_DK // 2, 1) * sin)
    return jnp.concatenate(outs, axis=1) if len(outs) > 1 else outs[0]


def _rope_mla(t, cos, sin_lo, sin_hi):
    half = MLA_ROPE // 2
    return t * cos + pltpu.roll(t, LANES - half, 1) * sin_lo + pltpu.roll(t, half, 1) * sin_hi


def _in_proj_kernel(x_ref, shift_ref, scale_ref, g_ref, wlo_ref, whi_ref, wkr_ref, cr_ref, sr_ref,
                    cm_ref, sm1_ref, sm2_ref, gcq_ref, gckv_ref, p_ref, kr_ref, h_scr):
    j = pl.program_id(1)

    @pl.when(j == 0)
    def _():
        x = x_ref[...]
        inv = lax.rsqrt(jnp.mean(x * x, axis=-1, keepdims=True) + EPS)
        h = (x * inv) * g_ref[...]
        h = h * (1.0 + scale_ref[...]) + shift_ref[...]
        hb = h.astype(BF16)
        h_scr[...] = hb
        kr = jnp.dot(hb, wkr_ref[...], preferred_element_type=F32)
        kr_ref[...] = _rope_mla(kr, cm_ref[...], sm1_ref[...], sm2_ref[...]).astype(kr_ref.dtype)

    def tile_range(lo, width):
        return (j >= lo // IN_TN) & (j < (lo + width) // IN_TN)

    def project(w_ref, epilogues, chunk=IN_CHUNK):
        for idx, c0 in enumerate(range(0, IN_TN, chunk)):
            acc = jnp.dot(h_scr[...], w_ref[:, c0:c0 + chunk], preferred_element_type=F32)
            p_ref[:, c0:c0 + chunk] = epilogues[idx % len(epilogues)](acc).astype(p_ref.dtype)

    def rope(acc):
        return _rope_ret(acc, cr_ref[...], sr_ref[...])

    def latent_norm(g_latent_ref):
        def apply(acc):
            inv = lax.rsqrt(jnp.mean(acc * acc, axis=-1, keepdims=True) + EPS)
            return (acc * inv) * g_latent_ref[...]
        return apply

    @pl.when(tile_range(C_RQ, RET_QK_W))
    def _():
        project(wlo_ref, [rope])

    @pl.when(tile_range(C_RK, RET_QK_W))
    def _():
        project(wlo_ref, [lambda acc: rope(acc) * (RET_DK ** -0.5)])

    @pl.when(tile_range(C_RV, RET_V_W))
    def _():
        project(wlo_ref, [lambda acc: acc])

    def silu(acc):
        return acc * _sigmoid(acc)

    @pl.when(tile_range(C_RG, RET_V_W))
    def _():
        project(wlo_ref, [silu])

    @pl.when(tile_range(C_MG, MLA_V_W))
    def _():
        project(whi_ref, [silu])

    @pl.when(tile_range(C_CQ, MLA_Q_RANK + MLA_KV_RANK))
    def _():
        project(wlo_ref, [latent_norm(gcq_ref), latent_norm(gckv_ref)], chunk=MLA_Q_RANK)

    @pl.when(j >= C_BG // IN_TN)
    def _():
        project(whi_ref, [_sigmoid])


def _in_proj(x2, mod4, g_norm3, w_lo, w_hi, w_kr, tabs, g_cq3, g_ckv3, layer, seq):
    t, d = x2.shape
    lo_tiles = C_MG // IN_TN
    n = C_MG + w_hi.shape[2]
    tm = min(seq, 1024)
    per_b = seq // tm
    cr, sr, cm, sm1, sm2 = tabs
    tab = pl.BlockSpec((tm, LANES), lambda i, j: (i, 0))
    return pl.pallas_call(
        _in_proj_kernel,
        grid=(t // tm, n // IN_TN),
        in_specs=[
            pl.BlockSpec((tm, d), lambda i, j: (i, 0)),
            pl.BlockSpec((None, None, 1, d), lambda i, j: (layer, i // per_b, 0, 0)),
            pl.BlockSpec((None, None, 1, d), lambda i, j: (layer, i // per_b, 0, 1)),
            pl.BlockSpec((None, 1, d), lambda i, j: (layer, 0, 0)),
            pl.BlockSpec((None, d, IN_TN), lambda i, j: (layer, 0, jnp.minimum(j, lo_tiles - 1))),
            pl.BlockSpec((None, d, IN_TN), lambda i, j: (layer, 0, jnp.maximum(j - lo_tiles, 0))),
            pl.BlockSpec((None, d, LANES), lambda i, j: (layer, 0, 0)),
            tab, tab, tab, tab, tab,
            pl.BlockSpec((None, 1, MLA_Q_RANK), lambda i, j: (layer, 0, 0)),
            pl.BlockSpec((None, 1, MLA_KV_RANK), lambda i, j: (layer, 0, 0)),
        ],
        out_specs=[
            pl.BlockSpec((tm, IN_TN), lambda i, j: (i, j)),
            pl.BlockSpec((tm, LANES), lambda i, j: (i, 0)),
        ],
        out_shape=[
            jax.ShapeDtypeStruct((t, n), BF16),
            jax.ShapeDtypeStruct((t, LANES), BF16),
        ],
        scratch_shapes=[pltpu.VMEM((tm, d), BF16)],
        compiler_params=_params(("parallel", "arbitrary")),
        name="in_proj",
    )(x2, mod4, mod4, g_norm3, w_lo, w_hi, w_kr, cr, sr, cm, sm1, sm2, g_cq3, g_ckv3)


UP_HEADS = MLA_HEADS


def _latent_up_kernel(cq_ref, ckv_ref, wq_ref, wk_ref, wv_ref, kr_ref, cm_ref, sm1_ref, sm2_ref,
                      q_ref, k_ref, v_ref):
    scale = (MLA_NOPE + MLA_ROPE) ** -0.5 * LOG2E
    cq = cq_ref[...]
    ckv = ckv_ref[...]
    cm, s1, s2 = cm_ref[...], sm1_ref[...], sm2_ref[...]
    kr = kr_ref[...]
    for hh in range(UP_HEADS):
        lo = hh * MLA_QK_PAD
        acc = jnp.dot(cq, wq_ref[:, lo:lo + MLA_QK_PAD], preferred_element_type=F32) * scale
        q_ref[:, lo:lo + MLA_NOPE] = acc[:, :MLA_NOPE].astype(q_ref.dtype)
        q_ref[:, lo + MLA_NOPE:lo + MLA_QK_PAD] = _rope_mla(acc[:, MLA_NOPE:], cm, s1, s2).astype(q_ref.dtype)
    ones = jnp.ones((cq.shape[0], MLA_V_PAD - MLA_DV), v_ref.dtype)
    pair_w = 2 * MLA_NOPE
    for h0 in range(0, UP_HEADS, 2):
        src = slice(h0 * MLA_NOPE, h0 * MLA_NOPE + pair_w)
        kn = jnp.dot(ckv, wk_ref[:, src], preferred_element_type=F32)
        vv = jnp.dot(ckv, wv_ref[:, src], preferred_element_type=F32)
        for hh in range(2):
            lo = (h0 + hh) * MLA_QK_PAD
            k_ref[:, lo:lo + MLA_NOPE] = kn[:, hh * MLA_NOPE:(hh + 1) * MLA_NOPE].astype(k_ref.dtype)
            k_ref[:, lo + MLA_NOPE:lo + MLA_QK_PAD] = kr
            vlo = (h0 + hh) * MLA_V_PAD
            v_ref[:, vlo:vlo + MLA_DV] = vv[:, hh * MLA_DV:(hh + 1) * MLA_DV].astype(v_ref.dtype)
            v_ref[:, vlo + MLA_DV:vlo + MLA_V_PAD] = ones


def _latent_up(p, kr, w_uq_p, w_k, w_v, tabs, layer):
    t = p.shape[0]
    tm = min(t, 512)
    _, _, cm, sm1, sm2 = tabs
    tab = pl.BlockSpec((tm, LANES), lambda i, j: (i, 0))

    def head_cols(width):
        return pl.BlockSpec((tm, UP_HEADS * width), lambda i, j: (i, j))

    return pl.pallas_call(
        _latent_up_kernel,
        grid=(t // tm, MLA_HEADS // UP_HEADS),
        in_specs=[
            pl.BlockSpec((tm, MLA_Q_RANK), lambda i, j: (i, C_CQ // MLA_Q_RANK)),
            pl.BlockSpec((tm, MLA_KV_RANK), lambda i, j: (i, C_CKV // MLA_KV_RANK)),
            pl.BlockSpec((None, MLA_Q_RANK, UP_HEADS * MLA_QK_PAD), lambda i, j: (layer, 0, j)),
            pl.BlockSpec((None, MLA_KV_RANK, UP_HEADS * MLA_NOPE), lambda i, j: (layer, 0, j)),
            pl.BlockSpec((None, MLA_KV_RANK, UP_HEADS * MLA_DV), lambda i, j: (layer, 0, j)),
            tab, tab, tab, tab,
        ],
        out_specs=[head_cols(MLA_QK_PAD), head_cols(MLA_QK_PAD), head_cols(MLA_V_PAD)],
        out_shape=[
            jax.ShapeDtypeStruct((t, MLA_HEADS * MLA_QK_PAD), BF16),
            jax.ShapeDtypeStruct((t, MLA_HEADS * MLA_QK_PAD), BF16),
            jax.ShapeDtypeStruct((t, MLA_HEADS * MLA_V_PAD), BF16),
        ],
        compiler_params=_params(("parallel", "arbitrary")),
        name="latent_up",
    )(p, p, w_uq_p, w_k, w_v, kr, cm, sm1, sm2)


RET_BLOCK = 256


RET_GROUP = 2


def _retention_kernel(q_ref, k_ref, v_ref, gate_ref, lg_ref, o_ref, state, dmat, xi, zeta):
    blk = dmat.shape[1]
    nblk = q_ref.shape[0] // blk

    state[...] = jnp.zeros_like(state)
    for hh in range(RET_GROUP):
        lg = lg_ref[hh]
        r = lax.broadcasted_iota(jnp.int32, (blk, blk), 0)
        c = lax.broadcasted_iota(jnp.int32, (blk, blk), 1)
        decay = jnp.exp(jnp.abs(r - c).astype(F32) * lg)
        dmat[hh] = jnp.where((c // CHUNK) <= (r // CHUNK), decay, 0.0)
        rx = lax.broadcasted_iota(jnp.int32, xi.shape[1:], 0).astype(F32)
        xi[hh] = jnp.exp((rx + 1.0) * lg[:, :RET_DV])
        rz = lax.broadcasted_iota(jnp.int32, zeta.shape[1:], 0).astype(F32)
        zeta[hh] = jnp.exp((blk - 1.0 - rz) * lg[:, :RET_DK])

    def body(j, carry):
        rows = pl.ds(pl.multiple_of(j * blk, blk), blk)
        for hh in range(RET_GROUP):
            qk_cols = slice(hh * RET_DK, (hh + 1) * RET_DK)
            v_cols = slice(hh * RET_DV, (hh + 1) * RET_DV)
            q = q_ref[rows, qk_cols]
            k = k_ref[rows, qk_cols]
            v = v_ref[rows, v_cols]
            s = lax.dot_general(q, k, (((1,), (1,)), ((), ())), preferred_element_type=F32)
            o = jnp.dot((s * dmat[hh]).astype(BF16), v, preferred_element_type=F32)
            st = state[hh]
            o = o + xi[hh] * jnp.dot(q, st.astype(BF16), preferred_element_type=F32)
            kz_t = (k.astype(F32) * zeta[hh]).T.astype(BF16)
            block_decay = jnp.exp(blk * lg_ref[hh][:, :RET_DV])
            state[hh] = st * block_decay + jnp.dot(kz_t, v, preferred_element_type=F32)

            mu = jnp.mean(o, axis=-1, keepdims=True)
            dlt = o - mu
            var = jnp.mean(dlt * dlt, axis=-1, keepdims=True)
            y = dlt * lax.rsqrt(var + EPS)
            o_ref[rows, v_cols] = y.astype(o_ref.dtype) * gate_ref[rows, v_cols]
        return carry

    lax.fori_loop(0, nblk, body, 0, unroll=4)


def _retention(p, bsz, seq):
    t = p.shape[0]
    blk = RET_BLOCK
    g = RET_GROUP
    log_gamma = jnp.log(1.0 - 2.0 ** (-5.0 - jnp.arange(RET_HEADS, dtype=F32)))
    lg = jnp.broadcast_to(log_gamma[:, None, None], (RET_HEADS, 1, blk))
    return pl.pallas_call(
        _retention_kernel,
        grid=(bsz, RET_HEADS // g),
        in_specs=[
            pl.BlockSpec((seq, g * RET_DK), lambda b, h: (b, C_RQ // (g * RET_DK) + h)),
            pl.BlockSpec((seq, g * RET_DK), lambda b, h: (b, C_RK // (g * RET_DK) + h)),
            pl.BlockSpec((seq, g * RET_DV), lambda b, h: (b, C_RV // (g * RET_DV) + h)),
            pl.BlockSpec((seq, g * RET_DV), lambda b, h: (b, C_RG // (g * RET_DV) + h)),
            pl.BlockSpec((g, 1, blk), lambda b, h: (h, 0, 0)),
        ],
        out_specs=pl.BlockSpec((seq, g * RET_DV), lambda b, h: (b, h)),
        out_shape=jax.ShapeDtypeStruct((t, RET_V_W), BF16),
        scratch_shapes=[
            pltpu.VMEM((g, RET_DK, RET_DV), F32),
            pltpu.VMEM((g, blk, blk), F32),
            pltpu.VMEM((g, blk, RET_DV), F32),
            pltpu.VMEM((g, blk, RET_DK), F32),
        ],
        compiler_params=_params(("parallel", "parallel")),
        name="retention",
    )(p, p, p, p, lg)


ATT_BLOCK = 512
ATT_HEADS = 2
ATT_REGION_PAIRS = 36


def _attention_kernel(q_ref, k_ref, v_ref, gate_ref, bias_ref, o_ref, s_scr, m_scr, acc_scr):
    seq = q_ref.shape[0]
    tq = s_scr.shape[1]
    nq = seq // tq

    def windows(qi, kb):
        if kb == qi and tq % (2 * CHUNK) == 0:
            return ((0, tq // 2, tq // 2), (tq // 2, tq, tq))
        return ((0, tq, tq),)

    def slot(next_pair, pair):
        for hh in range(ATT_HEADS):
            qk_cols = slice(hh * MLA_QK_PAD, (hh + 1) * MLA_QK_PAD)
            v_cols = slice(hh * MLA_V_PAD, (hh + 1) * MLA_V_PAD)
            m_cols = slice(hh * LANES, (hh + 1) * LANES)
            if pair is not None:
                qi, kb = pair
                for r0, r1, nk in windows(qi, kb):
                    s = s_scr[hh, r0:r1, :nk]
                    if kb == qi:
                        s = s + bias_ref[r0:r1, :nk]
                    m_cur = jnp.max(s, axis=-1, keepdims=True)
                    if kb == 0:
                        m_new = jnp.broadcast_to(m_cur, (r1 - r0, LANES))
                    else:
                        m_old = m_scr[r0:r1, m_cols]
                        m_new = jnp.maximum(m_old, m_cur)
                    p = jnp.exp2(s - jnp.concatenate([m_new] * (nk // LANES), axis=1))
                    v = v_ref[kb * tq:kb * tq + nk, v_cols]
                    pv = jnp.dot(p.astype(BF16), v, preferred_element_type=F32)
                    if kb == 0:
                        acc = pv
                    else:
                        alpha = jnp.concatenate([jnp.exp2(m_old - m_new)] * (MLA_V_PAD // LANES), axis=1)
                        acc = acc_scr[r0:r1, v_cols] * alpha + pv
                    if kb == qi:
                        out_rows = slice(qi * tq + r0, qi * tq + r1)
                        cols = slice(hh * MLA_DV, (hh + 1) * MLA_DV)
                        o = acc[:, :MLA_DV] / acc[:, MLA_DV:]
                        o_ref[out_rows, cols] = (o * gate_ref[out_rows, cols].astype(F32)).astype(o_ref.dtype)
                    else:
                        acc_scr[r0:r1, v_cols] = acc
                        m_scr[r0:r1, m_cols] = m_new
            if next_pair is not None:
                nqi, nkb = next_pair
                for r0, r1, nk in windows(nqi, nkb):
                    q = q_ref[nqi * tq + r0:nqi * tq + r1, qk_cols]
                    k = k_ref[nkb * tq:nkb * tq + nk, qk_cols]
                    s_scr[hh, r0:r1, :nk] = lax.dot_general(
                        q, k, (((1,), (1,)), ((), ())), preferred_element_type=F32)

    pairs = [(qi, kb) for qi in range(nq) for kb in range(qi + 1)]
    regions, current = [], []
    for qi in range(nq):
        if current and len(current) + qi + 1 > ATT_REGION_PAIRS:
            regions.append(current)
            current = []
        current += [(qi, kb) for kb in range(qi + 1)]
    regions.append(current)

    for r, region in enumerate(regions):
        @pl.when(pl.program_id(0) + r >= 0)
        def _(r=r, region=region):
            if r == 0:
                slot(pairs[0], None)
            for pair in region:
                n = pairs.index(pair)
                slot(pairs[n + 1] if n + 1 < len(pairs) else None, pair)


def _attention(qc, kc, v, p, bsz, seq):
    t = qc.shape[0]
    tq = min(seq, ATT_BLOCK)
    nh = ATT_HEADS
    r = lax.broadcasted_iota(jnp.int32, (tq, tq), 0) // CHUNK
    c = lax.broadcasted_iota(jnp.int32, (tq, tq), 1) // CHUNK
    bias = jnp.where(c <= r, 0.0, NEG_INF * LOG2E).astype(F32)
    return pl.pallas_call(
        _attention_kernel,
        grid=(bsz, MLA_HEADS // nh),
        in_specs=[
            pl.BlockSpec((seq, nh * MLA_QK_PAD), lambda b, h: (b, h)),
            pl.BlockSpec((seq, nh * MLA_QK_PAD), lambda b, h: (b, h)),
            pl.BlockSpec((seq, nh * MLA_V_PAD), lambda b, h: (b, h)),
            pl.BlockSpec((seq, nh * MLA_DV), lambda b, h: (b, C_MG // (nh * MLA_DV) + h)),
            pl.BlockSpec((tq, tq), lambda b, h: (0, 0)),
        ],
        out_specs=pl.BlockSpec((seq, nh * MLA_DV), lambda b, h: (b, h)),
        out_shape=jax.ShapeDtypeStruct((t, MLA_V_W), BF16),
        scratch_shapes=[
            pltpu.VMEM((nh, tq, tq), F32),
            pltpu.VMEM((tq, nh * LANES), F32),
            pltpu.VMEM((tq, nh * MLA_V_PAD), F32),
        ],
        compiler_params=_params(("parallel", "parallel")),
        name="attention",
    )(qc, kc, v, p, bias)


PROJ_CHUNK = 256


def _merge_kernel(a_ref, b_ref, wr_ref, wm_ref, ga_ref, gb_ref, o_ref):
    for c0 in range(0, o_ref.shape[1], PROJ_CHUNK):
        cols = slice(c0, c0 + PROJ_CHUNK)
        y_ret = jnp.dot(a_ref[...], wr_ref[:, cols], preferred_element_type=F32)
        y_mla = jnp.dot(b_ref[...], wm_ref[:, cols], preferred_element_type=F32)
        merged = ga_ref[:, cols].astype(F32) * y_ret + gb_ref[:, cols].astype(F32) * y_mla
        o_ref[:, cols] = merged.astype(o_ref.dtype)


def _merge(a, bm, p, w_ret, w_mla, layer):
    t, d_in = a.shape
    d = w_ret.shape[2]
    tm = min(t, 1024)
    tn = 1024
    return pl.pallas_call(
        _merge_kernel,
        grid=(t // tm, d // tn),
        in_specs=[
            pl.BlockSpec((tm, d_in), lambda i, j: (i, 0)),
            pl.BlockSpec((tm, d_in), lambda i, j: (i, 0)),
            pl.BlockSpec((None, d_in, tn), lambda i, j: (layer, 0, j)),
            pl.BlockSpec((None, d_in, tn), lambda i, j: (layer, 0, j)),
            pl.BlockSpec((tm, tn), lambda i, j: (i, C_BG // tn + j)),
            pl.BlockSpec((tm, tn), lambda i, j: (i, (C_BG + d) // tn + j)),
        ],
        out_specs=pl.BlockSpec((tm, tn), lambda i, j: (i, j)),
        out_shape=jax.ShapeDtypeStruct((t, d), BF16),
        compiler_params=_params(("parallel", "arbitrary")),
        name="merge_proj",
    )(a, bm, w_ret, w_mla, p, p)


def _out_kernel(m_ref, w_ref, x_ref, gate_ref, o_ref):
    for c0 in range(0, o_ref.shape[1], PROJ_CHUNK):
        cols = slice(c0, c0 + PROJ_CHUNK)
        out = jnp.dot(m_ref[...], w_ref[:, cols], preferred_element_type=F32)
        o_ref[:, cols] = x_ref[:, cols] + gate_ref[:, cols] * out


def _out_norm_kernel(m_ref, w_ref, x_ref, gate_ref, g_ref, o_ref):
    _out_kernel(m_ref, w_ref, x_ref, gate_ref, o_ref)
    y = o_ref[...]
    inv = lax.rsqrt(jnp.mean(y * y, axis=-1, keepdims=True) + EPS)
    o_ref[...] = (y * inv) * g_ref[...]


def _out_proj(merged, w_out, x2, mod4, layer, seq, g_final=None):
    t, d = x2.shape
    fuse_norm = g_final is not None
    tm = min(seq, 512)
    per_b = seq // tm
    tn = d
    gate_blk = 2 * d // tn
    in_specs = [
        pl.BlockSpec((tm, d), lambda i, j: (i, 0)),
        pl.BlockSpec((None, d, tn), lambda i, j: (layer, 0, j)),
        pl.BlockSpec((tm, tn), lambda i, j: (i, j)),
        pl.BlockSpec((None, None, 1, tn), lambda i, j: (layer, i // per_b, 0, gate_blk + j)),
    ]
    args = [merged, w_out, x2, mod4]
    if fuse_norm:
        in_specs.append(pl.BlockSpec((1, d), lambda i, j: (0, 0)))
        args.append(g_final.reshape(1, d))
    return pl.pallas_call(
        _out_norm_kernel if fuse_norm else _out_kernel,
        grid=(t // tm, d // tn),
        in_specs=in_specs,
        out_specs=pl.BlockSpec((tm, tn), lambda i, j: (i, j)),
        out_shape=jax.ShapeDtypeStruct((t, d), F32),
        compiler_params=_params(("parallel", "arbitrary")),
        name="out_norm" if fuse_norm else "out_proj",
    )(*args)


def kernel(x, c, positions, w_mod, b_mod, g_norm, w_in, g_cq, g_ckv, w_uq, w_ukv,
           w_ret_proj, w_mla_proj, w_out, g_final):
    bsz, seq, d = x.shape
    depth = w_in.shape[0]
    t = bsz * seq
    assert seq % RET_BLOCK == 0 and seq % min(seq, ATT_BLOCK) == 0 and d % IN_TN == 0

    kr_lo = C_CKV + MLA_KV_RANK
    w_in_b = w_in.astype(BF16)
    w_lo = w_in_b
    w_hi = w_in_b[:, :, kr_lo + MLA_ROPE:]
    w_kr = jnp.pad(w_in[:, :, kr_lo:kr_lo + MLA_ROPE], ((0, 0), (0, 0), (0, LANES - MLA_ROPE))).astype(BF16)
    w_uq_p = jnp.pad(
        w_uq.reshape(depth, MLA_Q_RANK, MLA_HEADS, MLA_NOPE + MLA_ROPE),
        ((0, 0), (0, 0), (0, 0), (0, MLA_QK_PAD - MLA_NOPE - MLA_ROPE)),
    ).reshape(depth, MLA_Q_RANK, MLA_HEADS * MLA_QK_PAD).astype(BF16)
    w_ukv4 = w_ukv.reshape(depth, MLA_KV_RANK, MLA_HEADS, MLA_NOPE + MLA_DV)
    w_k = w_ukv4[..., :MLA_NOPE].reshape(depth, MLA_KV_RANK, MLA_HEADS * MLA_NOPE).astype(BF16)
    w_v = w_ukv4[..., MLA_NOPE:].reshape(depth, MLA_KV_RANK, MLA_V_W).astype(BF16)
    w_ret = w_ret_proj.astype(BF16)
    w_mla = w_mla_proj.astype(BF16)
    w_o = w_out.astype(BF16)

    tabs = _rope_tables(positions)
    mod = _modulation(c, w_mod, b_mod)
    mod4 = mod.reshape(depth, mod.shape[1], 1, 3 * d)
    g_norm3 = g_norm.reshape(depth, 1, d)
    g_cq3 = g_cq.reshape(depth, 1, MLA_Q_RANK)
    g_ckv3 = g_ckv.reshape(depth, 1, MLA_KV_RANK)

    x2 = x.reshape(t, d)
    for layer in range(depth):
        p, kr = _in_proj(x2, mod4, g_norm3, w_lo, w_hi, w_kr, tabs, g_cq3, g_ckv3, layer, seq)
        qc, kc, v = _latent_up(p, kr, w_uq_p, w_k, w_v, tabs, layer)
        a = _retention(p, bsz, seq)
        bm = _attention(qc, kc, v, p, bsz, seq)
        merged = _merge(a, bm, p, w_ret, w_mla, layer)
        last = layer == depth - 1
        x2 = _out_proj(merged, w_o, x2, mod4, layer, seq, g_final if last else None)
    return x2.reshape(bsz, seq, d)
```

```python
import jax
import jax.numpy as jnp
from jax import lax
from jax.experimental import pallas as pl
from jax.experimental.pallas import tpu as pltpu

F32 = jnp.float32
BF16 = jnp.bfloat16

CHUNK = 64
EPS = 1e-6
NEG_INF = -1e30
ROPE_BASE = 10000.0
RET_HEADS = 8
RET_DK = 128
RET_DV = 256
RET_QK_W = RET_HEADS * RET_DK
RET_V_W = RET_HEADS * RET_DV
MLA_HEADS = 16
MLA_Q_RANK = 512
MLA_KV_RANK = 512
MLA_NOPE = 128
MLA_ROPE = 64
MLA_DV = 128
MLA_V_W = MLA_HEADS * MLA_DV
MLA_QK_PAD = 256
MLA_V_PAD = 2 * MLA_DV
LOG2E = 1.4426950408889634

LANES = 128
VMEM_LIMIT = 56 * 1024 * 1024

C_RQ = 0
C_RK = C_RQ + RET_QK_W
C_RV = C_RK + RET_QK_W
C_RG = C_RV + RET_V_W
C_CQ = C_RG + RET_V_W
C_CKV = C_CQ + MLA_Q_RANK
C_MG = C_CKV + MLA_KV_RANK
C_BG = C_MG + MLA_V_W
IN_TN = 1024
IN_CHUNK = 256


def _params(sem):
    return pltpu.CompilerParams(dimension_semantics=sem, vmem_limit_bytes=VMEM_LIMIT)


def _sigmoid(x):
    return 0.5 * jnp.tanh(0.5 * x) + 0.5


def _rope_tables_kernel(pos_ref, inv_ref, cr_ref, sr_ref, cm_ref, sm1_ref, sm2_ref):
    pos = pos_ref[...].astype(F32)
    lane = lax.broadcasted_iota(jnp.int32, cr_ref.shape, 1)
    ang = pos * inv_ref[...]
    c = jnp.cos(ang)
    s = jnp.sin(ang)
    ret_half = RET_DK // 2
    half = MLA_ROPE // 2
    c_up, s_up = pltpu.roll(c, LANES - ret_half, 1), pltpu.roll(s, LANES - ret_half, 1)
    c_up2, s_up2 = pltpu.roll(c, LANES - half, 1), pltpu.roll(s, LANES - half, 1)
    cr_ref[...] = jnp.where(lane < ret_half, c, pltpu.roll(c, ret_half, 1))
    sr_ref[...] = jnp.where(lane < ret_half, -s, pltpu.roll(s, ret_half, 1))
    in_lo = lane < half
    in_hi = (lane >= half) & (lane < MLA_ROPE)
    cm_ref[...] = jnp.where(in_lo, c_up, jnp.where(in_hi, c_up2, 0.0))
    sm1_ref[...] = jnp.where(in_lo, -s_up, 0.0)
    sm2_ref[...] = jnp.where(in_hi, s_up2, 0.0)


def _rope_tables(positions):
    t = positions.size
    tm = min(t, 1024)
    inv_r = 1.0 / (ROPE_BASE ** (jnp.arange(0, RET_DK, 2, dtype=F32) / RET_DK))
    inv_m = 1.0 / (ROPE_BASE ** (jnp.arange(0, MLA_ROPE, 2, dtype=F32) / MLA_ROPE))
    inv = jnp.concatenate([inv_r, inv_m, jnp.zeros((LANES - RET_DK // 2 - MLA_ROPE // 2,), F32)])[None, :]
    row = pl.BlockSpec((tm, LANES), lambda i: (i, 0))
    return pl.pallas_call(
        _rope_tables_kernel,
        grid=(t // tm,),
        in_specs=[pl.BlockSpec((tm, 1), lambda i: (i, 0)), pl.BlockSpec((1, LANES), lambda i: (0, 0))],
        out_specs=[row] * 5,
        out_shape=[jax.ShapeDtypeStruct((t, LANES), F32)] * 5,
        compiler_params=_params(("parallel",)),
        name="rope_tables",
    )(positions.reshape(t, 1), inv)


def _mod_kernel(c_ref, w_ref, b_ref, o_ref):
    c = c_ref[...]
    ca = (c * _sigmoid(c)).astype(BF16)
    o_ref[...] = jnp.dot(ca, w_ref[...].astype(BF16), preferred_element_type=F32) + b_ref[...]


def _modulation(c, w_mod, b_mod):
    depth, d, n = w_mod.shape
    bsz = c.shape[0]
    rows = 8
    cp = jnp.pad(c, ((0, rows - bsz), (0, 0)))
    tn = 512
    return pl.pallas_call(
        _mod_kernel,
        grid=(depth, n // tn),
        in_specs=[
            pl.BlockSpec((rows, d), lambda l, j: (0, 0)),
            pl.BlockSpec((None, d, tn), lambda l, j: (l, 0, j)),
            pl.BlockSpec((None, 1, tn), lambda l, j: (l, 0, j)),
        ],
        out_specs=pl.BlockSpec((None, rows, tn), lambda l, j: (l, 0, j)),
        out_shape=jax.ShapeDtypeStruct((depth, rows, n), F32),
        compiler_params=_params(("parallel", "parallel")),
        name="modulation",
    )(cp, w_mod, b_mod.reshape(depth, 1, n))


def _rope_ret(a, cos, sin):
    outs = []
    for hh in range(a.shape[1] // RET_DK):
        t = a[:, hh * RET_DK:(hh + 1) * RET_DK]
        outs.append(t * cos + pltpu.roll(t, RET_DK // 2, 1) * sin)
    return jnp.concatenate(outs, axis=1) if len(outs) > 1 else outs[0]


def _rope_mla(t, cos, sin_lo, sin_hi):
    half = MLA_ROPE // 2
    return t * cos + pltpu.roll(t, LANES - half, 1) * sin_lo + pltpu.roll(t, half, 1) * sin_hi


def _in_proj_kernel(x_ref, shift_ref, scale_ref, g_ref, wlo_ref, whi_ref, wkr_ref, cr_ref, sr_ref,
                    cm_ref, sm1_ref, sm2_ref, gcq_ref, gckv_ref, p_ref, kr_ref, h_scr):
    j = pl.program_id(1)

    @pl.when(j == 0)
    def _():
        x = x_ref[...]
        inv = lax.rsqrt(jnp.mean(x * x, axis=-1, keepdims=True) + EPS)
        h = (x * inv) * g_ref[...]
        h = h * (1.0 + scale_ref[...]) + shift_ref[...]
        hb = h.astype(BF16)
        h_scr[...] = hb
        kr = jnp.dot(hb, wkr_ref[...], preferred_element_type=F32)
        kr_ref[...] = _rope_mla(kr, cm_ref[...], sm1_ref[...], sm2_ref[...]).astype(kr_ref.dtype)

    def tile_range(lo, width):
        return (j >= lo // IN_TN) & (j < (lo + width) // IN_TN)

    def project(w_ref, epilogues, chunk=IN_CHUNK):
        for idx, c0 in enumerate(range(0, IN_TN, chunk)):
            acc = jnp.dot(h_scr[...], w_ref[:, c0:c0 + chunk], preferred_element_type=F32)
            p_ref[:, c0:c0 + chunk] = epilogues[idx % len(epilogues)](acc).astype(p_ref.dtype)

    def rope(acc):
        return _rope_ret(acc, cr_ref[...], sr_ref[...])

    def latent_norm(g_latent_ref):
        def apply(acc):
            inv = lax.rsqrt(jnp.mean(acc * acc, axis=-1, keepdims=True) + EPS)
            return (acc * inv) * g_latent_ref[...]
        return apply

    @pl.when(tile_range(C_RQ, RET_QK_W))
    def _():
        project(wlo_ref, [rope])

    @pl.when(tile_range(C_RK, RET_QK_W))
    def _():
        project(wlo_ref, [lambda acc: rope(acc) * (RET_DK ** -0.5)])

    @pl.when(tile_range(C_RV, RET_V_W))
    def _():
        project(wlo_ref, [lambda acc: acc])

    def silu(acc):
        return acc * _sigmoid(acc)

    @pl.when(tile_range(C_RG, RET_V_W))
    def _():
        project(wlo_ref, [silu])

    @pl.when(tile_range(C_MG, MLA_V_W))
    def _():
        project(whi_ref, [silu])

    @pl.when(tile_range(C_CQ, MLA_Q_RANK + MLA_KV_RANK))
    def _():
        project(wlo_ref, [latent_norm(gcq_ref), latent_norm(gckv_ref)], chunk=MLA_Q_RANK)

    @pl.when(j >= C_BG // IN_TN)
    def _():
        project(whi_ref, [_sigmoid])


def _in_proj(x2, mod4, g_norm3, w_lo, w_hi, w_kr, tabs, g_cq3, g_ckv3, layer, seq):
    t, d = x2.shape
    lo_tiles = C_MG // IN_TN
    n = C_MG + w_hi.shape[2]
    tm = min(seq, 1024)
    per_b = seq // tm
    cr, sr, cm, sm1, sm2 = tabs
    tab = pl.BlockSpec((tm, LANES), lambda i, j: (i, 0))
    return pl.pallas_call(
        _in_proj_kernel,
        grid=(t // tm, n // IN_TN),
        in_specs=[
            pl.BlockSpec((tm, d), lambda i, j: (i, 0)),
            pl.BlockSpec((None, None, 1, d), lambda i, j: (layer, i // per_b, 0, 0)),
            pl.BlockSpec((None, None, 1, d), lambda i, j: (layer, i // per_b, 0, 1)),
            pl.BlockSpec((None, 1, d), lambda i, j: (layer, 0, 0)),
            pl.BlockSpec((None, d, IN_TN), lambda i, j: (layer, 0, jnp.minimum(j, lo_tiles - 1))),
            pl.BlockSpec((None, d, IN_TN), lambda i, j: (layer, 0, jnp.maximum(j - lo_tiles, 0))),
            pl.BlockSpec((None, d, LANES), lambda i, j: (layer, 0, 0)),
            tab, tab, tab, tab, tab,
            pl.BlockSpec((None, 1, MLA_Q_RANK), lambda i, j: (layer, 0, 0)),
            pl.BlockSpec((None, 1, MLA_KV_RANK), lambda i, j: (layer, 0, 0)),
        ],
        out_specs=[
            pl.BlockSpec((tm, IN_TN), lambda i, j: (i, j)),
            pl.BlockSpec((tm, LANES), lambda i, j: (i, 0)),
        ],
        out_shape=[
            jax.ShapeDtypeStruct((t, n), BF16),
            jax.ShapeDtypeStruct((t, LANES), BF16),
        ],
        scratch_shapes=[pltpu.VMEM((tm, d), BF16)],
        compiler_params=_params(("parallel", "arbitrary")),
        name="in_proj",
    )(x2, mod4, mod4, g_norm3, w_lo, w_hi, w_kr, cr, sr, cm, sm1, sm2, g_cq3, g_ckv3)


UP_HEADS = MLA_HEADS


def _latent_up_kernel(cq_ref, ckv_ref, wq_ref, wk_ref, wv_ref, kr_ref, cm_ref, sm1_ref, sm2_ref,
                      q_ref, k_ref, v_ref):
    scale = (MLA_NOPE + MLA_ROPE) ** -0.5 * LOG2E
    cq = cq_ref[...]
    ckv = ckv_ref[...]
    cm, s1, s2 = cm_ref[...], sm1_ref[...], sm2_ref[...]
    kr = kr_ref[...]
    for hh in range(UP_HEADS):
        lo = hh * MLA_QK_PAD
        acc = jnp.dot(cq, wq_ref[:, lo:lo + MLA_QK_PAD], preferred_element_type=F32) * scale
        q_ref[:, lo:lo + MLA_NOPE] = acc[:, :MLA_NOPE].astype(q_ref.dtype)
        q_ref[:, lo + MLA_NOPE:lo + MLA_QK_PAD] = _rope_mla(acc[:, MLA_NOPE:], cm, s1, s2).astype(q_ref.dtype)
    ones = jnp.ones((cq.shape[0], MLA_V_PAD - MLA_DV), v_ref.dtype)
    pair_w = 2 * MLA_NOPE
    for h0 in range(0, UP_HEADS, 2):
        src = slice(h0 * MLA_NOPE, h0 * MLA_NOPE + pair_w)
        kn = jnp.dot(ckv, wk_ref[:, src], preferred_element_type=F32)
        vv = jnp.dot(ckv, wv_ref[:, src], preferred_element_type=F32)
        for hh in range(2):
            lo = (h0 + hh) * MLA_QK_PAD
            k_ref[:, lo:lo + MLA_NOPE] = kn[:, hh * MLA_NOPE:(hh + 1) * MLA_NOPE].astype(k_ref.dtype)
            k_ref[:, lo + MLA_NOPE:lo + MLA_QK_PAD] = kr
            vlo = (h0 + hh) * MLA_V_PAD
            v_ref[:, vlo:vlo + MLA_DV] = vv[:, hh * MLA_DV:(hh + 1) * MLA_DV].astype(v_ref.dtype)
            v_ref[:, vlo + MLA_DV:vlo + MLA_V_PAD] = ones


def _latent_up(p, kr, w_uq_p, w_k, w_v, tabs, layer):
    t = p.shape[0]
    tm = min(t, 512)
    _, _, cm, sm1, sm2 = tabs
    tab = pl.BlockSpec((tm, LANES), lambda i, j: (i, 0))

    def head_cols(width):
        return pl.BlockSpec((tm, UP_HEADS * width), lambda i, j: (i, j))

    return pl.pallas_call(
        _latent_up_kernel,
        grid=(t // tm, MLA_HEADS // UP_HEADS),
        in_specs=[
            pl.BlockSpec((tm, MLA_Q_RANK), lambda i, j: (i, C_CQ // MLA_Q_RANK)),
            pl.BlockSpec((tm, MLA_KV_RANK), lambda i, j: (i, C_CKV // MLA_KV_RANK)),
            pl.BlockSpec((None, MLA_Q_RANK, UP_HEADS * MLA_QK_PAD), lambda i, j: (layer, 0, j)),
            pl.BlockSpec((None, MLA_KV_RANK, UP_HEADS * MLA_NOPE), lambda i, j: (layer, 0, j)),
            pl.BlockSpec((None, MLA_KV_RANK, UP_HEADS * MLA_DV), lambda i, j: (layer, 0, j)),
            tab, tab, tab, tab,
        ],
        out_specs=[head_cols(MLA_QK_PAD), head_cols(MLA_QK_PAD), head_cols(MLA_V_PAD)],
        out_shape=[
            jax.ShapeDtypeStruct((t, MLA_HEADS * MLA_QK_PAD), BF16),
            jax.ShapeDtypeStruct((t, MLA_HEADS * MLA_QK_PAD), BF16),
            jax.ShapeDtypeStruct((t, MLA_HEADS * MLA_V_PAD), BF16),
        ],
        compiler_params=_params(("parallel", "arbitrary")),
        name="latent_up",
    )(p, p, w_uq_p, w_k, w_v, kr, cm, sm1, sm2)


RET_BLOCK = 256


RET_GROUP = 2


def _retention_kernel(q_ref, k_ref, v_ref, gate_ref, lg_ref, o_ref, state, dmat, xi, zeta):
    blk = dmat.shape[1]
    nblk = q_ref.shape[0] // blk

    state[...] = jnp.zeros_like(state)
    for hh in range(RET_GROUP):
        lg = lg_ref[hh]
        r = lax.broadcasted_iota(jnp.int32, (blk, blk), 0)
        c = lax.broadcasted_iota(jnp.int32, (blk, blk), 1)
        decay = jnp.exp(jnp.abs(r - c).astype(F32) * lg)
        dmat[hh] = jnp.where((c // CHUNK) <= (r // CHUNK), decay, 0.0)
        rx = lax.broadcasted_iota(jnp.int32, xi.shape[1:], 0).astype(F32)
        xi[hh] = jnp.exp((rx + 1.0) * lg[:, :RET_DV])
        rz = lax.broadcasted_iota(jnp.int32, zeta.shape[1:], 0).astype(F32)
        zeta[hh] = jnp.exp((blk - 1.0 - rz) * lg[:, :RET_DK])

    def body(j, carry):
        rows = pl.ds(pl.multiple_of(j * blk, blk), blk)
        for hh in range(RET_GROUP):
            qk_cols = slice(hh * RET_DK, (hh + 1) * RET_DK)
            v_cols = slice(hh * RET_DV, (hh + 1) * RET_DV)
            q = q_ref[rows, qk_cols]
            k = k_ref[rows, qk_cols]
            v = v_ref[rows, v_cols]
            s = lax.dot_general(q, k, (((1,), (1,)), ((), ())), preferred_element_type=F32)
            o = jnp.dot((s * dmat[hh]).astype(BF16), v, preferred_element_type=F32)
            st = state[hh]
            o = o + xi[hh] * jnp.dot(q, st.astype(BF16), preferred_element_type=F32)
            kz_t = (k.astype(F32) * zeta[hh]).T.astype(BF16)
            block_decay = jnp.exp(blk * lg_ref[hh][:, :RET_DV])
            state[hh] = st * block_decay + jnp.dot(kz_t, v, preferred_element_type=F32)

            mu = jnp.mean(o, axis=-1, keepdims=True)
            dlt = o - mu
            var = jnp.mean(dlt * dlt, axis=-1, keepdims=True)
            y = dlt * lax.rsqrt(var + EPS)
            o_ref[rows, v_cols] = y.astype(o_ref.dtype) * gate_ref[rows, v_cols]
        return carry

    lax.fori_loop(0, nblk, body, 0, unroll=True)


def _retention(p, bsz, seq):
    t = p.shape[0]
    blk = RET_BLOCK
    g = RET_GROUP
    log_gamma = jnp.log(1.0 - 2.0 ** (-5.0 - jnp.arange(RET_HEADS, dtype=F32)))
    lg = jnp.broadcast_to(log_gamma[:, None, None], (RET_HEADS, 1, blk))
    return pl.pallas_call(
        _retention_kernel,
        grid=(bsz, RET_HEADS // g),
        in_specs=[
            pl.BlockSpec((seq, g * RET_DK), lambda b, h: (b, C_RQ // (g * RET_DK) + h)),
            pl.BlockSpec((seq, g * RET_DK), lambda b, h: (b, C_RK // (g * RET_DK) + h)),
            pl.BlockSpec((seq, g * RET_DV), lambda b, h: (b, C_RV // (g * RET_DV) + h)),
            pl.BlockSpec((seq, g * RET_DV), lambda b, h: (b, C_RG // (g * RET_DV) + h)),
            pl.BlockSpec((g, 1, blk), lambda b, h: (h, 0, 0)),
        ],
        out_specs=pl.BlockSpec((seq, g * RET_DV), lambda b, h: (b, h)),
        out_shape=jax.ShapeDtypeStruct((t, RET_V_W), BF16),
        scratch_shapes=[
            pltpu.VMEM((g, RET_DK, RET_DV), F32),
            pltpu.VMEM((g, blk, blk), F32),
            pltpu.VMEM((g, blk, RET_DV), F32),
            pltpu.VMEM((g, blk, RET_DK), F32),
        ],
        compiler_params=_params(("parallel", "parallel")),
        name="retention",
    )(p, p, p, p, lg)


ATT_BLOCK = 512
ATT_HEADS = 2
ATT_REGION_PAIRS = 36


def _attention_kernel(q_ref, k_ref, v_ref, gate_ref, bias_ref, o_ref, s_scr, m_scr, acc_scr):
    seq = q_ref.shape[0]
    tq = s_scr.shape[1]
    nq = seq // tq

    def windows(qi, kb):
        if kb == qi and tq % (2 * CHUNK) == 0:
            return ((0, tq // 2, tq // 2), (tq // 2, tq, tq))
        return ((0, tq, tq),)

    def slot(next_pair, pair):
        for hh in range(ATT_HEADS):
            qk_cols = slice(hh * MLA_QK_PAD, (hh + 1) * MLA_QK_PAD)
            v_cols = slice(hh * MLA_V_PAD, (hh + 1) * MLA_V_PAD)
            m_cols = slice(hh * LANES, (hh + 1) * LANES)
            if pair is not None:
                qi, kb = pair
                for r0, r1, nk in windows(qi, kb):
                    s = s_scr[hh, r0:r1, :nk]
                    if kb == qi:
                        s = s + bias_ref[r0:r1, :nk]
                    m_cur = jnp.max(s, axis=-1, keepdims=True)
                    if kb == 0:
                        m_new = jnp.broadcast_to(m_cur, (r1 - r0, LANES))
                    else:
                        m_old = m_scr[r0:r1, m_cols]
                        m_new = jnp.maximum(m_old, m_cur)
                    p = jnp.exp2(s - jnp.concatenate([m_new] * (nk // LANES), axis=1))
                    v = v_ref[kb * tq:kb * tq + nk, v_cols]
                    pv = jnp.dot(p.astype(BF16), v, preferred_element_type=F32)
                    if kb == 0:
                        acc = pv
                    else:
                        alpha = jnp.concatenate([jnp.exp2(m_old - m_new)] * (MLA_V_PAD // LANES), axis=1)
                        acc = acc_scr[r0:r1, v_cols] * alpha + pv
                    if kb == qi:
                        out_rows = slice(qi * tq + r0, qi * tq + r1)
                        cols = slice(hh * MLA_DV, (hh + 1) * MLA_DV)
                        o = acc[:, :MLA_DV] / acc[:, MLA_DV:]
                        o_ref[out_rows, cols] = (o * gate_ref[out_rows, cols].astype(F32)).astype(o_ref.dtype)
                    else:
                        acc_scr[r0:r1, v_cols] = acc
                        m_scr[r0:r1, m_cols] = m_new
            if next_pair is not None:
                nqi, nkb = next_pair
                for r0, r1, nk in windows(nqi, nkb):
                    q = q_ref[nqi * tq + r0:nqi * tq + r1, qk_cols]
                    k = k_ref[nkb * tq:nkb * tq + nk, qk_cols]
                    s_scr[hh, r0:r1, :nk] = lax.dot_general(
                        q, k, (((1,), (1,)), ((), ())), preferred_element_type=F32)

    pairs = [(qi, kb) for qi in range(nq) for kb in range(qi + 1)]
    regions, current = [], []
    for qi in range(nq):
        if current and len(current) + qi + 1 > ATT_REGION_PAIRS:
            regions.append(current)
            current = []
        current += [(qi, kb) for kb in range(qi + 1)]
    regions.append(current)

    for r, region in enumerate(regions):
        @pl.when(pl.program_id(0) + r >= 0)
        def _(r=r, region=region):
            if r == 0:
                slot(pairs[0], None)
            for pair in region:
                n = pairs.index(pair)
                slot(pairs[n + 1] if n + 1 < len(pairs) else None, pair)


def _attention(qc, kc, v, p, bsz, seq):
    t = qc.shape[0]
    tq = min(seq, ATT_BLOCK)
    nh = ATT_HEADS
    r = lax.broadcasted_iota(jnp.int32, (tq, tq), 0) // CHUNK
    c = lax.broadcasted_iota(jnp.int32, (tq, tq), 1) // CHUNK
    bias = jnp.where(c <= r, 0.0, NEG_INF * LOG2E).astype(F32)
    return pl.pallas_call(
        _attention_kernel,
        grid=(bsz, MLA_HEADS // nh),
        in_specs=[
            pl.BlockSpec((seq, nh * MLA_QK_PAD), lambda b, h: (b, h)),
            pl.BlockSpec((seq, nh * MLA_QK_PAD), lambda b, h: (b, h)),
            pl.BlockSpec((seq, nh * MLA_V_PAD), lambda b, h: (b, h)),
            pl.BlockSpec((seq, nh * MLA_DV), lambda b, h: (b, C_MG // (nh * MLA_DV) + h)),
            pl.BlockSpec((tq, tq), lambda b, h: (0, 0)),
        ],
        out_specs=pl.BlockSpec((seq, nh * MLA_DV), lambda b, h: (b, h)),
        out_shape=jax.ShapeDtypeStruct((t, MLA_V_W), BF16),
        scratch_shapes=[
            pltpu.VMEM((nh, tq, tq), F32),
            pltpu.VMEM((tq, nh * LANES), F32),
            pltpu.VMEM((tq, nh * MLA_V_PAD), F32),
        ],
        compiler_params=_params(("parallel", "parallel")),
        name="attention",
    )(qc, kc, v, p, bias)


PROJ_CHUNK = 256


def _merge_kernel(a_ref, b_ref, wr_ref, wm_ref, ga_ref, gb_ref, o_ref):
    for c0 in range(0, o_ref.shape[1], PROJ_CHUNK):
        cols = slice(c0, c0 + PROJ_CHUNK)
        y_ret = jnp.dot(a_ref[...], wr_ref[:, cols], preferred_element_type=F32)
        y_mla = jnp.dot(b_ref[...], wm_ref[:, cols], preferred_element_type=F32)
        merged = ga_ref[:, cols].astype(F32) * y_ret + gb_ref[:, cols].astype(F32) * y_mla
        o_ref[:, cols] = merged.astype(o_ref.dtype)


def _merge(a, bm, p, w_ret, w_mla, layer):
    t, d_in = a.shape
    d = w_ret.shape[2]
    tm = min(t, 1024)
    tn = 1024
    return pl.pallas_call(
        _merge_kernel,
        grid=(t // tm, d // tn),
        in_specs=[
            pl.BlockSpec((tm, d_in), lambda i, j: (i, 0)),
            pl.BlockSpec((tm, d_in), lambda i, j: (i, 0)),
            pl.BlockSpec((None, d_in, tn), lambda i, j: (layer, 0, j)),
            pl.BlockSpec((None, d_in, tn), lambda i, j: (layer, 0, j)),
            pl.BlockSpec((tm, tn), lambda i, j: (i, C_BG // tn + j)),
            pl.BlockSpec((tm, tn), lambda i, j: (i, (C_BG + d) // tn + j)),
        ],
        out_specs=pl.BlockSpec((tm, tn), lambda i, j: (i, j)),
        out_shape=jax.ShapeDtypeStruct((t, d), BF16),
        compiler_params=_params(("parallel", "arbitrary")),
        name="merge_proj",
    )(a, bm, w_ret, w_mla, p, p)


def _out_kernel(m_ref, w_ref, x_ref, gate_ref, o_ref):
    for c0 in range(0, o_ref.shape[1], PROJ_CHUNK):
        cols = slice(c0, c0 + PROJ_CHUNK)
        out = jnp.dot(m_ref[...], w_ref[:, cols], preferred_element_type=F32)
        o_ref[:, cols] = x_ref[:, cols] + gate_ref[:, cols] * out


def _out_norm_kernel(m_ref, w_ref, x_ref, gate_ref, g_ref, o_ref):
    _out_kernel(m_ref, w_ref, x_ref, gate_ref, o_ref)
    y = o_ref[...]
    inv = lax.rsqrt(jnp.mean(y * y, axis=-1, keepdims=True) + EPS)
    o_ref[...] = (y * inv) * g_ref[...]


def _out_proj(merged, w_out, x2, mod4, layer, seq, g_final=None):
    t, d = x2.shape
    fuse_norm = g_final is not None
    tm = min(seq, 512)
    per_b = seq // tm
    tn = d
    gate_blk = 2 * d // tn
    in_specs = [
        pl.BlockSpec((tm, d), lambda i, j: (i, 0)),
        pl.BlockSpec((None, d, tn), lambda i, j: (layer, 0, j)),
        pl.BlockSpec((tm, tn), lambda i, j: (i, j)),
        pl.BlockSpec((None, None, 1, tn), lambda i, j: (layer, i // per_b, 0, gate_blk + j)),
    ]
    args = [merged, w_out, x2, mod4]
    if fuse_norm:
        in_specs.append(pl.BlockSpec((1, d), lambda i, j: (0, 0)))
        args.append(g_final.reshape(1, d))
    return pl.pallas_call(
        _out_norm_kernel if fuse_norm else _out_kernel,
        grid=(t // tm, d // tn),
        in_specs=in_specs,
        out_specs=pl.BlockSpec((tm, tn), lambda i, j: (i, j)),
        out_shape=jax.ShapeDtypeStruct((t, d), F32),
        compiler_params=_params(("parallel", "arbitrary")),
        name="out_norm" if fuse_norm else "out_proj",
    )(*args)


def kernel(x, c, positions, w_mod, b_mod, g_norm, w_in, g_cq, g_ckv, w_uq, w_ukv,
           w_ret_proj, w_mla_proj, w_out, g_final):
    bsz, seq, d = x.shape
    depth = w_in.shape[0]
    t = bsz * seq
    assert seq % RET_BLOCK == 0 and seq % min(seq, ATT_BLOCK) == 0 and d % IN_TN == 0

    kr_lo = C_CKV + MLA_KV_RANK
    w_in_b = w_in.astype(BF16)
    w_lo = w_in_b
    w_hi = w_in_b[:, :, kr_lo + MLA_ROPE:]
    w_kr = jnp.pad(w_in[:, :, kr_lo:kr_lo + MLA_ROPE], ((0, 0), (0, 0), (0, LANES - MLA_ROPE))).astype(BF16)
    w_uq_p = jnp.pad(
        w_uq.reshape(depth, MLA_Q_RANK, MLA_HEADS, MLA_NOPE + MLA_ROPE),
        ((0, 0), (0, 0), (0, 0), (0, MLA_QK_PAD - MLA_NOPE - MLA_ROPE)),
    ).reshape(depth, MLA_Q_RANK, MLA_HEADS * MLA_QK_PAD).astype(BF16)
    w_ukv4 = w_ukv.reshape(depth, MLA_KV_RANK, MLA_HEADS, MLA_NOPE + MLA_DV)
    w_k = w_ukv4[..., :MLA_NOPE].reshape(depth, MLA_KV_RANK, MLA_HEADS * MLA_NOPE).astype(BF16)
    w_v = w_ukv4[..., MLA_NOPE:].reshape(depth, MLA_KV_RANK, MLA_V_W).astype(BF16)
    w_ret = w_ret_proj.astype(BF16)
    w_mla = w_mla_proj.astype(BF16)
    w_o = w_out.astype(BF16)

    tabs = _rope_tables(positions)
    mod = _modulation(c, w_mod, b_mod)
    mod4 = mod.reshape(depth, mod.shape[1], 1, 3 * d)
    g_norm3 = g_norm.reshape(depth, 1, d)
    g_cq3 = g_cq.reshape(depth, 1, MLA_Q_RANK)
    g_ckv3 = g_ckv.reshape(depth, 1, MLA_KV_RANK)

    x2 = x.reshape(t, d)
    for layer in range(depth):
        p, kr = _in_proj(x2, mod4, g_norm3, w_lo, w_hi, w_kr, tabs, g_cq3, g_ckv3, layer, seq)
        qc, kc, v = _latent_up(p, kr, w_uq_p, w_k, w_v, tabs, layer)
        a = _retention(p, bsz, seq)
        bm = _attention(qc, kc, v, p, bsz, seq)
        merged = _merge(a, bm, p, w_ret, w_mla, layer)
        last = layer == depth - 1
        x2 = _out_proj(merged, w_o, x2, mod4, layer, seq, g_final if last else None)
    return x2.reshape(bsz, seq, d)
```

```python
import jax
import jax.numpy as jnp
from jax import lax
from jax.experimental import pallas as pl
from jax.experimental.pallas import tpu as pltpu

F32 = jnp.float32
BF16 = jnp.bfloat16

CHUNK = 64
EPS = 1e-6
NEG_INF = -1e30
ROPE_BASE = 10000.0
RET_HEADS = 8
RET_DK = 128
RET_DV = 256
RET_QK_W = RET_HEADS * RET_DK
RET_V_W = RET_HEADS * RET_DV
MLA_HEADS = 16
MLA_Q_RANK = 512
MLA_KV_RANK = 512
MLA_NOPE = 128
MLA_ROPE = 64
MLA_DV = 128
MLA_V_W = MLA_HEADS * MLA_DV
MLA_QK_PAD = 256
MLA_V_PAD = 2 * MLA_DV
LOG2E = 1.4426950408889634

LANES = 128
VMEM_LIMIT = 56 * 1024 * 1024

C_RQ = 0
C_RK = C_RQ + RET_QK_W
C_RV = C_RK + RET_QK_W
C_RG = C_RV + RET_V_W
C_CQ = C_RG + RET_V_W
C_CKV = C_CQ + MLA_Q_RANK
C_MG = C_CKV + MLA_KV_RANK
C_BG = C_MG + MLA_V_W
IN_TN = 1024
IN_CHUNK = 256


def _params(sem):
    return pltpu.CompilerParams(dimension_semantics=sem, vmem_limit_bytes=VMEM_LIMIT)


def _sigmoid(x):
    return 0.5 * jnp.tanh(0.5 * x) + 0.5


def _rope_tables_kernel(pos_ref, inv_ref, cr_ref, sr_ref, cm_ref, sm1_ref, sm2_ref):
    pos = pos_ref[...].astype(F32)
    lane = lax.broadcasted_iota(jnp.int32, cr_ref.shape, 1)
    ang = pos * inv_ref[...]
    c = jnp.cos(ang)
    s = jnp.sin(ang)
    ret_half = RET_DK // 2
    half = MLA_ROPE // 2
    c_up, s_up = pltpu.roll(c, LANES - ret_half, 1), pltpu.roll(s, LANES - ret_half, 1)
    c_up2, s_up2 = pltpu.roll(c, LANES - half, 1), pltpu.roll(s, LANES - half, 1)
    cr_ref[...] = jnp.where(lane < ret_half, c, pltpu.roll(c, ret_half, 1))
    sr_ref[...] = jnp.where(lane < ret_half, -s, pltpu.roll(s, ret_half, 1))
    in_lo = lane < half
    in_hi = (lane >= half) & (lane < MLA_ROPE)
    cm_ref[...] = jnp.where(in_lo, c_up, jnp.where(in_hi, c_up2, 0.0))
    sm1_ref[...] = jnp.where(in_lo, -s_up, 0.0)
    sm2_ref[...] = jnp.where(in_hi, s_up2, 0.0)


def _rope_tables(positions):
    t = positions.size
    tm = min(t, 1024)
    inv_r = 1.0 / (ROPE_BASE ** (jnp.arange(0, RET_DK, 2, dtype=F32) / RET_DK))
    inv_m = 1.0 / (ROPE_BASE ** (jnp.arange(0, MLA_ROPE, 2, dtype=F32) / MLA_ROPE))
    inv = jnp.concatenate([inv_r, inv_m, jnp.zeros((LANES - RET_DK // 2 - MLA_ROPE // 2,), F32)])[None, :]
    row = pl.BlockSpec((tm, LANES), lambda i: (i, 0))
    return pl.pallas_call(
        _rope_tables_kernel,
        grid=(t // tm,),
        in_specs=[pl.BlockSpec((tm, 1), lambda i: (i, 0)), pl.BlockSpec((1, LANES), lambda i: (0, 0))],
        out_specs=[row] * 5,
        out_shape=[jax.ShapeDtypeStruct((t, LANES), F32)] * 5,
        compiler_params=_params(("parallel",)),
        name="rope_tables",
    )(positions.reshape(t, 1), inv)


def _mod_kernel(c_ref, w_ref, b_ref, o_ref):
    c = c_ref[...]
    ca = (c * _sigmoid(c)).astype(BF16)
    o_ref[...] = jnp.dot(ca, w_ref[...].astype(BF16), preferred_element_type=F32) + b_ref[...]


def _modulation(c, w_mod, b_mod):
    depth, d, n = w_mod.shape
    bsz = c.shape[0]
    rows = 8
    cp = jnp.pad(c, ((0, rows - bsz), (0, 0)))
    tn = 512
    return pl.pallas_call(
        _mod_kernel,
        grid=(depth, n // tn),
        in_specs=[
            pl.BlockSpec((rows, d), lambda l, j: (0, 0)),
            pl.BlockSpec((None, d, tn), lambda l, j: (l, 0, j)),
            pl.BlockSpec((None, 1, tn), lambda l, j: (l, 0, j)),
        ],
        out_specs=pl.BlockSpec((None, rows, tn), lambda l, j: (l, 0, j)),
        out_shape=jax.ShapeDtypeStruct((depth, rows, n), F32),
        compiler_params=_params(("parallel", "parallel")),
        name="modulation",
    )(cp, w_mod, b_mod.reshape(depth, 1, n))


def _rope_ret(a, cos, sin):
    outs = []
    for hh in range(a.shape[1] // RET_DK):
        t = a[:, hh * RET_DK:(hh + 1) * RET_DK]
        outs.append(t * cos + pltpu.roll(t, RET_DK // 2, 1) * sin)
    return jnp.concatenate(outs, axis=1) if len(outs) > 1 else outs[0]


def _rope_mla(t, cos, sin_lo, sin_hi):
    half = MLA_ROPE // 2
    return t * cos + pltpu.roll(t, LANES - half, 1) * sin_lo + pltpu.roll(t, half, 1) * sin_hi


IN_STEP_TILES = 2


def _in_proj_kernel(x_ref, shift_ref, scale_ref, g_ref, wa_ref, wb_ref, wkr_ref, cr_ref, sr_ref,
                    cm_ref, sm1_ref, sm2_ref, gcq_ref, gckv_ref, p_ref, kr_ref, h_scr):
    j = pl.program_id(1)

    @pl.when(j == 0)
    def _():
        x = x_ref[...]
        inv = lax.rsqrt(jnp.mean(x * x, axis=-1, keepdims=True) + EPS)
        h = (x * inv) * g_ref[...]
        h = h * (1.0 + scale_ref[...]) + shift_ref[...]
        hb = h.astype(BF16)
        h_scr[...] = hb
        kr = jnp.dot(hb, wkr_ref[...], preferred_element_type=F32)
        kr_ref[...] = _rope_mla(kr, cm_ref[...], sm1_ref[...], sm2_ref[...]).astype(kr_ref.dtype)

    def rope(acc):
        return _rope_ret(acc, cr_ref[...], sr_ref[...])

    def rope_scaled(acc):
        return rope(acc) * (RET_DK ** -0.5)

    def identity(acc):
        return acc

    def silu(acc):
        return acc * _sigmoid(acc)

    def latent_norm(g_latent_ref):
        def apply(acc):
            inv = lax.rsqrt(jnp.mean(acc * acc, axis=-1, keepdims=True) + EPS)
            return (acc * inv) * g_latent_ref[...]
        return apply

    def tile_kind(tile):
        lo = tile * IN_TN
        if lo < C_RK:
            return [rope], IN_CHUNK
        if lo < C_RV:
            return [rope_scaled], IN_CHUNK
        if lo < C_RG:
            return [identity], IN_CHUNK
        if lo < C_CQ:
            return [silu], IN_CHUNK
        if lo < C_MG:
            return [latent_norm(gcq_ref), latent_norm(gckv_ref)], MLA_Q_RANK
        if lo < C_BG:
            return [silu], IN_CHUNK
        return [_sigmoid], IN_CHUNK

    n_tiles = (C_BG + 2 * x_ref.shape[1]) // IN_TN
    for step in range(pl.cdiv(n_tiles, IN_STEP_TILES)):
        @pl.when(j == step)
        def _(step=step):
            for k, w_ref in enumerate((wa_ref, wb_ref)):
                tile = step * IN_STEP_TILES + k
                base = k * IN_TN
                if tile >= n_tiles:
                    p_ref[:, base:base + IN_TN] = jnp.zeros((p_ref.shape[0], IN_TN), p_ref.dtype)
                    continue
                epilogues, chunk = tile_kind(tile)
                for idx, c0 in enumerate(range(0, IN_TN, chunk)):
                    acc = jnp.dot(h_scr[...], w_ref[:, c0:c0 + chunk], preferred_element_type=F32)
                    p_ref[:, base + c0:base + c0 + chunk] = epilogues[idx % len(epilogues)](acc).astype(p_ref.dtype)


def _in_proj(x2, mod4, g_norm3, w_cat, w_kr, tabs, g_cq3, g_ckv3, layer, seq):
    t, d = x2.shape
    n_tiles = w_cat.shape[2] // IN_TN
    n_steps = pl.cdiv(n_tiles, IN_STEP_TILES)
    tm = min(seq, 1024)
    per_b = seq // tm
    cr, sr, cm, sm1, sm2 = tabs
    tab = pl.BlockSpec((tm, LANES), lambda i, j: (i, 0))

    def w_spec(k):
        return pl.BlockSpec((None, d, IN_TN), lambda i, j: (layer, 0, jnp.minimum(j * IN_STEP_TILES + k, n_tiles - 1)))

    return pl.pallas_call(
        _in_proj_kernel,
        grid=(t // tm, n_steps),
        in_specs=[
            pl.BlockSpec((tm, d), lambda i, j: (i, 0)),
            pl.BlockSpec((None, None, 1, d), lambda i, j: (layer, i // per_b, 0, 0)),
            pl.BlockSpec((None, None, 1, d), lambda i, j: (layer, i // per_b, 0, 1)),
            pl.BlockSpec((None, 1, d), lambda i, j: (layer, 0, 0)),
            w_spec(0), w_spec(1),
            pl.BlockSpec((None, d, LANES), lambda i, j: (layer, 0, 0)),
            tab, tab, tab, tab, tab,
            pl.BlockSpec((None, 1, MLA_Q_RANK), lambda i, j: (layer, 0, 0)),
            pl.BlockSpec((None, 1, MLA_KV_RANK), lambda i, j: (layer, 0, 0)),
        ],
        out_specs=[
            pl.BlockSpec((tm, IN_STEP_TILES * IN_TN), lambda i, j: (i, j)),
            pl.BlockSpec((tm, LANES), lambda i, j: (i, 0)),
        ],
        out_shape=[
            jax.ShapeDtypeStruct((t, n_steps * IN_STEP_TILES * IN_TN), BF16),
            jax.ShapeDtypeStruct((t, LANES), BF16),
        ],
        scratch_shapes=[pltpu.VMEM((tm, d), BF16)],
        compiler_params=_params(("parallel", "arbitrary")),
        name="in_proj",
    )(x2, mod4, mod4, g_norm3, w_cat, w_cat, w_kr, cr, sr, cm, sm1, sm2, g_cq3, g_ckv3)


UP_HEADS = MLA_HEADS


def _latent_up_kernel(cq_ref, ckv_ref, wq_ref, wk_ref, wv_ref, kr_ref, cm_ref, sm1_ref, sm2_ref,
                      q_ref, k_ref, v_ref):
    scale = (MLA_NOPE + MLA_ROPE) ** -0.5 * LOG2E
    cq = cq_ref[...]
    ckv = ckv_ref[...]
    cm, s1, s2 = cm_ref[...], sm1_ref[...], sm2_ref[...]
    kr = kr_ref[...]
    for hh in range(UP_HEADS):
        lo = hh * MLA_QK_PAD
        acc = jnp.dot(cq, wq_ref[:, lo:lo + MLA_QK_PAD], preferred_element_type=F32) * scale
        q_ref[:, lo:lo + MLA_NOPE] = acc[:, :MLA_NOPE].astype(q_ref.dtype)
        q_ref[:, lo + MLA_NOPE:lo + MLA_QK_PAD] = _rope_mla(acc[:, MLA_NOPE:], cm, s1, s2).astype(q_ref.dtype)
    ones = jnp.ones((cq.shape[0], MLA_V_PAD - MLA_DV), v_ref.dtype)
    pair_w = 2 * MLA_NOPE
    for h0 in range(0, UP_HEADS, 2):
        src = slice(h0 * MLA_NOPE, h0 * MLA_NOPE + pair_w)
        kn = jnp.dot(ckv, wk_ref[:, src], preferred_element_type=F32)
        vv = jnp.dot(ckv, wv_ref[:, src], preferred_element_type=F32)
        for hh in range(2):
            lo = (h0 + hh) * MLA_QK_PAD
            k_ref[:, lo:lo + MLA_NOPE] = kn[:, hh * MLA_NOPE:(hh + 1) * MLA_NOPE].astype(k_ref.dtype)
            k_ref[:, lo + MLA_NOPE:lo + MLA_QK_PAD] = kr
            vlo = (h0 + hh) * MLA_V_PAD
            v_ref[:, vlo:vlo + MLA_DV] = vv[:, hh * MLA_DV:(hh + 1) * MLA_DV].astype(v_ref.dtype)
            v_ref[:, vlo + MLA_DV:vlo + MLA_V_PAD] = ones


def _latent_up(p, kr, w_uq_p, w_k, w_v, tabs, layer):
    t = p.shape[0]
    tm = min(t, 512)
    _, _, cm, sm1, sm2 = tabs
    tab = pl.BlockSpec((tm, LANES), lambda i, j: (i, 0))

    def head_cols(width):
        return pl.BlockSpec((tm, UP_HEADS * width), lambda i, j: (i, j))

    return pl.pallas_call(
        _latent_up_kernel,
        grid=(t // tm, MLA_HEADS // UP_HEADS),
        in_specs=[
            pl.BlockSpec((tm, MLA_Q_RANK), lambda i, j: (i, C_CQ // MLA_Q_RANK)),
            pl.BlockSpec((tm, MLA_KV_RANK), lambda i, j: (i, C_CKV // MLA_KV_RANK)),
            pl.BlockSpec((None, MLA_Q_RANK, UP_HEADS * MLA_QK_PAD), lambda i, j: (layer, 0, j)),
            pl.BlockSpec((None, MLA_KV_RANK, UP_HEADS * MLA_NOPE), lambda i, j: (layer, 0, j)),
            pl.BlockSpec((None, MLA_KV_RANK, UP_HEADS * MLA_DV), lambda i, j: (layer, 0, j)),
            tab, tab, tab, tab,
        ],
        out_specs=[head_cols(MLA_QK_PAD), head_cols(MLA_QK_PAD), head_cols(MLA_V_PAD)],
        out_shape=[
            jax.ShapeDtypeStruct((t, MLA_HEADS * MLA_QK_PAD), BF16),
            jax.ShapeDtypeStruct((t, MLA_HEADS * MLA_QK_PAD), BF16),
            jax.ShapeDtypeStruct((t, MLA_HEADS * MLA_V_PAD), BF16),
        ],
        compiler_params=_params(("parallel", "arbitrary")),
        name="latent_up",
    )(p, p, w_uq_p, w_k, w_v, kr, cm, sm1, sm2)


RET_BLOCK = 256


RET_GROUP = 2


def _retention_kernel(q_ref, k_ref, v_ref, gate_ref, lg_ref, o_ref, state, dmat, xi, zeta):
    blk = dmat.shape[1]
    nblk = q_ref.shape[0] // blk

    state[...] = jnp.zeros_like(state)
    for hh in range(RET_GROUP):
        lg = lg_ref[hh]
        r = lax.broadcasted_iota(jnp.int32, (blk, blk), 0)
        c = lax.broadcasted_iota(jnp.int32, (blk, blk), 1)
        decay = jnp.exp(jnp.abs(r - c).astype(F32) * lg)
        dmat[hh] = jnp.where((c // CHUNK) <= (r // CHUNK), decay, 0.0)
        rx = lax.broadcasted_iota(jnp.int32, xi.shape[1:], 0).astype(F32)
        xi[hh] = jnp.exp((rx + 1.0) * lg[:, :RET_DV])
        rz = lax.broadcasted_iota(jnp.int32, zeta.shape[1:], 0).astype(F32)
        zeta[hh] = jnp.exp((blk - 1.0 - rz) * lg[:, :RET_DK])

    def body(j, carry):
        rows = pl.ds(pl.multiple_of(j * blk, blk), blk)
        for hh in range(RET_GROUP):
            qk_cols = slice(hh * RET_DK, (hh + 1) * RET_DK)
            v_cols = slice(hh * RET_DV, (hh + 1) * RET_DV)
            q = q_ref[rows, qk_cols]
            k = k_ref[rows, qk_cols]
            v = v_ref[rows, v_cols]
            s = lax.dot_general(q, k, (((1,), (1,)), ((), ())), preferred_element_type=F32)
            o = jnp.dot((s * dmat[hh]).astype(BF16), v, preferred_element_type=F32)
            st = state[hh]
            o = o + xi[hh] * jnp.dot(q, st.astype(BF16), preferred_element_type=F32)
            kz_t = (k.astype(F32) * zeta[hh]).T.astype(BF16)
            block_decay = jnp.exp(blk * lg_ref[hh][:, :RET_DV])
            state[hh] = st * block_decay + jnp.dot(kz_t, v, preferred_element_type=F32)

            mu = jnp.mean(o, axis=-1, keepdims=True)
            dlt = o - mu
            var = jnp.mean(dlt * dlt, axis=-1, keepdims=True)
            y = dlt * lax.rsqrt(var + EPS)
            o_ref[rows, v_cols] = y.astype(o_ref.dtype) * gate_ref[rows, v_cols]
        return carry

    lax.fori_loop(0, nblk, body, 0, unroll=True)


def _retention(p, bsz, seq):
    t = p.shape[0]
    blk = RET_BLOCK
    g = RET_GROUP
    log_gamma = jnp.log(1.0 - 2.0 ** (-5.0 - jnp.arange(RET_HEADS, dtype=F32)))
    lg = jnp.broadcast_to(log_gamma[:, None, None], (RET_HEADS, 1, blk))
    return pl.pallas_call(
        _retention_kernel,
        grid=(bsz, RET_HEADS // g),
        in_specs=[
            pl.BlockSpec((seq, g * RET_DK), lambda b, h: (b, C_RQ // (g * RET_DK) + h)),
            pl.BlockSpec((seq, g * RET_DK), lambda b, h: (b, C_RK // (g * RET_DK) + h)),
            pl.BlockSpec((seq, g * RET_DV), lambda b, h: (b, C_RV // (g * RET_DV) + h)),
            pl.BlockSpec((seq, g * RET_DV), lambda b, h: (b, C_RG // (g * RET_DV) + h)),
            pl.BlockSpec((g, 1, blk), lambda b, h: (h, 0, 0)),
        ],
        out_specs=pl.BlockSpec((seq, g * RET_DV), lambda b, h: (b, h)),
        out_shape=jax.ShapeDtypeStruct((t, RET_V_W), BF16),
        scratch_shapes=[
            pltpu.VMEM((g, RET_DK, RET_DV), F32),
            pltpu.VMEM((g, blk, blk), F32),
            pltpu.VMEM((g, blk, RET_DV), F32),
            pltpu.VMEM((g, blk, RET_DK), F32),
        ],
        compiler_params=_params(("parallel", "parallel")),
        name="retention",
    )(p, p, p, p, lg)


ATT_BLOCK = 512
ATT_HEADS = 2
ATT_REGION_PAIRS = 36


def _attention_kernel(q_ref, k_ref, v_ref, gate_ref, bias_ref, o_ref, s_scr, m_scr, acc_scr):
    seq = q_ref.shape[0]
    tq = s_scr.shape[1]
    nq = seq // tq

    def windows(qi, kb):
        if kb == qi and tq % (2 * CHUNK) == 0:
            return ((0, tq // 2, tq // 2), (tq // 2, tq, tq))
        return ((0, tq, tq),)

    def slot(next_pair, pair):
        for hh in range(ATT_HEADS):
            qk_cols = slice(hh * MLA_QK_PAD, (hh + 1) * MLA_QK_PAD)
            v_cols = slice(hh * MLA_V_PAD, (hh + 1) * MLA_V_PAD)
            m_cols = slice(hh * LANES, (hh + 1) * LANES)
            if pair is not None:
                qi, kb = pair
                for r0, r1, nk in windows(qi, kb):
                    s = s_scr[hh, r0:r1, :nk]
                    if kb == qi:
                        s = s + bias_ref[r0:r1, :nk]
                    m_cur = jnp.max(s, axis=-1, keepdims=True)
                    if kb == 0:
                        m_new = jnp.broadcast_to(m_cur, (r1 - r0, LANES))
                    else:
                        m_old = m_scr[r0:r1, m_cols]
                        m_new = jnp.maximum(m_old, m_cur)
                    p = jnp.exp2(s - jnp.concatenate([m_new] * (nk // LANES), axis=1))
                    v = v_ref[kb * tq:kb * tq + nk, v_cols]
                    pv = jnp.dot(p.astype(BF16), v, preferred_element_type=F32)
                    if kb == 0:
                        acc = pv
                    else:
                        alpha = jnp.concatenate([jnp.exp2(m_old - m_new)] * (MLA_V_PAD // LANES), axis=1)
                        acc = acc_scr[r0:r1, v_cols] * alpha + pv
                    if kb == qi:
                        out_rows = slice(qi * tq + r0, qi * tq + r1)
                        cols = slice(hh * MLA_DV, (hh + 1) * MLA_DV)
                        o = acc[:, :MLA_DV] / acc[:, MLA_DV:]
                        o_ref[out_rows, cols] = (o * gate_ref[out_rows, cols].astype(F32)).astype(o_ref.dtype)
                    else:
                        acc_scr[r0:r1, v_cols] = acc
                        m_scr[r0:r1, m_cols] = m_new
            if next_pair is not None:
                nqi, nkb = next_pair
                for r0, r1, nk in windows(nqi, nkb):
                    q = q_ref[nqi * tq + r0:nqi * tq + r1, qk_cols]
                    k = k_ref[nkb * tq:nkb * tq + nk, qk_cols]
                    s_scr[hh, r0:r1, :nk] = lax.dot_general(
                        q, k, (((1,), (1,)), ((), ())), preferred_element_type=F32)

    pairs = [(qi, kb) for qi in range(nq) for kb in range(qi + 1)]
    regions, current = [], []
    for qi in range(nq):
        if current and len(current) + qi + 1 > ATT_REGION_PAIRS:
            regions.append(current)
            current = []
        current += [(qi, kb) for kb in range(qi + 1)]
    regions.append(current)

    for r, region in enumerate(regions):
        @pl.when(pl.program_id(0) + r >= 0)
        def _(r=r, region=region):
            if r == 0:
                slot(pairs[0], None)
            for pair in region:
                n = pairs.index(pair)
                slot(pairs[n + 1] if n + 1 < len(pairs) else None, pair)


def _attention(qc, kc, v, p, bsz, seq):
    t = qc.shape[0]
    tq = min(seq, ATT_BLOCK)
    nh = ATT_HEADS
    r = lax.broadcasted_iota(jnp.int32, (tq, tq), 0) // CHUNK
    c = lax.broadcasted_iota(jnp.int32, (tq, tq), 1) // CHUNK
    bias = jnp.where(c <= r, 0.0, NEG_INF * LOG2E).astype(F32)
    return pl.pallas_call(
        _attention_kernel,
        grid=(bsz, MLA_HEADS // nh),
        in_specs=[
            pl.BlockSpec((seq, nh * MLA_QK_PAD), lambda b, h: (b, h)),
            pl.BlockSpec((seq, nh * MLA_QK_PAD), lambda b, h: (b, h)),
            pl.BlockSpec((seq, nh * MLA_V_PAD), lambda b, h: (b, h)),
            pl.BlockSpec((seq, nh * MLA_DV), lambda b, h: (b, C_MG // (nh * MLA_DV) + h)),
            pl.BlockSpec((tq, tq), lambda b, h: (0, 0)),
        ],
        out_specs=pl.BlockSpec((seq, nh * MLA_DV), lambda b, h: (b, h)),
        out_shape=jax.ShapeDtypeStruct((t, MLA_V_W), BF16),
        scratch_shapes=[
            pltpu.VMEM((nh, tq, tq), F32),
            pltpu.VMEM((tq, nh * LANES), F32),
            pltpu.VMEM((tq, nh * MLA_V_PAD), F32),
        ],
        compiler_params=_params(("parallel", "parallel")),
        name="attention",
    )(qc, kc, v, p, bias)


PROJ_CHUNK = 256


def _merge_kernel(a_ref, b_ref, wr_ref, wm_ref, ga_ref, gb_ref, o_ref):
    for c0 in range(0, o_ref.shape[1], PROJ_CHUNK):
        cols = slice(c0, c0 + PROJ_CHUNK)
        y_ret = jnp.dot(a_ref[...], wr_ref[:, cols], preferred_element_type=F32)
        y_mla = jnp.dot(b_ref[...], wm_ref[:, cols], preferred_element_type=F32)
        merged = ga_ref[:, cols].astype(F32) * y_ret + gb_ref[:, cols].astype(F32) * y_mla
        o_ref[:, cols] = merged.astype(o_ref.dtype)


def _merge(a, bm, p, w_ret, w_mla, layer):
    t, d_in = a.shape
    d = w_ret.shape[2]
    tm = min(t, 1024)
    tn = 1024
    return pl.pallas_call(
        _merge_kernel,
        grid=(t // tm, d // tn),
        in_specs=[
            pl.BlockSpec((tm, d_in), lambda i, j: (i, 0)),
            pl.BlockSpec((tm, d_in), lambda i, j: (i, 0)),
            pl.BlockSpec((None, d_in, tn), lambda i, j: (layer, 0, j)),
            pl.BlockSpec((None, d_in, tn), lambda i, j: (layer, 0, j)),
            pl.BlockSpec((tm, tn), lambda i, j: (i, C_BG // tn + j)),
            pl.BlockSpec((tm, tn), lambda i, j: (i, (C_BG + d) // tn + j)),
        ],
        out_specs=pl.BlockSpec((tm, tn), lambda i, j: (i, j)),
        out_shape=jax.ShapeDtypeStruct((t, d), BF16),
        compiler_params=_params(("parallel", "arbitrary")),
        name="merge_proj",
    )(a, bm, w_ret, w_mla, p, p)


def _out_kernel(m_ref, w_ref, x_ref, gate_ref, o_ref):
    for c0 in range(0, o_ref.shape[1], PROJ_CHUNK):
        cols = slice(c0, c0 + PROJ_CHUNK)
        out = jnp.dot(m_ref[...], w_ref[:, cols], preferred_element_type=F32)
        o_ref[:, cols] = x_ref[:, cols] + gate_ref[:, cols] * out


def _out_norm_kernel(m_ref, w_ref, x_ref, gate_ref, g_ref, o_ref):
    _out_kernel(m_ref, w_ref, x_ref, gate_ref, o_ref)
    y = o_ref[...]
    inv = lax.rsqrt(jnp.mean(y * y, axis=-1, keepdims=True) + EPS)
    o_ref[...] = (y * inv) * g_ref[...]


def _out_proj(merged, w_out, x2, mod4, layer, seq, g_final=None):
    t, d = x2.shape
    fuse_norm = g_final is not None
    tm = min(seq, 512)
    per_b = seq // tm
    tn = d
    gate_blk = 2 * d // tn
    in_specs = [
        pl.BlockSpec((tm, d), lambda i, j: (i, 0)),
        pl.BlockSpec((None, d, tn), lambda i, j: (layer, 0, j)),
        pl.BlockSpec((tm, tn), lambda i, j: (i, j)),
        pl.BlockSpec((None, None, 1, tn), lambda i, j: (layer, i // per_b, 0, gate_blk + j)),
    ]
    args = [merged, w_out, x2, mod4]
    if fuse_norm:
        in_specs.append(pl.BlockSpec((1, d), lambda i, j: (0, 0)))
        args.append(g_final.reshape(1, d))
    return pl.pallas_call(
        _out_norm_kernel if fuse_norm else _out_kernel,
        grid=(t // tm, d // tn),
        in_specs=in_specs,
        out_specs=pl.BlockSpec((tm, tn), lambda i, j: (i, j)),
        out_shape=jax.ShapeDtypeStruct((t, d), F32),
        compiler_params=_params(("parallel", "arbitrary")),
        name="out_norm" if fuse_norm else "out_proj",
    )(*args)


def kernel(x, c, positions, w_mod, b_mod, g_norm, w_in, g_cq, g_ckv, w_uq, w_ukv,
           w_ret_proj, w_mla_proj, w_out, g_final):
    bsz, seq, d = x.shape
    depth = w_in.shape[0]
    t = bsz * seq
    assert seq % RET_BLOCK == 0 and seq % min(seq, ATT_BLOCK) == 0 and d % IN_TN == 0

    kr_lo = C_CKV + MLA_KV_RANK
    w_cat = jnp.concatenate([w_in[:, :, :kr_lo], w_in[:, :, kr_lo + MLA_ROPE:]], axis=2).astype(BF16)
    w_kr = jnp.pad(w_in[:, :, kr_lo:kr_lo + MLA_ROPE], ((0, 0), (0, 0), (0, LANES - MLA_ROPE))).astype(BF16)
    w_uq_p = jnp.pad(
        w_uq.reshape(depth, MLA_Q_RANK, MLA_HEADS, MLA_NOPE + MLA_ROPE),
        ((0, 0), (0, 0), (0, 0), (0, MLA_QK_PAD - MLA_NOPE - MLA_ROPE)),
    ).reshape(depth, MLA_Q_RANK, MLA_HEADS * MLA_QK_PAD).astype(BF16)
    w_ukv4 = w_ukv.reshape(depth, MLA_KV_RANK, MLA_HEADS, MLA_NOPE + MLA_DV)
    w_k = w_ukv4[..., :MLA_NOPE].reshape(depth, MLA_KV_RANK, MLA_HEADS * MLA_NOPE).astype(BF16)
    w_v = w_ukv4[..., MLA_NOPE:].reshape(depth, MLA_KV_RANK, MLA_V_W).astype(BF16)
    w_ret = w_ret_proj.astype(BF16)
    w_mla = w_mla_proj.astype(BF16)
    w_o = w_out.astype(BF16)

    tabs = _rope_tables(positions)
    mod = _modulation(c, w_mod, b_mod)
    mod4 = mod.reshape(depth, mod.shape[1], 1, 3 * d)
    g_norm3 = g_norm.reshape(depth, 1, d)
    g_cq3 = g_cq.reshape(depth, 1, MLA_Q_RANK)
    g_ckv3 = g_ckv.reshape(depth, 1, MLA_KV_RANK)

    x2 = x.reshape(t, d)
    for layer in range(depth):
        p, kr = _in_proj(x2, mod4, g_norm3, w_cat, w_kr, tabs, g_cq3, g_ckv3, layer, seq)
        qc, kc, v = _latent_up(p, kr, w_uq_p, w_k, w_v, tabs, layer)
        a = _retention(p, bsz, seq)
        bm = _attention(qc, kc, v, p, bsz, seq)
        merged = _merge(a, bm, p, w_ret, w_mla, layer)
        last = layer == depth - 1
        x2 = _out_proj(merged, w_o, x2, mod4, layer, seq, g_final if last else None)
    return x2.reshape(bsz, seq, d)
```

```python
import jax
import jax.numpy as jnp
from jax import lax
from jax.experimental import pallas as pl
from jax.experimental.pallas import tpu as pltpu

F32 = jnp.float32
BF16 = jnp.bfloat16

CHUNK = 64
EPS = 1e-6
NEG_INF = -1e30
ROPE_BASE = 10000.0
RET_HEADS = 8
RET_DK = 128
RET_DV = 256
RET_QK_W = RET_HEADS * RET_DK
RET_V_W = RET_HEADS * RET_DV
MLA_HEADS = 16
MLA_Q_RANK = 512
MLA_KV_RANK = 512
MLA_NOPE = 128
MLA_ROPE = 64
MLA_DV = 128
MLA_V_W = MLA_HEADS * MLA_DV
MLA_QK_PAD = 256
MLA_V_PAD = 2 * MLA_DV
LOG2E = 1.4426950408889634

LANES = 128
VMEM_LIMIT = 56 * 1024 * 1024

C_RQ = 0
C_RK = C_RQ + RET_QK_W
C_RV = C_RK + RET_QK_W
C_RG = C_RV + RET_V_W
C_CQ = C_RG + RET_V_W
C_CKV = C_CQ + MLA_Q_RANK
C_MG = C_CKV + MLA_KV_RANK
C_BG = C_MG + MLA_V_W
IN_TN = 1024
IN_CHUNK = 256


def _params(sem):
    return pltpu.CompilerParams(dimension_semantics=sem, vmem_limit_bytes=VMEM_LIMIT)


def _sigmoid(x):
    return 0.5 * jnp.tanh(0.5 * x) + 0.5


def _rope_tables_kernel(pos_ref, inv_ref, cr_ref, sr_ref, cm_ref, sm1_ref, sm2_ref):
    pos = pos_ref[...].astype(F32)
    lane = lax.broadcasted_iota(jnp.int32, cr_ref.shape, 1)
    ang = pos * inv_ref[...]
    c = jnp.cos(ang)
    s = jnp.sin(ang)
    ret_half = RET_DK // 2
    half = MLA_ROPE // 2
    c_up, s_up = pltpu.roll(c, LANES - ret_half, 1), pltpu.roll(s, LANES - ret_half, 1)
    c_up2, s_up2 = pltpu.roll(c, LANES - half, 1), pltpu.roll(s, LANES - half, 1)
    cr_ref[...] = jnp.where(lane < ret_half, c, pltpu.roll(c, ret_half, 1))
    sr_ref[...] = jnp.where(lane < ret_half, -s, pltpu.roll(s, ret_half, 1))
    in_lo = lane < half
    in_hi = (lane >= half) & (lane < MLA_ROPE)
    cm_ref[...] = jnp.where(in_lo, c_up, jnp.where(in_hi, c_up2, 0.0))
    sm1_ref[...] = jnp.where(in_lo, -s_up, 0.0)
    sm2_ref[...] = jnp.where(in_hi, s_up2, 0.0)


def _rope_tables(positions):
    t = positions.size
    tm = min(t, 1024)
    inv_r = 1.0 / (ROPE_BASE ** (jnp.arange(0, RET_DK, 2, dtype=F32) / RET_DK))
    inv_m = 1.0 / (ROPE_BASE ** (jnp.arange(0, MLA_ROPE, 2, dtype=F32) / MLA_ROPE))
    inv = jnp.concatenate([inv_r, inv_m, jnp.zeros((LANES - RET_DK // 2 - MLA_ROPE // 2,), F32)])[None, :]
    row = pl.BlockSpec((tm, LANES), lambda i: (i, 0))
    return pl.pallas_call(
        _rope_tables_kernel,
        grid=(t // tm,),
        in_specs=[pl.BlockSpec((tm, 1), lambda i: (i, 0)), pl.BlockSpec((1, LANES), lambda i: (0, 0))],
        out_specs=[row] * 5,
        out_shape=[jax.ShapeDtypeStruct((t, LANES), F32)] * 5,
        compiler_params=_params(("parallel",)),
        name="rope_tables",
    )(positions.reshape(t, 1), inv)


def _mod_kernel(c_ref, w_ref, b_ref, o_ref):
    c = c_ref[...]
    ca = (c * _sigmoid(c)).astype(BF16)
    o_ref[...] = jnp.dot(ca, w_ref[...].astype(BF16), preferred_element_type=F32) + b_ref[...]


def _modulation(c, w_mod, b_mod):
    depth, d, n = w_mod.shape
    bsz = c.shape[0]
    rows = 8
    cp = jnp.pad(c, ((0, rows - bsz), (0, 0)))
    tn = 512
    return pl.pallas_call(
        _mod_kernel,
        grid=(depth, n // tn),
        in_specs=[
            pl.BlockSpec((rows, d), lambda l, j: (0, 0)),
            pl.BlockSpec((None, d, tn), lambda l, j: (l, 0, j)),
            pl.BlockSpec((None, 1, tn), lambda l, j: (l, 0, j)),
        ],
        out_specs=pl.BlockSpec((None, rows, tn), lambda l, j: (l, 0, j)),
        out_shape=jax.ShapeDtypeStruct((depth, rows, n), F32),
        compiler_params=_params(("parallel", "parallel")),
        name="modulation",
    )(cp, w_mod, b_mod.reshape(depth, 1, n))


def _rope_ret(a, cos, sin):
    outs = []
    for hh in range(a.shape[1] // RET_DK):
        t = a[:, hh * RET_DK:(hh + 1) * RET_DK]
        outs.append(t * cos + pltpu.roll(t, RET_DK // 2, 1) * sin)
    return jnp.concatenate(outs, axis=1) if len(outs) > 1 else outs[0]


def _rope_mla(t, cos, sin_lo, sin_hi):
    half = MLA_ROPE // 2
    return t * cos + pltpu.roll(t, LANES - half, 1) * sin_lo + pltpu.roll(t, half, 1) * sin_hi


def _in_proj_kernel(x_ref, shift_ref, scale_ref, g_ref, wlo_ref, whi_ref, wkr_ref, cr_ref, sr_ref,
                    cm_ref, sm1_ref, sm2_ref, gcq_ref, gckv_ref, p_ref, kr_ref, h_scr):
    j = pl.program_id(1)

    @pl.when(j == 0)
    def _():
        x = x_ref[...]
        inv = lax.rsqrt(jnp.mean(x * x, axis=-1, keepdims=True) + EPS)
        h = (x * inv) * g_ref[...]
        h = h * (1.0 + scale_ref[...]) + shift_ref[...]
        hb = h.astype(BF16)
        h_scr[...] = hb
        kr = jnp.dot(hb, wkr_ref[...], preferred_element_type=F32)
        kr_ref[...] = _rope_mla(kr, cm_ref[...], sm1_ref[...], sm2_ref[...]).astype(kr_ref.dtype)

    def tile_range(lo, width):
        return (j >= lo // IN_TN) & (j < (lo + width) // IN_TN)

    def project(w_ref, epilogues, chunk=IN_CHUNK):
        for idx, c0 in enumerate(range(0, IN_TN, chunk)):
            acc = jnp.dot(h_scr[...], w_ref[:, c0:c0 + chunk], preferred_element_type=F32)
            p_ref[:, c0:c0 + chunk] = epilogues[idx % len(epilogues)](acc).astype(p_ref.dtype)

    def rope(acc):
        return _rope_ret(acc, cr_ref[...], sr_ref[...])

    def latent_norm(g_latent_ref):
        def apply(acc):
            inv = lax.rsqrt(jnp.mean(acc * acc, axis=-1, keepdims=True) + EPS)
            return (acc * inv) * g_latent_ref[...]
        return apply

    @pl.when(tile_range(C_RQ, RET_QK_W))
    def _():
        project(wlo_ref, [rope])

    @pl.when(tile_range(C_RK, RET_QK_W))
    def _():
        project(wlo_ref, [lambda acc: rope(acc) * (RET_DK ** -0.5)])

    @pl.when(tile_range(C_RV, RET_V_W))
    def _():
        project(wlo_ref, [lambda acc: acc])

    def silu(acc):
        return acc * _sigmoid(acc)

    @pl.when(tile_range(C_RG, RET_V_W))
    def _():
        project(wlo_ref, [silu])

    @pl.when(tile_range(C_MG, MLA_V_W))
    def _():
        project(whi_ref, [silu])

    @pl.when(tile_range(C_CQ, MLA_Q_RANK + MLA_KV_RANK))
    def _():
        project(wlo_ref, [latent_norm(gcq_ref), latent_norm(gckv_ref)], chunk=MLA_Q_RANK)

    @pl.when(j >= C_BG // IN_TN)
    def _():
        project(whi_ref, [_sigmoid])


def _in_proj(x2, mod4, g_norm3, w_lo, w_hi, w_kr, tabs, g_cq3, g_ckv3, layer, seq):
    t, d = x2.shape
    lo_tiles = C_MG // IN_TN
    n = C_MG + w_hi.shape[2]
    tm = min(seq, 1024)
    per_b = seq // tm
    cr, sr, cm, sm1, sm2 = tabs
    tab = pl.BlockSpec((tm, LANES), lambda i, j: (i, 0))
    return pl.pallas_call(
        _in_proj_kernel,
        grid=(t // tm, n // IN_TN),
        in_specs=[
            pl.BlockSpec((tm, d), lambda i, j: (i, 0)),
            pl.BlockSpec((None, None, 1, d), lambda i, j: (layer, i // per_b, 0, 0)),
            pl.BlockSpec((None, None, 1, d), lambda i, j: (layer, i // per_b, 0, 1)),
            pl.BlockSpec((None, 1, d), lambda i, j: (layer, 0, 0)),
            pl.BlockSpec((None, d, IN_TN), lambda i, j: (layer, 0, jnp.minimum(j, lo_tiles - 1))),
            pl.BlockSpec((None, d, IN_TN), lambda i, j: (layer, 0, jnp.maximum(j - lo_tiles, 0))),
            pl.BlockSpec((None, d, LANES), lambda i, j: (layer, 0, 0)),
            tab, tab, tab, tab, tab,
            pl.BlockSpec((None, 1, MLA_Q_RANK), lambda i, j: (layer, 0, 0)),
            pl.BlockSpec((None, 1, MLA_KV_RANK), lambda i, j: (layer, 0, 0)),
        ],
        out_specs=[
            pl.BlockSpec((tm, IN_TN), lambda i, j: (i, j)),
            pl.BlockSpec((tm, LANES), lambda i, j: (i, 0)),
        ],
        out_shape=[
            jax.ShapeDtypeStruct((t, n), BF16),
            jax.ShapeDtypeStruct((t, LANES), BF16),
        ],
        scratch_shapes=[pltpu.VMEM((tm, d), BF16)],
        compiler_params=_params(("parallel", "arbitrary")),
        name="in_proj",
    )(x2, mod4, mod4, g_norm3, w_lo, w_hi, w_kr, cr, sr, cm, sm1, sm2, g_cq3, g_ckv3)


UP_HEADS = MLA_HEADS


def _latent_up_kernel(cq_ref, ckv_ref, wq_ref, wk_ref, wv_ref, kr_ref, cm_ref, sm1_ref, sm2_ref,
                      q_ref, k_ref, v_ref):
    scale = (MLA_NOPE + MLA_ROPE) ** -0.5 * LOG2E
    cq = cq_ref[...]
    ckv = ckv_ref[...]
    cm, s1, s2 = cm_ref[...], sm1_ref[...], sm2_ref[...]
    kr = kr_ref[...]
    for hh in range(UP_HEADS):
        lo = hh * MLA_QK_PAD
        acc = jnp.dot(cq, wq_ref[:, lo:lo + MLA_QK_PAD], preferred_element_type=F32) * scale
        q_ref[:, lo:lo + MLA_NOPE] = acc[:, :MLA_NOPE].astype(q_ref.dtype)
        q_ref[:, lo + MLA_NOPE:lo + MLA_QK_PAD] = _rope_mla(acc[:, MLA_NOPE:], cm, s1, s2).astype(q_ref.dtype)
    ones = jnp.ones((cq.shape[0], MLA_V_PAD - MLA_DV), v_ref.dtype)
    pair_w = 2 * MLA_NOPE
    for h0 in range(0, UP_HEADS, 2):
        src = slice(h0 * MLA_NOPE, h0 * MLA_NOPE + pair_w)
        kn = jnp.dot(ckv, wk_ref[:, src], preferred_element_type=F32)
        vv = jnp.dot(ckv, wv_ref[:, src], preferred_element_type=F32)
        for hh in range(2):
            lo = (h0 + hh) * MLA_QK_PAD
            k_ref[:, lo:lo + MLA_NOPE] = kn[:, hh * MLA_NOPE:(hh + 1) * MLA_NOPE].astype(k_ref.dtype)
            k_ref[:, lo + MLA_NOPE:lo + MLA_QK_PAD] = kr
            vlo = (h0 + hh) * MLA_V_PAD
            v_ref[:, vlo:vlo + MLA_DV] = vv[:, hh * MLA_DV:(hh + 1) * MLA_DV].astype(v_ref.dtype)
            v_ref[:, vlo + MLA_DV:vlo + MLA_V_PAD] = ones


def _latent_up(p, kr, w_uq_p, w_k, w_v, tabs, layer):
    t = p.shape[0]
    tm = min(t, 512)
    _, _, cm, sm1, sm2 = tabs
    tab = pl.BlockSpec((tm, LANES), lambda i, j: (i, 0))

    def head_cols(width):
        return pl.BlockSpec((tm, UP_HEADS * width), lambda i, j: (i, j))

    return pl.pallas_call(
        _latent_up_kernel,
        grid=(t // tm, MLA_HEADS // UP_HEADS),
        in_specs=[
            pl.BlockSpec((tm, MLA_Q_RANK), lambda i, j: (i, C_CQ // MLA_Q_RANK)),
            pl.BlockSpec((tm, MLA_KV_RANK), lambda i, j: (i, C_CKV // MLA_KV_RANK)),
            pl.BlockSpec((None, MLA_Q_RANK, UP_HEADS * MLA_QK_PAD), lambda i, j: (layer, 0, j)),
            pl.BlockSpec((None, MLA_KV_RANK, UP_HEADS * MLA_NOPE), lambda i, j: (layer, 0, j)),
            pl.BlockSpec((None, MLA_KV_RANK, UP_HEADS * MLA_DV), lambda i, j: (layer, 0, j)),
            tab, tab, tab, tab,
        ],
        out_specs=[head_cols(MLA_QK_PAD), head_cols(MLA_QK_PAD), head_cols(MLA_V_PAD)],
        out_shape=[
            jax.ShapeDtypeStruct((t, MLA_HEADS * MLA_QK_PAD), BF16),
            jax.ShapeDtypeStruct((t, MLA_HEADS * MLA_QK_PAD), BF16),
            jax.ShapeDtypeStruct((t, MLA_HEADS * MLA_V_PAD), BF16),
        ],
        compiler_params=_params(("parallel", "arbitrary")),
        name="latent_up",
    )(p, p, w_uq_p, w_k, w_v, kr, cm, sm1, sm2)


RET_BLOCK = 256


RET_GROUP = 2


def _retention_kernel(q_ref, k_ref, v_ref, gate_ref, lg_ref, o_ref, state, dmat, xi, zeta):
    blk = dmat.shape[1]
    nblk = q_ref.shape[0] // blk

    state[...] = jnp.zeros_like(state)
    for hh in range(RET_GROUP):
        lg = lg_ref[hh]
        r = lax.broadcasted_iota(jnp.int32, (blk, blk), 0)
        c = lax.broadcasted_iota(jnp.int32, (blk, blk), 1)
        decay = jnp.exp(jnp.abs(r - c).astype(F32) * lg)
        dmat[hh] = jnp.where((c // CHUNK) <= (r // CHUNK), decay, 0.0)
        rx = lax.broadcasted_iota(jnp.int32, xi.shape[1:], 0).astype(F32)
        xi[hh] = jnp.exp((rx + 1.0) * lg[:, :RET_DV])
        rz = lax.broadcasted_iota(jnp.int32, zeta.shape[1:], 0).astype(F32)
        zeta[hh] = jnp.exp((blk - 1.0 - rz) * lg[:, :RET_DK])

    def body(j, carry):
        rows = pl.ds(pl.multiple_of(j * blk, blk), blk)
        for hh in range(RET_GROUP):
            qk_cols = slice(hh * RET_DK, (hh + 1) * RET_DK)
            v_cols = slice(hh * RET_DV, (hh + 1) * RET_DV)
            q = q_ref[rows, qk_cols]
            k = k_ref[rows, qk_cols]
            v = v_ref[rows, v_cols]
            s = lax.dot_general(q, k, (((1,), (1,)), ((), ())), preferred_element_type=F32)
            o = jnp.dot((s * dmat[hh]).astype(BF16), v, preferred_element_type=F32)
            st = state[hh]
            o = o + xi[hh] * jnp.dot(q, st.astype(BF16), preferred_element_type=F32)
            kz_t = (k.astype(F32) * zeta[hh]).T.astype(BF16)
            block_decay = jnp.exp(blk * lg_ref[hh][:, :RET_DV])
            state[hh] = st * block_decay + jnp.dot(kz_t, v, preferred_element_type=F32)

            mu = jnp.mean(o, axis=-1, keepdims=True)
            dlt = o - mu
            var = jnp.mean(dlt * dlt, axis=-1, keepdims=True)
            y = dlt * lax.rsqrt(var + EPS)
            o_ref[rows, v_cols] = y.astype(o_ref.dtype) * gate_ref[rows, v_cols]
        return carry

    lax.fori_loop(0, nblk, body, 0, unroll=True)


def _retention(p, bsz, seq):
    t = p.shape[0]
    blk = RET_BLOCK
    g = RET_GROUP
    log_gamma = jnp.log(1.0 - 2.0 ** (-5.0 - jnp.arange(RET_HEADS, dtype=F32)))
    lg = jnp.broadcast_to(log_gamma[:, None, None], (RET_HEADS, 1, blk))
    return pl.pallas_call(
        _retention_kernel,
        grid=(bsz, RET_HEADS // g),
        in_specs=[
            pl.BlockSpec((seq, g * RET_DK), lambda b, h: (b, C_RQ // (g * RET_DK) + h)),
            pl.BlockSpec((seq, g * RET_DK), lambda b, h: (b, C_RK // (g * RET_DK) + h)),
            pl.BlockSpec((seq, g * RET_DV), lambda b, h: (b, C_RV // (g * RET_DV) + h)),
            pl.BlockSpec((seq, g * RET_DV), lambda b, h: (b, C_RG // (g * RET_DV) + h)),
            pl.BlockSpec((g, 1, blk), lambda b, h: (h, 0, 0)),
        ],
        out_specs=pl.BlockSpec((seq, g * RET_DV), lambda b, h: (b, h)),
        out_shape=jax.ShapeDtypeStruct((t, RET_V_W), BF16),
        scratch_shapes=[
            pltpu.VMEM((g, RET_DK, RET_DV), F32),
            pltpu.VMEM((g, blk, blk), F32),
            pltpu.VMEM((g, blk, RET_DV), F32),
            pltpu.VMEM((g, blk, RET_DK), F32),
        ],
        compiler_params=_params(("parallel", "parallel")),
        name="retention",
    )(p, p, p, p, lg)


ATT_BLOCK = 512
ATT_HEADS = 2
ATT_REGION_PAIRS = 36


def _attention_kernel(q_ref, k_ref, v_ref, gate_ref, bias_ref, o_ref, s_scr, m_scr, acc_scr):
    seq = q_ref.shape[0]
    tq = s_scr.shape[1]
    nq = seq // tq

    def windows(qi, kb):
        if kb == qi and tq % (2 * CHUNK) == 0:
            return ((0, tq // 2, tq // 2), (tq // 2, tq, tq))
        return ((0, tq, tq),)

    def slot(next_pair, pair, n):
        for hh in range(ATT_HEADS):
            qk_cols = slice(hh * MLA_QK_PAD, (hh + 1) * MLA_QK_PAD)
            v_cols = slice(hh * MLA_V_PAD, (hh + 1) * MLA_V_PAD)
            m_cols = slice(hh * LANES, (hh + 1) * LANES)
            cur_s = (n % 2) * ATT_HEADS + hh
            next_s = ((n + 1) % 2) * ATT_HEADS + hh
            if pair is not None:
                qi, kb = pair
                for r0, r1, nk in windows(qi, kb):
                    s = s_scr[cur_s, r0:r1, :nk]
                    if kb == qi:
                        s = s + bias_ref[r0:r1, :nk]
                    m_cur = jnp.max(s, axis=-1, keepdims=True)
                    if kb == 0:
                        m_new = jnp.broadcast_to(m_cur, (r1 - r0, LANES))
                    else:
                        m_old = m_scr[r0:r1, m_cols]
                        m_new = jnp.maximum(m_old, m_cur)
                    p = jnp.exp2(s - jnp.concatenate([m_new] * (nk // LANES), axis=1))
                    v = v_ref[kb * tq:kb * tq + nk, v_cols]
                    pv = jnp.dot(p.astype(BF16), v, preferred_element_type=F32)
                    if kb == 0:
                        acc = pv
                    else:
                        alpha = jnp.concatenate([jnp.exp2(m_old - m_new)] * (MLA_V_PAD // LANES), axis=1)
                        acc = acc_scr[r0:r1, v_cols] * alpha + pv
                    if kb == qi:
                        out_rows = slice(qi * tq + r0, qi * tq + r1)
                        cols = slice(hh * MLA_DV, (hh + 1) * MLA_DV)
                        o = acc[:, :MLA_DV] / acc[:, MLA_DV:]
                        o_ref[out_rows, cols] = (o * gate_ref[out_rows, cols].astype(F32)).astype(o_ref.dtype)
                    else:
                        acc_scr[r0:r1, v_cols] = acc
                        m_scr[r0:r1, m_cols] = m_new
            if next_pair is not None:
                nqi, nkb = next_pair
                for r0, r1, nk in windows(nqi, nkb):
                    q = q_ref[nqi * tq + r0:nqi * tq + r1, qk_cols]
                    k = k_ref[nkb * tq:nkb * tq + nk, qk_cols]
                    s_scr[next_s, r0:r1, :nk] = lax.dot_general(
                        q, k, (((1,), (1,)), ((), ())), preferred_element_type=F32)

    pairs = [(qi, kb) for qi in range(nq) for kb in range(qi + 1)]
    regions, current = [], []
    for qi in range(nq):
        if current and len(current) + qi + 1 > ATT_REGION_PAIRS:
            regions.append(current)
            current = []
        current += [(qi, kb) for kb in range(qi + 1)]
    regions.append(current)

    for r, region in enumerate(regions):
        @pl.when(pl.program_id(0) + r >= 0)
        def _(r=r, region=region):
            if r == 0:
                slot(pairs[0], None, -1)
            for pair in region:
                n = pairs.index(pair)
                slot(pairs[n + 1] if n + 1 < len(pairs) else None, pair, n)


def _attention(qc, kc, v, p, bsz, seq):
    t = qc.shape[0]
    tq = min(seq, ATT_BLOCK)
    nh = ATT_HEADS
    r = lax.broadcasted_iota(jnp.int32, (tq, tq), 0) // CHUNK
    c = lax.broadcasted_iota(jnp.int32, (tq, tq), 1) // CHUNK
    bias = jnp.where(c <= r, 0.0, NEG_INF * LOG2E).astype(F32)
    return pl.pallas_call(
        _attention_kernel,
        grid=(bsz, MLA_HEADS // nh),
        in_specs=[
            pl.BlockSpec((seq, nh * MLA_QK_PAD), lambda b, h: (b, h)),
            pl.BlockSpec((seq, nh * MLA_QK_PAD), lambda b, h: (b, h)),
            pl.BlockSpec((seq, nh * MLA_V_PAD), lambda b, h: (b, h)),
            pl.BlockSpec((seq, nh * MLA_DV), lambda b, h: (b, C_MG // (nh * MLA_DV) + h)),
            pl.BlockSpec((tq, tq), lambda b, h: (0, 0)),
        ],
        out_specs=pl.BlockSpec((seq, nh * MLA_DV), lambda b, h: (b, h)),
        out_shape=jax.ShapeDtypeStruct((t, MLA_V_W), BF16),
        scratch_shapes=[
            pltpu.VMEM((2 * nh, tq, tq), F32),
            pltpu.VMEM((tq, nh * LANES), F32),
            pltpu.VMEM((tq, nh * MLA_V_PAD), F32),
        ],
        compiler_params=_params(("parallel", "parallel")),
        name="attention",
    )(qc, kc, v, p, bias)


PROJ_CHUNK = 256


def _merge_kernel(a_ref, b_ref, wr_ref, wm_ref, ga_ref, gb_ref, o_ref):
    for c0 in range(0, o_ref.shape[1], PROJ_CHUNK):
        cols = slice(c0, c0 + PROJ_CHUNK)
        y_ret = jnp.dot(a_ref[...], wr_ref[:, cols], preferred_element_type=F32)
        y_mla = jnp.dot(b_ref[...], wm_ref[:, cols], preferred_element_type=F32)
        merged = ga_ref[:, cols].astype(F32) * y_ret + gb_ref[:, cols].astype(F32) * y_mla
        o_ref[:, cols] = merged.astype(o_ref.dtype)


def _merge(a, bm, p, w_ret, w_mla, layer):
    t, d_in = a.shape
    d = w_ret.shape[2]
    tm = min(t, 1024)
    tn = 1024
    return pl.pallas_call(
        _merge_kernel,
        grid=(t // tm, d // tn),
        in_specs=[
            pl.BlockSpec((tm, d_in), lambda i, j: (i, 0)),
            pl.BlockSpec((tm, d_in), lambda i, j: (i, 0)),
            pl.BlockSpec((None, d_in, tn), lambda i, j: (layer, 0, j)),
            pl.BlockSpec((None, d_in, tn), lambda i, j: (layer, 0, j)),
            pl.BlockSpec((tm, tn), lambda i, j: (i, C_BG // tn + j)),
            pl.BlockSpec((tm, tn), lambda i, j: (i, (C_BG + d) // tn + j)),
        ],
        out_specs=pl.BlockSpec((tm, tn), lambda i, j: (i, j)),
        out_shape=jax.ShapeDtypeStruct((t, d), BF16),
        compiler_params=_params(("parallel", "arbitrary")),
        name="merge_proj",
    )(a, bm, w_ret, w_mla, p, p)


def _out_kernel(m_ref, w_ref, x_ref, gate_ref, o_ref):
    for c0 in range(0, o_ref.shape[1], PROJ_CHUNK):
        cols = slice(c0, c0 + PROJ_CHUNK)
        out = jnp.dot(m_ref[...], w_ref[:, cols], preferred_element_type=F32)
        o_ref[:, cols] = x_ref[:, cols] + gate_ref[:, cols] * out


def _out_norm_kernel(m_ref, w_ref, x_ref, gate_ref, g_ref, o_ref):
    _out_kernel(m_ref, w_ref, x_ref, gate_ref, o_ref)
    y = o_ref[...]
    inv = lax.rsqrt(jnp.mean(y * y, axis=-1, keepdims=True) + EPS)
    o_ref[...] = (y * inv) * g_ref[...]


def _out_proj(merged, w_out, x2, mod4, layer, seq, g_final=None):
    t, d = x2.shape
    fuse_norm = g_final is not None
    tm = min(seq, 512)
    per_b = seq // tm
    tn = d
    gate_blk = 2 * d // tn
    in_specs = [
        pl.BlockSpec((tm, d), lambda i, j: (i, 0)),
        pl.BlockSpec((None, d, tn), lambda i, j: (layer, 0, j)),
        pl.BlockSpec((tm, tn), lambda i, j: (i, j)),
        pl.BlockSpec((None, None, 1, tn), lambda i, j: (layer, i // per_b, 0, gate_blk + j)),
    ]
    args = [merged, w_out, x2, mod4]
    if fuse_norm:
        in_specs.append(pl.BlockSpec((1, d), lambda i, j: (0, 0)))
        args.append(g_final.reshape(1, d))
    return pl.pallas_call(
        _out_norm_kernel if fuse_norm else _out_kernel,
        grid=(t // tm, d // tn),
        in_specs=in_specs,
        out_specs=pl.BlockSpec((tm, tn), lambda i, j: (i, j)),
        out_shape=jax.ShapeDtypeStruct((t, d), F32),
        compiler_params=_params(("parallel", "arbitrary")),
        name="out_norm" if fuse_norm else "out_proj",
    )(*args)


def kernel(x, c, positions, w_mod, b_mod, g_norm, w_in, g_cq, g_ckv, w_uq, w_ukv,
           w_ret_proj, w_mla_proj, w_out, g_final):
    bsz, seq, d = x.shape
    depth = w_in.shape[0]
    t = bsz * seq
    assert seq % RET_BLOCK == 0 and seq % min(seq, ATT_BLOCK) == 0 and d % IN_TN == 0

    kr_lo = C_CKV + MLA_KV_RANK
    w_in_b = w_in.astype(BF16)
    w_lo = w_in_b
    w_hi = w_in_b[:, :, kr_lo + MLA_ROPE:]
    w_kr = jnp.pad(w_in[:, :, kr_lo:kr_lo + MLA_ROPE], ((0, 0), (0, 0), (0, LANES - MLA_ROPE))).astype(BF16)
    w_uq_p = jnp.pad(
        w_uq.reshape(depth, MLA_Q_RANK, MLA_HEADS, MLA_NOPE + MLA_ROPE),
        ((0, 0), (0, 0), (0, 0), (0, MLA_QK_PAD - MLA_NOPE - MLA_ROPE)),
    ).reshape(depth, MLA_Q_RANK, MLA_HEADS * MLA_QK_PAD).astype(BF16)
    w_ukv4 = w_ukv.reshape(depth, MLA_KV_RANK, MLA_HEADS, MLA_NOPE + MLA_DV)
    w_k = w_ukv4[..., :MLA_NOPE].reshape(depth, MLA_KV_RANK, MLA_HEADS * MLA_NOPE).astype(BF16)
    w_v = w_ukv4[..., MLA_NOPE:].reshape(depth, MLA_KV_RANK, MLA_V_W).astype(BF16)
    w_ret = w_ret_proj.astype(BF16)
    w_mla = w_mla_proj.astype(BF16)
    w_o = w_out.astype(BF16)

    tabs = _rope_tables(positions)
    mod = _modulation(c, w_mod, b_mod)
    mod4 = mod.reshape(depth, mod.shape[1], 1, 3 * d)
    g_norm3 = g_norm.reshape(depth, 1, d)
    g_cq3 = g_cq.reshape(depth, 1, MLA_Q_RANK)
    g_ckv3 = g_ckv.reshape(depth, 1, MLA_KV_RANK)

    x2 = x.reshape(t, d)
    for layer in range(depth):
        p, kr = _in_proj(x2, mod4, g_norm3, w_lo, w_hi, w_kr, tabs, g_cq3, g_ckv3, layer, seq)
        qc, kc, v = _latent_up(p, kr, w_uq_p, w_k, w_v, tabs, layer)
        a = _retention(p, bsz, seq)
        bm = _attention(qc, kc, v, p, bsz, seq)
        merged = _merge(a, bm, p, w_ret, w_mla, layer)
        last = layer == depth - 1
        x2 = _out_proj(merged, w_o, x2, mod4, layer, seq, g_final if last else None)
    return x2.reshape(bsz, seq, d)
```
